```python
import math
import jax, jax.numpy as jnp
from jax import lax
import numpy as np

D_MODEL = 2048
BATCH = 16
SEQ = 2048
DEPTH = 2

F32 = jnp.float32

GROUP_WIDTH = D_MODEL // 4
MLA_HEADS = 4
MLA_NOPE = 128
MLA_ROPE = 64
MLA_V = GROUP_WIDTH // MLA_HEADS
MLA_Q_LORA = 384
MLA_KV_LORA = 256
ATTN_BLOCK = 128
RET_HEADS = 4
RET_DV = GROUP_WIDTH // RET_HEADS
RET_DK = RET_DV // 2
RET_CHUNK = 128
RET_DECAY_EXP_FWD = 5.0
RET_DECAY_EXP_BWD = 5.5
SSD_HEADDIM = 64
SSD_HEADS = GROUP_WIDTH // SSD_HEADDIM
SSD_GROUPS = 2
SSD_STATE = 128
SSD_CONV = 5
SSD_CHUNK = 128
SSD_CONV_CH = GROUP_WIDTH + 2 * SSD_GROUPS * SSD_STATE
HY_ORDER = 2
HY_WIDTH = GROUP_WIDTH
HY_SHORT = 3
HY_EMB = 33
HY_FILTER_HIDDEN = 64
HY_TARGET = 1e-2
HY_FAST_PCT = 0.3
HY_SLOW_PCT = 1.5
HY_MIN_DECAY = math.log(HY_TARGET) / HY_SLOW_PCT
HY_MAX_DECAY = math.log(HY_TARGET) / HY_FAST_PCT
D_FF = 5632
N_EXPERTS = 8
TOP_K = 2
D_FF_EXPERT = 7168
MOE_BLOCK = 256
ROPE_BASE = 10000.0
ALPHA = (2 * DEPTH) ** 0.25
BETA = (8 * DEPTH) ** -0.25
N_DENSE = (DEPTH + 1) // 2
N_MOE = DEPTH // 2

IN_SPLITS = (MLA_Q_LORA, MLA_KV_LORA, MLA_ROPE,
             RET_HEADS * RET_DK, RET_HEADS * RET_DK, GROUP_WIDTH, GROUP_WIDTH,
             GROUP_WIDTH, SSD_CONV_CH, 2 * SSD_HEADS,
             (HY_ORDER + 1) * HY_WIDTH)
IN_COLS = sum(IN_SPLITS)

kernel_name = 'hymba_style_mla_retnet_ssd_hyena_deepnorm_moe'


def rms_norm(x, w, eps=1e-6):
    xf = x.astype(F32)
    y = xf * lax.rsqrt(jnp.mean(xf * xf, axis=-1, keepdims=True) + eps)
    return (y * w.astype(F32)).astype(x.dtype)


def layer_norm(x, g, b, eps=1e-5):
    xf = x.astype(F32)
    mu = jnp.mean(xf, axis=-1, keepdims=True)
    var = jnp.mean(jnp.square(xf - mu), axis=-1, keepdims=True)
    return ((xf - mu) * lax.rsqrt(var + eps) * g.astype(F32) + b.astype(F32)).astype(x.dtype)


def rotary(x):
    s, d = x.shape[1], x.shape[-1]
    half = d // 2
    inv_freq = ROPE_BASE ** (-jnp.arange(half, dtype=F32) * 2.0 / d)
    ang = jnp.arange(s, dtype=F32)[:, None] * inv_freq[None, :]
    cos = jnp.cos(ang)[None, :, None, :]
    sin = jnp.sin(ang)[None, :, None, :]
    xf = x.astype(F32)
    x1, x2 = xf[..., :half], xf[..., half:]
    return jnp.concatenate([x1 * cos - x2 * sin, x2 * cos + x1 * sin], axis=-1).astype(x.dtype)


def depthwise_conv(x, w, b):
    k = w.shape[0]
    y = lax.conv_general_dilated(x, w[:, None, :].astype(x.dtype), window_strides=(1,),
                                 padding=[(k // 2, k // 2)],
                                 dimension_numbers=('NWC', 'WIO', 'NWC'),
                                 feature_group_count=x.shape[-1])
    return y + b.astype(x.dtype)


def mla_attention(q_nope, q_pe, k_nope, k_pe, v):
    bsz, s, h, _ = q_nope.shape
    nq = s // ATTN_BLOCK
    scale = (MLA_NOPE + MLA_ROPE) ** -0.5

    def to_blocks(t):
        return jnp.moveaxis(t.reshape(bsz, nq, ATTN_BLOCK, h, t.shape[-1]), 1, 0)

    def attend(blk):
        qn, qp = blk
        sc = (jnp.einsum('bqhd,bkhd->bhqk', qn, k_nope)
              + jnp.einsum('bqhr,bkr->bhqk', qp, k_pe))
        p = jax.nn.softmax(sc.astype(F32) * scale, axis=-1)
        return jnp.einsum('bhqk,bkhe->bqhe', p.astype(v.dtype), v)

    out = lax.map(attend, (to_blocks(q_nope), to_blocks(q_pe)))
    return jnp.moveaxis(out, 0, 1).reshape(bsz, s, h * v.shape[-1])


def retention_one_direction(q, k, v, log_gamma, include_diag):
    bsz, s, h, dk = q.shape
    dv = v.shape[-1]
    n = RET_CHUNK
    c = s // n
    qc = q.reshape(bsz, c, n, h, dk)
    kc = k.reshape(bsz, c, n, h, dk)
    vc = v.reshape(bsz, c, n, h, dv)
    idx = jnp.arange(n, dtype=F32)
    diff = idx[:, None] - idx[None, :]
    mask = (diff >= 0) if include_diag else (diff > 0)
    decay = jnp.where(mask[..., None], jnp.exp(jnp.where(mask, diff, 0.0)[..., None] * log_gamma), 0.0)
    scores = jnp.einsum('bcihd,bcjhd->bchij', qc, kc) * jnp.moveaxis(decay, -1, 0)
    inner = jnp.einsum('bchij,bcjhe->bcihe', scores, vc)
    k_decay = jnp.exp((n - 1.0 - idx)[:, None] * log_gamma)
    states = jnp.einsum('bcjhd,jh,bcjhe->bchde', kc, k_decay, vc)
    chunk_decay = jnp.exp(n * log_gamma)[:, None, None]

    def step(carry, st):
        return carry * chunk_decay + st, carry

    _, prev = lax.scan(step, jnp.zeros((bsz, h, dk, dv), F32), jnp.moveaxis(states, 1, 0))
    prev = jnp.moveaxis(prev, 0, 1)
    q_decay = jnp.exp((idx + 1.0)[:, None] * log_gamma)
    cross = jnp.einsum('bcihd,ih,bchde->bcihe', qc, q_decay, prev)
    return (inner + cross).reshape(bsz, s, h, dv)


def bidirectional_retention(q, k, v):
    q, k, v = (t.astype(F32) for t in (q, k, v))
    bsz, s, h, dv = v.shape
    heads = jnp.arange(RET_HEADS, dtype=F32)
    lg_fwd = jnp.log1p(-jnp.exp2(-RET_DECAY_EXP_FWD - heads))
    lg_bwd = jnp.log1p(-jnp.exp2(-RET_DECAY_EXP_BWD - heads))
    y = (retention_one_direction(q, k, v, lg_fwd, True)
         + jnp.flip(retention_one_direction(jnp.flip(q, 1), jnp.flip(k, 1), jnp.flip(v, 1), lg_bwd, False), 1))
    mu = jnp.mean(y, axis=-1, keepdims=True)
    var = jnp.mean(jnp.square(y - mu), axis=-1, keepdims=True)
    y = (y - mu) * lax.rsqrt(var + 1e-6)
    return y.reshape(bsz, s, h * dv)


def ssd_chunked(x, dt, a, bm, cm, include_diag):
    bsz, s, h, p = x.shape
    g, n = bm.shape[2], bm.shape[3]
    e = h // g
    q = SSD_CHUNK
    c = s // q
    xc = (x * dt[..., None]).reshape(bsz, c, q, g, e, p)
    la = (dt * a).reshape(bsz, c, q, g, e)
    bc = bm.reshape(bsz, c, q, g, n)
    cc = cm.reshape(bsz, c, q, g, n)
    cs = jnp.cumsum(la, axis=2)
    idx = jnp.arange(q)
    mask = (idx[:, None] >= idx[None, :]) if include_diag else (idx[:, None] > idx[None, :])
    mask6 = mask[:, :, None, None]
    seg = cs[:, :, :, None] - cs[:, :, None, :]
    lmat = jnp.where(mask6, jnp.exp(jnp.where(mask6, seg, 0.0)), 0.0)
    cb = jnp.einsum('bcign,bcjgn->bcijg', cc, bc)
    y_diag = jnp.einsum('bcijg,bcijge,bcjgep->bcigep', cb, lmat, xc)
    to_end = jnp.exp(cs[:, :, -1:] - cs)
    states = jnp.einsum('bcjgn,bcjge,bcjgep->bcgepn', bc, to_end, xc)
    chunk_decay = jnp.exp(cs[:, :, -1])

    def step(carry, inp):
        st, dec = inp
        return carry * dec[..., None, None] + st, carry

    init = jnp.zeros((bsz, g, e, p, n), F32)
    _, prev = lax.scan(step, init, (jnp.moveaxis(states, 1, 0), jnp.moveaxis(chunk_decay, 1, 0)))
    prev = jnp.moveaxis(prev, 0, 1)
    y_off = jnp.einsum('bcign,bcige,bcgepn->bcigep', cc, jnp.exp(cs), prev)
    return (y_diag + y_off).reshape(bsz, s, h, p)


def hyena_filter_spectrum(seq_len, w1, b1, w2, b2, w3, freq):
    t = jnp.linspace(0.0, 1.0, seq_len, dtype=F32)[:, None]
    bands = (HY_EMB - 1) // 2
    ang = 2.0 * math.pi * jnp.arange(seq_len, dtype=F32)[:, None] / seq_len
    f = jnp.linspace(1e-4, bands - 1, bands, dtype=F32)[None, :]
    z = jnp.concatenate([t, jnp.cos(f * ang), -jnp.sin(f * ang)], axis=-1)
    freq = freq.astype(F32)
    hid = jnp.sin(freq[0] * (z @ w1.astype(F32) + b1.astype(F32)))
    hid = jnp.sin(freq[1] * (hid @ w2.astype(F32) + b2.astype(F32)))
    filt = (hid @ w3.astype(F32)).reshape(seq_len, HY_ORDER, 2, HY_WIDTH)
    deltas = jnp.abs(jnp.linspace(HY_MIN_DECAY, HY_MAX_DECAY, HY_WIDTH, dtype=F32))
    filt = filt * jnp.exp(-t * deltas)[:, None, None, :]
    h_fwd, h_bwd = filt[:, :, 0], filt[:, :, 1]
    h_full = jnp.concatenate([h_fwd, jnp.zeros((1, HY_ORDER, HY_WIDTH), F32), jnp.flip(h_bwd[1:], 0)], axis=0)
    return jnp.fft.rfft(h_full, axis=0)


def fft_long_conv(u, h_freq, bias):
    seq_len = u.shape[1]
    spec = jnp.fft.rfft(u, n=2 * seq_len, axis=1) * h_freq[None]
    return jnp.fft.irfft(spec, n=2 * seq_len, axis=1)[:, :seq_len] + u * bias.astype(F32)


def hybrid_mixer(x, w_in, mla_q_norm, mla_w_uq, mla_kv_norm, mla_w_ukv, mla_out_norm,
                 ssd_conv_w, ssd_conv_b, ssd_dt_bias, ssd_a_log, ssd_d, ssd_norm,
                 hy_conv_w, hy_conv_b, hy_w1, hy_b1, hy_w2, hy_b2, hy_w3, hy_freq, hy_bias, hy_out_norm,
                 w_out):
    bsz, s, _ = x.shape
    dtype = x.dtype
    proj = jnp.einsum('bsd,de->bse', x, w_in)
    split_idx = np.cumsum(IN_SPLITS)[:-1].tolist()
    (q_c, kv_c, k_pe, r_q, r_k, r_v, r_g, m_z, m_xbc, m_dt, h_u) = jnp.split(proj, split_idx, axis=-1)

    q = jnp.einsum('bsr,re->bse', rms_norm(q_c, mla_q_norm), mla_w_uq).reshape(bsz, s, MLA_HEADS, MLA_NOPE + MLA_ROPE)
    q_nope, q_pe = q[..., :MLA_NOPE], rotary(q[..., MLA_NOPE:])
    kv = jnp.einsum('bsr,re->bse', rms_norm(kv_c, mla_kv_norm), mla_w_ukv).reshape(bsz, s, MLA_HEADS, MLA_NOPE + MLA_V)
    k_nope, v = kv[..., :MLA_NOPE], kv[..., MLA_NOPE:]
    k_pe = rotary(k_pe[:, :, None, :])[:, :, 0]
    out_a = rms_norm(mla_attention(q_nope, q_pe, k_nope, k_pe, v), mla_out_norm).astype(dtype)

    rq = rotary(r_q.reshape(bsz, s, RET_HEADS, RET_DK))
    rk = rotary(r_k.reshape(bsz, s, RET_HEADS, RET_DK)) * (RET_DK ** -0.5)
    rv = r_v.reshape(bsz, s, RET_HEADS, RET_DV)
    out_b = (bidirectional_retention(rq, rk, rv) * jax.nn.silu(r_g.astype(F32))).astype(dtype)

    gn = SSD_GROUPS * SSD_STATE
    xbc = jax.nn.silu(depthwise_conv(m_xbc, ssd_conv_w, ssd_conv_b)).astype(F32)
    xs = xbc[..., :GROUP_WIDTH].reshape(bsz, s, SSD_HEADS, SSD_HEADDIM)
    bm = xbc[..., GROUP_WIDTH:GROUP_WIDTH + gn].reshape(bsz, s, SSD_GROUPS, SSD_STATE)
    cm = xbc[..., GROUP_WIDTH + gn:].reshape(bsz, s, SSD_GROUPS, SSD_STATE)
    dt = jax.nn.softplus(m_dt.astype(F32).reshape(bsz, s, 2, SSD_HEADS) + ssd_dt_bias.astype(F32))
    a = -jnp.exp(ssd_a_log.astype(F32))
    y_f = ssd_chunked(xs, dt[:, :, 0], a[0], bm, cm, True)
    y_b = jnp.flip(ssd_chunked(jnp.flip(xs, 1), jnp.flip(dt[:, :, 1], 1), a[1],
                               jnp.flip(bm, 1), jnp.flip(cm, 1), False), 1)
    y_c = (y_f + y_b + xs * ssd_d.astype(F32)[:, None]).reshape(bsz, s, GROUP_WIDTH)
    out_c = rms_norm(y_c * jax.nn.silu(m_z.astype(F32)), ssd_norm).astype(dtype)

    u = depthwise_conv(h_u, hy_conv_w, hy_conv_b).astype(F32)
    hy_parts = jnp.split(u, HY_ORDER + 1, axis=-1)
    h_freq = hyena_filter_spectrum(s, hy_w1, hy_b1, hy_w2, hy_b2, hy_w3, hy_freq)
    z = hy_parts[0]
    for o in range(HY_ORDER):
        z = hy_parts[o + 1] * fft_long_conv(z, h_freq[:, o], hy_bias[o])
    out_d = rms_norm(z, hy_out_norm).astype(dtype)

    mixed = jnp.concatenate([out_a, out_b, out_c, out_d], axis=-1)
    return jnp.einsum('bse,ed->bsd', mixed, w_out)


def swiglu(x, w_gate, w_up, w_down):
    return (jax.nn.silu(x @ w_gate) * (x @ w_up)) @ w_down


def moe_swiglu(x, w_router, w_gate, w_up, w_down):
    bsz, s, d = x.shape
    xf = x.reshape(-1, d)
    n_tok = xf.shape[0]
    n_asg = n_tok * TOP_K
    logits = (xf @ w_router).astype(F32)
    top_val, top_idx = lax.top_k(logits, TOP_K)
    gates = jax.nn.softmax(top_val, axis=-1)
    e_flat = top_idx.reshape(-1)
    g_flat = gates.reshape(-1)
    tok_flat = jnp.repeat(jnp.arange(n_tok, dtype=jnp.int32), TOP_K)
    order = jnp.argsort(e_flat)
    e_sorted, tok_sorted, g_sorted = e_flat[order], tok_flat[order], g_flat[order]
    counts = jnp.bincount(e_flat, length=N_EXPERTS)
    padded = (counts + MOE_BLOCK - 1) // MOE_BLOCK * MOE_BLOCK
    start = jnp.cumsum(counts) - counts
    pend = jnp.cumsum(padded)
    pstart = pend - padded
    dest = pstart[e_sorted] + (jnp.arange(n_asg, dtype=jnp.int32) - start[e_sorted])
    cap = n_asg + N_EXPERTS * MOE_BLOCK
    n_blocks = cap // MOE_BLOCK
    slot_tok = jnp.full((cap,), n_tok, jnp.int32).at[dest].set(tok_sorted)
    slot_gate = jnp.zeros((cap,), F32).at[dest].set(g_sorted)
    block_expert = jnp.minimum(
        jnp.searchsorted(pend, jnp.arange(n_blocks, dtype=pend.dtype) * MOE_BLOCK, side='right'),
        N_EXPERTS - 1)
    x_pad = jnp.concatenate([xf, jnp.zeros((1, d), xf.dtype)], axis=0)
    x_slots = x_pad[slot_tok].reshape(n_blocks, MOE_BLOCK, d)

    def expert_block(args):
        xb, e = args
        return swiglu(xb, w_gate[e], w_up[e], w_down[e])

    y_slots = lax.map(expert_block, (x_slots, block_expert)).reshape(cap, d)
    y = jnp.zeros((n_tok + 1, d), y_slots.dtype).at[slot_tok].add(y_slots * slot_gate[:, None].astype(y_slots.dtype))
    return y[:n_tok].reshape(bsz, s, d).astype(x.dtype)


def setup_inputs(seed: int = 0) -> dict:
    key = jax.random.key(seed)
    ks = jax.random.split(key, 40)

    def nrm(i, shape, scale):
        return jax.random.normal(ks[i], shape, F32) * scale

    L = DEPTH
    D = D_MODEL
    W = GROUP_WIDTH
    HID = HY_FILTER_HIDDEN
    dt0 = jnp.exp(jax.random.uniform(ks[9], (L, 2, SSD_HEADS), F32, minval=math.log(1e-3), maxval=math.log(1e-1)))
    return {
        'x': nrm(0, (BATCH, SEQ, D), 1.0),
        'w_in': nrm(1, (L, D, IN_COLS), D ** -0.5),
        'mla_q_norm': 1.0 + nrm(2, (L, MLA_Q_LORA), 0.02),
        'mla_w_uq': nrm(3, (L, MLA_Q_LORA, MLA_HEADS * (MLA_NOPE + MLA_ROPE)), MLA_Q_LORA ** -0.5),
        'mla_kv_norm': 1.0 + nrm(4, (L, MLA_KV_LORA), 0.02),
        'mla_w_ukv': nrm(5, (L, MLA_KV_LORA, MLA_HEADS * (MLA_NOPE + MLA_V)), MLA_KV_LORA ** -0.5),
        'mla_out_norm': 1.0 + nrm(6, (L, W), 0.02),
        'ssd_conv_w': nrm(7, (L, SSD_CONV, SSD_CONV_CH), SSD_CONV ** -0.5),
        'ssd_conv_b': nrm(8, (L, SSD_CONV_CH), 0.02),
        'ssd_dt_bias': dt0 + jnp.log(-jnp.expm1(-dt0)),
        'ssd_a_log': jnp.log(jax.random.uniform(ks[10], (L, 2, SSD_HEADS), F32, minval=1.0, maxval=16.0)),
        'ssd_d': 1.0 + nrm(11, (L, SSD_HEADS), 0.02),
        'ssd_norm': 1.0 + nrm(12, (L, W), 0.02),
        'hy_conv_w': nrm(13, (L, HY_SHORT, (HY_ORDER + 1) * W), HY_SHORT ** -0.5),
        'hy_conv_b': nrm(14, (L, (HY_ORDER + 1) * W), 0.02),
        'hy_w1': nrm(15, (L, HY_EMB, HID), HY_EMB ** -0.5),
        'hy_b1': nrm(16, (L, HID), 0.1),
        'hy_w2': nrm(17, (L, HID, HID), HID ** -0.5),
        'hy_b2': nrm(18, (L, HID), 0.1),
        'hy_w3': nrm(19, (L, HID, HY_ORDER * 2 * W), HID ** -0.5),
        'hy_freq': 1.0 + nrm(20, (L, 2, HID), 0.02),
        'hy_bias': nrm(21, (L, HY_ORDER, W), 1.0),
        'hy_out_norm': 1.0 + nrm(22, (L, W), 0.02),
        'w_out': nrm(23, (L, D, D), BETA * D ** -0.5),
        'ln1_g': 1.0 + nrm(24, (L, D), 0.02),
        'ln1_b': nrm(25, (L, D), 0.02),
        'ln2_g': 1.0 + nrm(26, (L, D), 0.02),
        'ln2_b': nrm(27, (L, D), 0.02),
        'ffn_w_gate': nrm(28, (N_DENSE, D, D_FF), D ** -0.5),
        'ffn_w_up': nrm(29, (N_DENSE, D, D_FF), D ** -0.5),
        'ffn_w_down': nrm(30, (N_DENSE, D_FF, D), BETA * D_FF ** -0.5),
        'moe_router': nrm(31, (N_MOE, D, N_EXPERTS), D ** -0.5),
        'moe_w_gate': nrm(32, (N_MOE, N_EXPERTS, D, D_FF_EXPERT), D ** -0.5),
        'moe_w_up': nrm(33, (N_MOE, N_EXPERTS, D, D_FF_EXPERT), D ** -0.5),
        'moe_w_down': nrm(34, (N_MOE, N_EXPERTS, D_FF_EXPERT, D), BETA * D_FF_EXPERT ** -0.5),
    }


def reference(x, w_in, mla_q_norm, mla_w_uq, mla_kv_norm, mla_w_ukv, mla_out_norm,
              ssd_conv_w, ssd_conv_b, ssd_dt_bias, ssd_a_log, ssd_d, ssd_norm,
              hy_conv_w, hy_conv_b, hy_w1, hy_b1, hy_w2, hy_b2, hy_w3, hy_freq, hy_bias, hy_out_norm,
              w_out, ln1_g, ln1_b, ln2_g, ln2_b,
              ffn_w_gate, ffn_w_up, ffn_w_down,
              moe_router, moe_w_gate, moe_w_up, moe_w_down):
    for layer in range(DEPTH):
        h = hybrid_mixer(x, w_in[layer], mla_q_norm[layer], mla_w_uq[layer], mla_kv_norm[layer],
                         mla_w_ukv[layer], mla_out_norm[layer],
                         ssd_conv_w[layer], ssd_conv_b[layer], ssd_dt_bias[layer], ssd_a_log[layer],
                         ssd_d[layer], ssd_norm[layer],
                         hy_conv_w[layer], hy_conv_b[layer], hy_w1[layer], hy_b1[layer], hy_w2[layer],
                         hy_b2[layer], hy_w3[layer], hy_freq[layer], hy_bias[layer], hy_out_norm[layer],
                         w_out[layer])
        x = layer_norm(ALPHA * x + h, ln1_g[layer], ln1_b[layer])
        j = layer // 2
        if layer % 2 == 0:
            f = swiglu(x, ffn_w_gate[j], ffn_w_up[j], ffn_w_down[j])
        else:
            f = moe_swiglu(x, moe_router[j], moe_w_gate[j], moe_w_up[j], moe_w_down[j])
        x = layer_norm(ALPHA * x + f, ln2_g[layer], ln2_b[layer])
    return x
```

```python
import functools
import math

import numpy as np
import jax
import jax.numpy as jnp
from jax import lax
from jax.experimental import pallas as pl
from jax.experimental.pallas import tpu as pltpu

F32 = jnp.float32
BF16 = jnp.bfloat16
HIGHEST = lax.Precision.HIGHEST

D_MODEL = 2048
DEPTH = 2
GROUP_WIDTH = 512
MLA_HEADS = 4
MLA_NOPE = 128
MLA_ROPE = 64
MLA_V = 128
MLA_Q_LORA = 384
MLA_KV_LORA = 256
RET_HEADS = 4
RET_DV = 128
RET_DK = 64
RET_DECAY_EXP_FWD = 5.0
RET_DECAY_EXP_BWD = 5.5
SSD_HEADDIM = 64
SSD_HEADS = 8
SSD_GROUPS = 2
SSD_STATE = 128
SSD_CONV = 5
SSD_CONV_CH = 1024
HY_ORDER = 2
HY_WIDTH = 512
HY_SHORT = 3
HY_EMB = 33
HY_FILTER_HIDDEN = 64
HY_MIN_DECAY = math.log(1e-2) / 1.5
HY_MAX_DECAY = math.log(1e-2) / 0.3
N_EXPERTS = 8
TOP_K = 2
ROPE_BASE = 10000.0
ALPHA = (2 * DEPTH) ** 0.25

V7X_VMEM_BYTES = 64 * 1024 * 1024
LANES = 128
SUBLANES = 8

PROJ_COLS = 6272
COL_HY = 0
COL_QC = 1536
COL_RQ = 2048
COL_KVPE = 4096
COL_Z = 4608
COL_XBC = 5120
COL_DT = 6144

RET_CHUNK = 256
SSD_CHUNK = 128
HY_ROW_TILE = 512


def _cparams(semantics, vmem_mb):
    assert vmem_mb * 1024 * 1024 < V7X_VMEM_BYTES
    return pltpu.CompilerParams(dimension_semantics=semantics, vmem_limit_bytes=vmem_mb * 1024 * 1024)


def _sigmoid(x):
    return 1.0 / (1.0 + jnp.exp(-x))


def _silu(x):
    return x * _sigmoid(x)


def _rms(x, w, eps=1e-6):
    return x * lax.rsqrt(jnp.mean(x * x, axis=-1, keepdims=True) + eps) * w


def _layer_norm(y, g, b, eps=1e-5):
    mu = jnp.mean(y, axis=-1, keepdims=True)
    d = y - mu
    var = jnp.mean(d * d, axis=-1, keepdims=True)
    return d * lax.rsqrt(var + eps) * g + b


def _dot(a, b):
    return jnp.dot(a, b, preferred_element_type=F32)


def _dot_nt(a, b):
    return lax.dot_general(a, b, (((1,), (1,)), ((), ())), preferred_element_type=F32)


def _dot_tn(a, b):
    return lax.dot_general(a, b, (((0,), (0,)), ((), ())), preferred_element_type=F32)


def _dot_f32(a, b):
    return jnp.dot(a, b, preferred_element_type=F32, precision=HIGHEST)


def _mm_body(x_ref, w_ref, o_ref, xb_ref):
    @pl.when(pl.program_id(1) == 0)
    def _():
        xb_ref[...] = x_ref[...].astype(BF16)

    o_ref[...] = _dot(xb_ref[...], w_ref[...])


def _matmul(x, w, tm, tn, vmem_mb=48):
    m, k = x.shape
    n = w.shape[1]
    assert m % tm == 0 and n % tn == 0
    return pl.pallas_call(
        _mm_body,
        grid=(m // tm, n // tn),
        in_specs=[pl.BlockSpec((tm, k), lambda i, j: (i, 0)),
                  pl.BlockSpec((k, tn), lambda i, j: (0, j))],
        out_specs=pl.BlockSpec((tm, tn), lambda i, j: (i, j)),
        out_shape=jax.ShapeDtypeStruct((m, n), F32),
        scratch_shapes=[pltpu.VMEM((tm, k), BF16)],
        compiler_params=_cparams(("parallel", "arbitrary"), vmem_mb),
    )(x, w)


def _rot_half_cols(w, heads):
    k = w.shape[0]
    w = w.reshape(k, heads, 2, 32)
    return jnp.stack([-w[:, :, 1], w[:, :, 0]], axis=2).reshape(k, heads * 64)


def _prep_w_in(w):
    k = w.shape[0]

    def sl(a, b):
        return w[:, a:b]

    def zc(n):
        return jnp.zeros((k, n), w.dtype)

    q_c, kv_c, k_pe = sl(0, 384), sl(384, 640), sl(640, 704)
    r_q, r_k, r_v, r_g = sl(704, 960), sl(960, 1216), sl(1216, 1728), sl(1728, 2240)
    m_z, m_xbc, m_dt, h_u = sl(2240, 2752), sl(2752, 3776), sl(3776, 3792), sl(3792, 5328)
    cols = [h_u, q_c, zc(128),
            r_q, r_k, _rot_half_cols(r_q, RET_HEADS), _rot_half_cols(r_k, RET_HEADS), r_v, r_g,
            kv_c, k_pe, zc(64), _rot_half_cols(k_pe, 1), zc(64),
            m_z, m_xbc, m_dt, zc(112)]
    out = jnp.concatenate(cols, axis=1).astype(BF16)
    assert out.shape[1] == PROJ_COLS
    return out


def _prep_w_uq(w):
    k = w.shape[0]
    w = w.reshape(k, MLA_HEADS, MLA_NOPE + MLA_ROPE)
    nope, rope = w[:, :, :MLA_NOPE], w[:, :, MLA_NOPE:]
    z = jnp.zeros((k, MLA_HEADS, 64), w.dtype)
    main = jnp.concatenate([nope, rope, z], axis=2).reshape(k, MLA_HEADS * 256)
    rr = _rot_half_cols(rope.reshape(k, MLA_HEADS * 64), MLA_HEADS).reshape(k, MLA_HEADS, 64)
    rot = jnp.concatenate([rr, z], axis=2).reshape(k, MLA_HEADS * 128)
    return main.astype(BF16), rot.astype(BF16)


def _prep_w_ukv(w):
    k = w.shape[0]
    w = w.reshape(k, MLA_HEADS, MLA_NOPE + MLA_V)
    return jnp.concatenate([w[:, :, :MLA_NOPE].reshape(k, -1), w[:, :, MLA_NOPE:].reshape(k, -1)], axis=1).astype(BF16)


def _rope_tables(s):
    half = 32
    inv_freq = ROPE_BASE ** (-jnp.arange(half, dtype=F32) * 2.0 / 64)
    ang = jnp.arange(s, dtype=F32)[:, None] * inv_freq[None, :]
    cos, sin = jnp.cos(ang), jnp.sin(ang)
    cos64 = jnp.concatenate([cos, cos], axis=1)
    sin64 = jnp.concatenate([sin, sin], axis=1)
    z = jnp.zeros((s, 64), F32)
    return (jnp.concatenate([cos64, z], axis=1), jnp.concatenate([sin64, z], axis=1),
            jnp.tile(cos64, (1, 4)), jnp.tile(sin64, (1, 4)))


def _ret_tables(n):
    heads = jnp.arange(RET_HEADS, dtype=F32)
    lgf = jnp.log1p(-jnp.exp2(-RET_DECAY_EXP_FWD - heads))
    lgb = jnp.log1p(-jnp.exp2(-RET_DECAY_EXP_BWD - heads))
    idx = jnp.arange(n, dtype=F32)

    def lanes(tab, width):
        return jnp.repeat(tab, width, axis=1)

    kdf = lanes(jnp.exp((n - 1.0 - idx)[:, None] * lgf), RET_DK)
    kdb = lanes(jnp.exp(idx[:, None] * lgb), RET_DK)
    qdf = lanes(jnp.exp((idx + 1.0)[:, None] * lgf), RET_DK)
    qdb = lanes(jnp.exp((n - idx)[:, None] * lgb), RET_DK)
    cdf = lanes(jnp.exp(n * lgf)[None, :], RET_DV)
    cdb = lanes(jnp.exp(n * lgb)[None, :], RET_DV)
    diff = idx[:, None] - idx[None, :]
    dec = jnp.where(diff[None] >= 0,
                    jnp.exp(jnp.maximum(diff, 0.0)[None] * lgf[:, None, None]),
                    jnp.exp(jnp.maximum(-diff, 0.0)[None] * lgb[:, None, None]))
    bd = (jnp.arange(RET_HEADS * RET_DK)[:, None] // RET_DK == jnp.arange(RET_HEADS * RET_DV)[None, :] // RET_DV)
    return kdf, kdb, qdf, qdb, cdf, cdb, dec.astype(F32), bd.astype(F32)


def _ssd_tables(n):
    tri = (jnp.arange(n)[:, None] >= jnp.arange(n)[None, :]).astype(F32)
    lane_head = jnp.arange(GROUP_WIDTH) // SSD_HEADDIM
    r = jnp.arange(LANES)
    ef = (r[:, None] == lane_head[None, :]).astype(F32)
    eb = (r[:, None] == lane_head[None, :] + SSD_HEADS).astype(F32)
    return tri, jnp.concatenate([ef, eb], axis=1)


def _dft_tables(l):
    f = jnp.arange(l, dtype=jnp.int32)
    k = (f[:, None] * f[None, :]) % (2 * l)
    ang = k.astype(F32) * (math.pi / l)
    c, s = jnp.cos(ang), jnp.sin(ang)
    sc = jnp.where(f == 0, 1.0, 2.0).astype(F32)[:, None] / (2.0 * l)
    nyq = jnp.where(f % 2 == 0, 1.0, -1.0).astype(F32)[None, :] / (2.0 * l)
    c_sc = jnp.concatenate([c * sc, nyq, jnp.zeros((7, l), F32)], axis=0)
    return c.astype(BF16), s.astype(BF16), c_sc.astype(BF16), (-s * sc).astype(BF16)


def _hyena_positions(l):
    t = jnp.linspace(0.0, 1.0, l, dtype=F32)[:, None]
    bands = (HY_EMB - 1) // 2
    ang = 2.0 * math.pi * jnp.arange(l, dtype=F32)[:, None] / l
    f = jnp.linspace(1e-4, bands - 1, bands, dtype=F32)[None, :]
    z = jnp.concatenate([t, jnp.cos(f * ang), -jnp.sin(f * ang), jnp.zeros((l, LANES - HY_EMB), F32)], axis=-1)
    deltas = jnp.abs(jnp.linspace(HY_MIN_DECAY, HY_MAX_DECAY, HY_WIDTH, dtype=F32))[None, :]
    return z, deltas


def _conv_body(taps, act, x_ref, p_ref, n_ref, w_ref, b_ref, o_ref):
    r = pl.program_id(1)
    tr = x_ref.shape[1]
    half = taps // 2
    prev = jnp.where(r > 0, p_ref[0], 0.0)
    nxt = jnp.where(r < pl.num_programs(1) - 1, n_ref[0], 0.0)
    win = jnp.concatenate([prev, x_ref[0], nxt], axis=0)
    rows = tr + 2 * SUBLANES
    acc = jnp.broadcast_to(b_ref[...], (tr, x_ref.shape[2]))
    for t in range(taps):
        sh = (half - t) % rows
        shifted = win if sh == 0 else pltpu.roll(win, sh, axis=0)
        acc = acc + shifted[SUBLANES:SUBLANES + tr] * w_ref[t:t + 1, :]
    if act:
        acc = _silu(acc)
    o_ref[0] = acc


def _dwconv(proj3, col0, width, w, b, act):
    bsz, s, _ = proj3.shape
    taps = w.shape[0]
    cb = 512
    tr = min(512, s)
    off = col0 // cb
    nb8 = s // SUBLANES
    t8 = tr // SUBLANES
    return pl.pallas_call(
        functools.partial(_conv_body, taps, act),
        grid=(bsz, s // tr, width // cb),
        in_specs=[pl.BlockSpec((1, tr, cb), lambda i, r, c: (i, r, c + off)),
                  pl.BlockSpec((1, SUBLANES, cb), lambda i, r, c: (i, jnp.maximum(r * t8 - 1, 0), c + off)),
                  pl.BlockSpec((1, SUBLANES, cb), lambda i, r, c: (i, jnp.minimum((r + 1) * t8, nb8 - 1), c + off)),
                  pl.BlockSpec((taps, cb), lambda i, r, c: (0, c)),
                  pl.BlockSpec((1, cb), lambda i, r, c: (0, c))],
        out_specs=pl.BlockSpec((1, tr, cb), lambda i, r, c: (i, r, c)),
        out_shape=jax.ShapeDtypeStruct((bsz, s, width), F32),
        compiler_params=_cparams(("parallel", "parallel", "parallel"), 32),
    )(proj3, proj3, proj3, w, b.reshape(1, width))


def _mla_body(qc_ref, kvpe_ref, cq_ref, sq_ref, ck_ref, sk_ref, qn_ref, wq_ref, wqr_ref, kvn_ref, wkv_ref, on_ref,
              o_ref, k_scr, v_scr):
    @pl.when(pl.program_id(1) == 0)
    def _():
        kvpe = kvpe_ref[0]
        kvn = _rms(kvpe[:, :MLA_KV_LORA], kvn_ref[...]).astype(BF16)
        kpe = (kvpe[:, 256:384] * ck_ref[...] + kvpe[:, 384:512] * sk_ref[...]).astype(BF16)
        for h in range(MLA_HEADS):
            kn = _dot(kvn, wkv_ref[:, h * 128:(h + 1) * 128]).astype(BF16)
            k_scr[h] = jnp.concatenate([kn, kpe], axis=1)
            v_scr[h] = _dot(kvn, wkv_ref[:, 512 + h * 128:512 + (h + 1) * 128]).astype(BF16)

    scale = (MLA_NOPE + MLA_ROPE) ** -0.5
    qn = _rms(qc_ref[0][:, :MLA_Q_LORA], qn_ref[...]).astype(BF16)
    outs = []
    for h in range(MLA_HEADS):
        qm = _dot(qn, wq_ref[:, h * 256:(h + 1) * 256])
        qr = _dot(qn, wqr_ref[:, h * 128:(h + 1) * 128])
        qpe = qm[:, 128:] * cq_ref[...] + qr * sq_ref[...]
        qh = (jnp.concatenate([qm[:, :128], qpe], axis=1) * scale).astype(BF16)
        sc = _dot_nt(qh, k_scr[h])
        p = jnp.exp(sc - jnp.max(sc, axis=-1, keepdims=True))
        den = jnp.sum(p, axis=-1, keepdims=True)
        outs.append(_dot(p.astype(BF16), v_scr[h]) / den)
    o_ref[0] = _rms(jnp.concatenate(outs, axis=1), on_ref[...])


def _mla(proj3, cos_slab, sin_slab, q_norm, w_q, w_qr, kv_norm, w_kv, out_norm):
    bsz, s, _ = proj3.shape
    tq = min(512, s)
    const = lambda i, j: (0, 0)
    return pl.pallas_call(
        _mla_body,
        grid=(bsz, s // tq),
        in_specs=[pl.BlockSpec((1, tq, 512), lambda i, j: (i, j, COL_QC // 512)),
                  pl.BlockSpec((1, s, 512), lambda i, j: (i, 0, COL_KVPE // 512)),
                  pl.BlockSpec((tq, 128), lambda i, j: (j, 0)),
                  pl.BlockSpec((tq, 128), lambda i, j: (j, 0)),
                  pl.BlockSpec((s, 128), const),
                  pl.BlockSpec((s, 128), const),
                  pl.BlockSpec((1, MLA_Q_LORA), const),
                  pl.BlockSpec(w_q.shape, const),
                  pl.BlockSpec(w_qr.shape, const),
                  pl.BlockSpec((1, MLA_KV_LORA), const),
                  pl.BlockSpec(w_kv.shape, const),
                  pl.BlockSpec((1, GROUP_WIDTH), const)],
        out_specs=pl.BlockSpec((1, tq, GROUP_WIDTH), lambda i, j: (i, j, 0)),
        out_shape=jax.ShapeDtypeStruct((bsz, s, GROUP_WIDTH), F32),
        scratch_shapes=[pltpu.VMEM((MLA_HEADS, s, 256), BF16), pltpu.VMEM((MLA_HEADS, s, MLA_V), BF16)],
        compiler_params=_cparams(("parallel", "arbitrary"), 48),
    )(proj3, proj3, cos_slab, sin_slab, cos_slab, sin_slab, q_norm.reshape(1, -1), w_q, w_qr,
      kv_norm.reshape(1, -1), w_kv, out_norm.reshape(1, -1))


def _ret_state_body(nchunks, rk_ref, rkr_ref, rv_ref, cos_ref, sin_ref, kdf_ref, kdb_ref, cdf_ref, cdb_ref, bd_ref,
                    sf_ref, sb_ref, st_scr):
    t = pl.program_id(1)
    k = (rk_ref[0] * cos_ref[...] + rkr_ref[0] * sin_ref[...]) * (RET_DK ** -0.5)
    v = rv_ref[0].astype(BF16)

    @pl.when((t == 0) | (t == nchunks))
    def _():
        st_scr[...] = jnp.zeros_like(st_scr)

    def step(out_ref, kd_ref, cd_ref):
        out_ref[0, 0] = st_scr[...]
        new = _dot_tn((k * kd_ref[...]).astype(BF16), v)
        st_scr[...] = st_scr[...] * cd_ref[...] + new * bd_ref[...]

    @pl.when(t < nchunks)
    def _():
        step(sf_ref, kdf_ref, cdf_ref)

    @pl.when(t >= nchunks)
    def _():
        step(sb_ref, kdb_ref, cdb_ref)


def _ret_out_body(rq_ref, rk_ref, rqr_ref, rkr_ref, rv_ref, rg_ref, cos_ref, sin_ref, qdf_ref, qdb_ref, dec_ref,
                  sf_ref, sb_ref, o_ref):
    cos, sin = cos_ref[...], sin_ref[...]
    q = rq_ref[0] * cos + rqr_ref[0] * sin
    kb = ((rk_ref[0] * cos + rkr_ref[0] * sin) * (RET_DK ** -0.5)).astype(BF16)
    vb = rv_ref[0].astype(BF16)
    cross = (_dot((q * qdf_ref[...]).astype(BF16), sf_ref[0, 0].astype(BF16))
             + _dot((q * qdb_ref[...]).astype(BF16), sb_ref[0, 0].astype(BF16)))
    lane_head = lax.broadcasted_iota(jnp.int32, (1, RET_HEADS * RET_DK), 1) // RET_DK
    outs = []
    for h in range(RET_HEADS):
        qh = jnp.where(lane_head == h, q, 0.0).astype(BF16)
        sc = _dot_nt(qh, kb) * dec_ref[h]
        y = _dot(sc.astype(BF16), vb[:, h * RET_DV:(h + 1) * RET_DV]) + cross[:, h * RET_DV:(h + 1) * RET_DV]
        mu = jnp.mean(y, axis=-1, keepdims=True)
        d = y - mu
        var = jnp.mean(d * d, axis=-1, keepdims=True)
        outs.append(d * lax.rsqrt(var + 1e-6))
    o_ref[0] = jnp.concatenate(outs, axis=1) * _silu(rg_ref[0])


def _retention(proj3, cos4, sin4):
    bsz, s, _ = proj3.shape
    n = min(RET_CHUNK, s)
    c = s // n
    kdf, kdb, qdf, qdb, cdf, cdb, dec, bd = _ret_tables(n)
    c256 = COL_RQ // 256
    c512 = COL_RQ // 512
    const2 = lambda i, t: (0, 0)

    def chunk(t):
        return jnp.where(t < c, t, 2 * c - 1 - t)

    state_shape = jax.ShapeDtypeStruct((bsz, c, RET_HEADS * RET_DK, GROUP_WIDTH), F32)
    sf, sb = pl.pallas_call(
        functools.partial(_ret_state_body, c),
        grid=(bsz, 2 * c),
        in_specs=[pl.BlockSpec((1, n, 256), lambda i, t: (i, chunk(t), c256 + 1)),
                  pl.BlockSpec((1, n, 256), lambda i, t: (i, chunk(t), c256 + 3)),
                  pl.BlockSpec((1, n, 512), lambda i, t: (i, chunk(t), c512 + 2)),
                  pl.BlockSpec((n, 256), lambda i, t: (chunk(t), 0)),
                  pl.BlockSpec((n, 256), lambda i, t: (chunk(t), 0)),
                  pl.BlockSpec((n, 256), const2), pl.BlockSpec((n, 256), const2),
                  pl.BlockSpec((1, 512), const2), pl.BlockSpec((1, 512), const2),
                  pl.BlockSpec((256, 512), const2)],
        out_specs=[pl.BlockSpec((1, 1, 256, 512), lambda i, t: (i, jnp.minimum(t, c - 1), 0, 0)),
                   pl.BlockSpec((1, 1, 256, 512), lambda i, t: (i, jnp.minimum(2 * c - 1 - t, c - 1), 0, 0))],
        out_shape=[state_shape, state_shape],
        scratch_shapes=[pltpu.VMEM((RET_HEADS * RET_DK, GROUP_WIDTH), F32)],
        compiler_params=_cparams(("parallel", "arbitrary"), 32),
    )(proj3, proj3, proj3, cos4, sin4, kdf, kdb, cdf, cdb, bd)

    const3 = lambda i, j: (0, 0, 0)
    const2 = lambda i, j: (0, 0)
    return pl.pallas_call(
        _ret_out_body,
        grid=(bsz, c),
        in_specs=[pl.BlockSpec((1, n, 256), lambda i, j: (i, j, c256)),
                  pl.BlockSpec((1, n, 256), lambda i, j: (i, j, c256 + 1)),
                  pl.BlockSpec((1, n, 256), lambda i, j: (i, j, c256 + 2)),
                  pl.BlockSpec((1, n, 256), lambda i, j: (i, j, c256 + 3)),
                  pl.BlockSpec((1, n, 512), lambda i, j: (i, j, c512 + 2)),
                  pl.BlockSpec((1, n, 512), lambda i, j: (i, j, c512 + 3)),
                  pl.BlockSpec((n, 256), lambda i, j: (j, 0)),
                  pl.BlockSpec((n, 256), lambda i, j: (j, 0)),
                  pl.BlockSpec((n, 256), const2), pl.BlockSpec((n, 256), const2),
                  pl.BlockSpec((RET_HEADS, n, n), const3),
                  pl.BlockSpec((1, 1, 256, 512), lambda i, j: (i, j, 0, 0)),
                  pl.BlockSpec((1, 1, 256, 512), lambda i, j: (i, j, 0, 0))],
        out_specs=pl.BlockSpec((1, n, GROUP_WIDTH), lambda i, j: (i, j, 0)),
        out_shape=jax.ShapeDtypeStruct((bsz, s, GROUP_WIDTH), F32),
        compiler_params=_cparams(("parallel", "parallel"), 32),
    )(proj3, proj3, proj3, proj3, proj3, proj3, cos4, sin4, qdf, qdb, dec, sf, sb)


def _ssd_decays(dt_ref, dtb_ref, a_ref, tri_ref, e_ref):
    dt_raw = dt_ref[0] + dtb_ref[...]
    dt = jnp.maximum(dt_raw, 0.0) + jnp.log1p(jnp.exp(-jnp.abs(dt_raw)))
    la = dt * a_ref[...]
    cs = _dot_f32(tri_ref[...], la)
    e = e_ref[...]
    return dt, la, cs, _dot_f32(dt, e), _dot_f32(la, e), _dot_f32(cs, e)


def _ssd_state_body(nchunks, xbc_ref, dt_ref, dtb_ref, a_ref, tri_ref, e_ref, sf_ref, sb_ref, st_scr):
    t = pl.program_id(1)
    n = xbc_ref.shape[1]
    w = GROUP_WIDTH
    _, _, _, dt_e, la_e, cs_e = _ssd_decays(dt_ref, dtb_ref, a_ref, tri_ref, e_ref)
    xs = xbc_ref[0][:, :w]
    tot_e = cs_e[n - 1:n, :]

    @pl.when((t == 0) | (t == nchunks))
    def _():
        st_scr[...] = jnp.zeros_like(st_scr)

    def step(out_ref, lo, weight):
        out_ref[0, 0] = st_scr[...]
        xw = (xs * dt_e[:, lo:lo + w] * weight).astype(BF16)
        new = []
        for g in range(SSD_GROUPS):
            bg = xbc_ref[0][:, w + g * SSD_STATE:w + (g + 1) * SSD_STATE].astype(BF16)
            new.append(_dot_tn(bg, xw[:, g * 256:(g + 1) * 256]))
        st_scr[...] = st_scr[...] * jnp.exp(tot_e[:, lo:lo + w]) + jnp.concatenate(new, axis=1)

    @pl.when(t < nchunks)
    def _():
        step(sf_ref, 0, jnp.exp(tot_e[:, :w] - cs_e[:, :w]))

    @pl.when(t >= nchunks)
    def _():
        step(sb_ref, w, jnp.exp(cs_e[:, w:] - la_e[:, w:]))


def _ssd_out_body(xbc_ref, dt_ref, z_ref, dtb_ref, a_ref, tri_ref, e_ref, dskip_ref, nw_ref, sf_ref, sb_ref, o_ref):
    n = xbc_ref.shape[1]
    w = GROUP_WIDTH
    _, la, cs, dt_e, la_e, cs_e = _ssd_decays(dt_ref, dtb_ref, a_ref, tri_ref, e_ref)
    xbc = xbc_ref[0]
    xs = xbc[:, :w]
    xdt_f = xs * dt_e[:, :w]
    xdt_b = xs * dt_e[:, w:]
    ecs = cs - la
    cs_t = cs.T
    ecs_t = ecs.T
    ii = lax.broadcasted_iota(jnp.int32, (n, n), 0)
    jj = lax.broadcasted_iota(jnp.int32, (n, n), 1)
    low = lax.broadcasted_iota(jnp.int32, (1, LANES), 1) < SSD_HEADDIM
    neg = -1e30
    ydiag = []
    yoff = []
    for g in range(SSD_GROUPS):
        bg = xbc[:, w + g * SSD_STATE:w + (g + 1) * SSD_STATE].astype(BF16)
        cg = xbc[:, w + 256 + g * SSD_STATE:w + 256 + (g + 1) * SSD_STATE].astype(BF16)
        cb = _dot_nt(cg, bg)
        for pair in range(2):
            p = 2 * g + pair
            lhs = []
            for h in (2 * p, 2 * p + 1):
                lf = jnp.exp(jnp.where(ii >= jj, cs[:, h:h + 1] - cs_t[h:h + 1, :], neg))
                lhs.append((cb * lf).astype(BF16))
            for h in (2 * p, 2 * p + 1):
                hb = SSD_HEADS + h
                lb = jnp.exp(jnp.where(jj > ii, ecs_t[hb:hb + 1, :] - ecs[:, hb:hb + 1], neg))
                lhs.append((cb * lb).astype(BF16))
            xf = xdt_f[:, p * LANES:(p + 1) * LANES]
            xb = xdt_b[:, p * LANES:(p + 1) * LANES]
            rhs = jnp.concatenate([jnp.where(low, xf, 0.0), jnp.where(low, 0.0, xf),
                                   jnp.where(low, xb, 0.0), jnp.where(low, 0.0, xb)], axis=0).astype(BF16)
            ydiag.append(_dot(jnp.concatenate(lhs, axis=1), rhs))
        yoff.append(_dot(cg, sf_ref[0, 0][:, g * 256:(g + 1) * 256].astype(BF16)) * jnp.exp(cs_e[:, g * 256:(g + 1) * 256])
                    + _dot(cg, sb_ref[0, 0][:, g * 256:(g + 1) * 256].astype(BF16))
                    * jnp.exp(cs_e[n - 1:n, w + g * 256:w + (g + 1) * 256]
                              - (cs_e[:, w + g * 256:w + (g + 1) * 256] - la_e[:, w + g * 256:w + (g + 1) * 256])))
    y = jnp.concatenate(ydiag, axis=1) + jnp.concatenate(yoff, axis=1) + xs * dskip_ref[...]
    o_ref[0] = _rms(y * _silu(z_ref[0]), nw_ref[...])


def _ssd(proj3, xbc_act, dt_bias, a_log, d_skip, norm_w):
    bsz, s, _ = proj3.shape
    n = SSD_CHUNK
    c = s // n
    tri, e = _ssd_tables(n)
    pad = jnp.zeros((LANES - 2 * SSD_HEADS,), F32)
    dtb = jnp.concatenate([dt_bias.reshape(-1), pad]).reshape(1, LANES)
    a = jnp.concatenate([-jnp.exp(a_log.reshape(-1)), pad]).reshape(1, LANES)
    dskip = jnp.repeat(d_skip, SSD_HEADDIM).reshape(1, GROUP_WIDTH)
    cdt = COL_DT // LANES
    const2 = lambda i, t: (0, 0)

    def chunk(t):
        return jnp.where(t < c, t, 2 * c - 1 - t)

    state_shape = jax.ShapeDtypeStruct((bsz, c, SSD_STATE, GROUP_WIDTH), F32)
    sf, sb = pl.pallas_call(
        functools.partial(_ssd_state_body, c),
        grid=(bsz, 2 * c),
        in_specs=[pl.BlockSpec((1, n, SSD_CONV_CH), lambda i, t: (i, chunk(t), 0)),
                  pl.BlockSpec((1, n, LANES), lambda i, t: (i, chunk(t), cdt)),
                  pl.BlockSpec((1, LANES), const2), pl.BlockSpec((1, LANES), const2),
                  pl.BlockSpec((n, n), const2), pl.BlockSpec((LANES, 2 * GROUP_WIDTH), const2)],
        out_specs=[pl.BlockSpec((1, 1, SSD_STATE, GROUP_WIDTH), lambda i, t: (i, jnp.minimum(t, c - 1), 0, 0)),
                   pl.BlockSpec((1, 1, SSD_STATE, GROUP_WIDTH), lambda i, t: (i, jnp.minimum(2 * c - 1 - t, c - 1), 0, 0))],
        out_shape=[state_shape, state_shape],
        scratch_shapes=[pltpu.VMEM((SSD_STATE, GROUP_WIDTH), F32)],
        compiler_params=_cparams(("parallel", "arbitrary"), 32),
    )(xbc_act, proj3, dtb, a, tri, e)

    const2 = lambda i, j: (0, 0)
    return pl.pallas_call(
        _ssd_out_body,
        grid=(bsz, c),
        in_specs=[pl.BlockSpec((1, n, SSD_CONV_CH), lambda i, j: (i, j, 0)),
                  pl.BlockSpec((1, n, LANES), lambda i, j: (i, j, cdt)),
                  pl.BlockSpec((1, n, GROUP_WIDTH), lambda i, j: (i, j, COL_Z // GROUP_WIDTH)),
                  pl.BlockSpec((1, LANES), const2), pl.BlockSpec((1, LANES), const2),
                  pl.BlockSpec((n, n), const2), pl.BlockSpec((LANES, 2 * GROUP_WIDTH), const2),
                  pl.BlockSpec((1, GROUP_WIDTH), const2), pl.BlockSpec((1, GROUP_WIDTH), const2),
                  pl.BlockSpec((1, 1, SSD_STATE, GROUP_WIDTH), lambda i, j: (i, j, 0, 0)),
                  pl.BlockSpec((1, 1, SSD_STATE, GROUP_WIDTH), lambda i, j: (i, j, 0, 0))],
        out_specs=pl.BlockSpec((1, n, GROUP_WIDTH), lambda i, j: (i, j, 0)),
        out_shape=jax.ShapeDtypeStruct((bsz, s, GROUP_WIDTH), F32),
        compiler_params=_cparams(("parallel", "parallel"), 32),
    )(xbc_act, proj3, proj3, dtb, a, tri, e, dskip, norm_w.reshape(1, -1), sf, sb)


def _hy_filter_body(z_ref, w1_ref, b1_ref, w2_ref, b2_ref, w3_ref, fr_ref, dl_ref, hs_ref, hd_ref):
    tl = z_ref.shape[0]
    z = z_ref[...]
    hid = jnp.sin(fr_ref[0:1, :] * (_dot_f32(z, w1_ref[...]) + b1_ref[...]))
    hid = jnp.sin(fr_ref[1:2, :] * (_dot_f32(hid, w2_ref[...]) + b2_ref[...]))
    filt = _dot_f32(hid, w3_ref[...])
    dec = jnp.exp(-z[:, 0:1] * dl_ref[...])
    row = pl.program_id(0) * tl + lax.broadcasted_iota(jnp.int32, (tl, 1), 0)
    for o in range(HY_ORDER):
        base = o * 2 * HY_WIDTH
        hf = filt[:, base:base + HY_WIDTH] * dec
        hb = jnp.where(row == 0, 0.0, filt[:, base + HY_WIDTH:base + 2 * HY_WIDTH] * dec)
        hs_ref[:, o * HY_WIDTH:(o + 1) * HY_WIDTH] = hf + hb
        hd_ref[:, o * HY_WIDTH:(o + 1) * HY_WIDTH] = hf - hb


def _hy_filters(l, w1, b1, w2, b2, w3, freq):
    z, deltas = _hyena_positions(l)
    hid = HY_FILTER_HIDDEN
    w1p = jnp.zeros((LANES, LANES), F32).at[:HY_EMB, :hid].set(w1)
    w2p = jnp.zeros((LANES, LANES), F32).at[:hid, :hid].set(w2)
    w3p = jnp.zeros((LANES, w3.shape[1]), F32).at[:hid].set(w3)
    b1p = jnp.zeros((1, LANES), F32).at[0, :hid].set(b1)
    b2p = jnp.zeros((1, LANES), F32).at[0, :hid].set(b2)
    frp = jnp.zeros((2, LANES), F32).at[:, :hid].set(freq)
    tl = min(256, l)
    ncol = HY_ORDER * HY_WIDTH
    const = lambda i: (0, 0)
    return pl.pallas_call(
        _hy_filter_body,
        grid=(l // tl,),
        in_specs=[pl.BlockSpec((tl, LANES), lambda i: (i, 0)),
                  pl.BlockSpec((LANES, LANES), const), pl.BlockSpec((1, LANES), const),
                  pl.BlockSpec((LANES, LANES), const), pl.BlockSpec((1, LANES), const),
                  pl.BlockSpec((LANES, w3.shape[1]), const), pl.BlockSpec((2, LANES), const),
                  pl.BlockSpec((1, HY_WIDTH), const)],
        out_specs=[pl.BlockSpec((tl, ncol), lambda i: (i, 0)), pl.BlockSpec((tl, ncol), lambda i: (i, 0))],
        out_shape=[jax.ShapeDtypeStruct((l, ncol), F32), jax.ShapeDtypeStruct((l, ncol), F32)],
        compiler_params=_cparams(("parallel",), 32),
    )(z, w1p, b1p, w2p, b2p, w3p, frp, deltas)


def _hy_conv_body(u0_ref, u1_ref, u2_ref, c_ref, s_ref, hr0_ref, hi0_ref, hr1_ref, hi1_ref, b0_ref, b1_ref, o_ref,
                  xb_scr, pre_scr, pim_scr, z_scr):
    l = u0_ref.shape[1]
    tf = min(HY_ROW_TILE, l)
    sgn = jnp.where(lax.broadcasted_iota(jnp.int32, (l, 1), 0) % 2 == 0, 1.0, -1.0)

    def long_conv(x, gate_ref, hr_ref, hi_ref, b_ref, out_ref):
        xb_scr[...] = x.astype(BF16)
        x_nyq = jnp.sum(x * sgn, axis=0, keepdims=True) * hr_ref[l:l + 1, :]
        for r in range(l // tf):
            rows = pl.ds(r * tf, tf)
            xc = _dot(c_ref[rows, :], xb_scr[...])
            xs = _dot(s_ref[rows, :], xb_scr[...])
            hre = hr_ref[rows, :]
            him = hi_ref[rows, :]
            pre_scr[rows, :] = (xc * hre + xs * him).astype(BF16)
            pim_scr[rows, :] = (xc * him - xs * hre).astype(BF16)
        for r in range(l // tf):
            rows = pl.ds(r * tf, tf)
            y = _dot(c_ref[rows, :], pre_scr[...]) - _dot(s_ref[rows, :], pim_scr[...])
            y = y + sgn[r * tf:(r + 1) * tf] * x_nyq
            out_ref[rows, :] = gate_ref[0, rows, :] * (y + x[r * tf:(r + 1) * tf] * b_ref[0])

    long_conv(u0_ref[0], u1_ref, hr0_ref, hi0_ref, b0_ref, z_scr)
    long_conv(z_scr[...], u2_ref, hr1_ref, hi1_ref, b1_ref, o_ref.at[0])


def _hy_long_conv(u, cmat, smat, hre, him, bias):
    bsz, l, _ = u.shape
    tc = 256
    nb = HY_WIDTH // tc
    const = lambda j, i: (0, 0)
    single = pl.Buffered(1)
    bias3 = bias.reshape(HY_ORDER, 1, HY_WIDTH)
    return pl.pallas_call(
        _hy_conv_body,
        grid=(nb, bsz),
        in_specs=[pl.BlockSpec((1, l, tc), lambda j, i: (i, 0, j)),
                  pl.BlockSpec((1, l, tc), lambda j, i: (i, 0, nb + j)),
                  pl.BlockSpec((1, l, tc), lambda j, i: (i, 0, 2 * nb + j)),
                  pl.BlockSpec((l, l), const, pipeline_mode=single),
                  pl.BlockSpec((l, l), const, pipeline_mode=single),
                  pl.BlockSpec((l + SUBLANES, tc), lambda j, i: (0, j), pipeline_mode=single),
                  pl.BlockSpec((l, tc), lambda j, i: (0, j), pipeline_mode=single),
                  pl.BlockSpec((l + SUBLANES, tc), lambda j, i: (0, nb + j), pipeline_mode=single),
                  pl.BlockSpec((l, tc), lambda j, i: (0, nb + j), pipeline_mode=single),
                  pl.BlockSpec((1, 1, tc), lambda j, i: (0, 0, j)),
                  pl.BlockSpec((1, 1, tc), lambda j, i: (1, 0, j))],
        out_specs=pl.BlockSpec((1, l, tc), lambda j, i: (i, 0, j)),
        out_shape=jax.ShapeDtypeStruct((bsz, l, HY_WIDTH), F32),
        scratch_shapes=[pltpu.VMEM((l, tc), BF16), pltpu.VMEM((l, tc), BF16), pltpu.VMEM((l, tc), BF16),
                        pltpu.VMEM((l, tc), F32)],
        compiler_params=_cparams(("parallel", "parallel"), 56),
    )(u, u, u, cmat, smat, hre, him, hre, him, bias3, bias3)


def _hyena(proj3, conv_w, conv_b, w1, b1, w2, b2, w3, freq, bias):
    l = proj3.shape[1]
    u = _dwconv(proj3, COL_HY, (HY_ORDER + 1) * HY_WIDTH, conv_w, conv_b, act=False)
    hs, hd = _hy_filters(l, w1, b1, w2, b2, w3, freq)
    cmat, smat, c_sc, s_sc_neg = _dft_tables(l)
    tn = 256
    hre = _matmul(c_sc, hs.astype(BF16), l + SUBLANES, tn)
    him = _matmul(s_sc_neg, hd.astype(BF16), l, tn)
    return _hy_long_conv(u, cmat, smat, hre, him, bias)


def _out_proj_body(a_ref, b_ref, c_ref, d_ref, x_ref, w_ref, dn_ref, g_ref, beta_ref, o_ref):
    gw = GROUP_WIDTH
    h = _dot(a_ref[...].astype(BF16), w_ref[0:gw, :])
    h += _dot(b_ref[...].astype(BF16), w_ref[gw:2 * gw, :])
    h += _dot(c_ref[...].astype(BF16), w_ref[2 * gw:3 * gw, :])
    h += _dot(_rms(d_ref[...], dn_ref[...]).astype(BF16), w_ref[3 * gw:4 * gw, :])
    o_ref[...] = _layer_norm(ALPHA * x_ref[...] + h, g_ref[...], beta_ref[...])


def _out_proj_ln(a, b, c, d, x, w_out, hy_norm, g, beta):
    m = x.shape[0]
    tm = min(512, m)
    gw = GROUP_WIDTH
    const = lambda i: (0, 0)
    row = lambda i: (i, 0)
    return pl.pallas_call(
        _out_proj_body,
        grid=(m // tm,),
        in_specs=[pl.BlockSpec((tm, gw), row), pl.BlockSpec((tm, gw), row), pl.BlockSpec((tm, gw), row),
                  pl.BlockSpec((tm, gw), row), pl.BlockSpec((tm, D_MODEL), row),
                  pl.BlockSpec((D_MODEL, D_MODEL), const), pl.BlockSpec((1, gw), const),
                  pl.BlockSpec((1, D_MODEL), const), pl.BlockSpec((1, D_MODEL), const)],
        out_specs=pl.BlockSpec((tm, D_MODEL), row),
        out_shape=jax.ShapeDtypeStruct((m, D_MODEL), F32),
        compiler_params=_cparams(("parallel",), 48),
    )(a, b, c, d, x, w_out, hy_norm.reshape(1, -1), g.reshape(1, -1), beta.reshape(1, -1))


def _ffn_body(x_ref, wg_ref, wu_ref, wd_ref, g_ref, beta_ref, o_ref, xb_ref, acc_ref):
    j = pl.program_id(1)

    @pl.when(j == 0)
    def _():
        xb_ref[...] = x_ref[...].astype(BF16)
        acc_ref[...] = jnp.zeros_like(acc_ref)

    xb = xb_ref[...]
    hidden = (_silu(_dot(xb, wg_ref[...])) * _dot(xb, wu_ref[...])).astype(BF16)
    acc_ref[...] += _dot(hidden, wd_ref[...])

    @pl.when(j == pl.num_programs(1) - 1)
    def _():
        o_ref[...] = _layer_norm(ALPHA * x_ref[...] + acc_ref[...], g_ref[...], beta_ref[...])


def _ffn_ln(x, wg, wu, wd, g, beta):
    m = x.shape[0]
    dff = wg.shape[1]
    tm = min(512, m)
    tf = 512
    const = lambda i, j: (0, 0)
    return pl.pallas_call(
        _ffn_body,
        grid=(m // tm, dff // tf),
        in_specs=[pl.BlockSpec((tm, D_MODEL), lambda i, j: (i, 0)),
                  pl.BlockSpec((D_MODEL, tf), lambda i, j: (0, j)),
                  pl.BlockSpec((D_MODEL, tf), lambda i, j: (0, j)),
                  pl.BlockSpec((tf, D_MODEL), lambda i, j: (j, 0)),
                  pl.BlockSpec((1, D_MODEL), const), pl.BlockSpec((1, D_MODEL), const)],
        out_specs=pl.BlockSpec((tm, D_MODEL), lambda i, j: (i, 0)),
        out_shape=jax.ShapeDtypeStruct((m, D_MODEL), F32),
        scratch_shapes=[pltpu.VMEM((tm, D_MODEL), BF16), pltpu.VMEM((tm, D_MODEL), F32)],
        compiler_params=_cparams(("parallel", "arbitrary"), 48),
    )(x, wg, wu, wd, g.reshape(1, -1), beta.reshape(1, -1))


def _router_body(x_ref, w_ref, idx_ref, gate_ref):
    logits = _dot_f32(x_ref[...], w_ref[...])
    lane = lax.broadcasted_iota(jnp.int32, logits.shape, 1)
    neg = -jnp.inf
    logits = jnp.where(lane < N_EXPERTS, logits, neg)
    m1 = jnp.max(logits, axis=-1, keepdims=True)
    i1 = jnp.min(jnp.where(logits == m1, lane, LANES), axis=-1, keepdims=True)
    rest = jnp.where(lane == i1, neg, logits)
    m2 = jnp.max(rest, axis=-1, keepdims=True)
    i2 = jnp.min(jnp.where(rest == m2, lane, LANES), axis=-1, keepdims=True)
    e2 = jnp.exp(m2 - m1)
    den = 1.0 + e2
    idx_ref[...] = jnp.where(lane == 0, i1, jnp.where(lane == 1, i2, 0))
    gate_ref[...] = jnp.where(lane == 0, 1.0 / den, jnp.where(lane == 1, e2 / den, 0.0))


def _router(x, w_router):
    m = x.shape[0]
    tm = min(512, m)
    wp = jnp.zeros((D_MODEL, LANES), F32).at[:, :N_EXPERTS].set(w_router)
    return pl.pallas_call(
        _router_body,
        grid=(m // tm,),
        in_specs=[pl.BlockSpec((tm, D_MODEL), lambda i: (i, 0)), pl.BlockSpec((D_MODEL, LANES), lambda i: (0, 0))],
        out_specs=[pl.BlockSpec((tm, LANES), lambda i: (i, 0)), pl.BlockSpec((tm, LANES), lambda i: (i, 0))],
        out_shape=[jax.ShapeDtypeStruct((m, LANES), jnp.int32), jax.ShapeDtypeStruct((m, LANES), F32)],
        compiler_params=_cparams(("parallel",), 32),
    )(x, wp)


def _gather_body(rows, idx_ref, src_ref, dst_ref, sem):
    base = pl.program_id(0) * rows

    def copy(src_row, dst_row):
        return pltpu.make_async_copy(src_ref.at[pl.ds(src_row, 1)], dst_ref.at[pl.ds(dst_row, 1)], sem)

    def start(r, carry):
        copy(idx_ref[0, 0, r], base + r).start()
        return carry

    def wait(r, carry):
        copy(0, 0).wait()
        return carry

    lax.fori_loop(0, rows, start, 0)
    lax.fori_loop(0, rows, wait, 0)


def _gather_rows(src, idx):
    n_out = idx.shape[0]
    rows = 256
    assert n_out % rows == 0
    return pl.pallas_call(
        functools.partial(_gather_body, rows),
        grid=(n_out // rows,),
        in_specs=[pl.BlockSpec((1, 1, rows), lambda i: (i, 0, 0), memory_space=pltpu.SMEM),
                  pl.BlockSpec(memory_space=pl.ANY)],
        out_specs=pl.BlockSpec(memory_space=pl.ANY),
        out_shape=jax.ShapeDtypeStruct((n_out,) + src.shape[1:], src.dtype),
        scratch_shapes=[pltpu.SemaphoreType.DMA(())],
        compiler_params=pltpu.CompilerParams(dimension_semantics=("arbitrary",)),
    )(idx.reshape(n_out // rows, 1, rows), src)


def _expert_body(be_ref, nu_ref, x_ref, wg_ref, wu_ref, wd_ref, o_ref, xb_ref):
    i = pl.program_id(0)
    j = pl.program_id(1)

    @pl.when(j == 0)
    def _():
        xb_ref[...] = x_ref[...].astype(BF16)
        o_ref[...] = jnp.zeros_like(o_ref)

    @pl.when(i < nu_ref[0])
    def _():
        xb = xb_ref[...]
        hidden = (_silu(_dot(xb, wg_ref[0])) * _dot(xb, wu_ref[0])).astype(BF16)
        o_ref[...] += _dot(hidden, wd_ref[0])


def _expert_ffn(x_slots, block_expert, n_used, wg, wu, wd, tm):
    cap = x_slots.shape[0]
    dff = wg.shape[2]
    tf = 512
    grid_spec = pltpu.PrefetchScalarGridSpec(
        num_scalar_prefetch=2,
        grid=(cap // tm, dff // tf),
        in_specs=[pl.BlockSpec((tm, D_MODEL), lambda i, j, be, nu: (i, 0)),
                  pl.BlockSpec((1, D_MODEL, tf), lambda i, j, be, nu: (be[i], 0, j)),
                  pl.BlockSpec((1, D_MODEL, tf), lambda i, j, be, nu: (be[i], 0, j)),
                  pl.BlockSpec((1, tf, D_MODEL), lambda i, j, be, nu: (be[i], j, 0))],
        out_specs=pl.BlockSpec((tm, D_MODEL), lambda i, j, be, nu: (i, 0)),
        scratch_shapes=[pltpu.VMEM((tm, D_MODEL), BF16)],
    )
    return pl.pallas_call(
        _expert_body,
        grid_spec=grid_spec,
        out_shape=jax.ShapeDtypeStruct((cap, D_MODEL), F32),
        compiler_params=_cparams(("parallel", "arbitrary"), 56),
    )(block_expert, n_used, x_slots, wg, wu, wd)


def _combine_body(x_ref, y0_ref, y1_ref, gt_ref, g_ref, beta_ref, o_ref):
    gt = gt_ref[...]
    f = gt[:, 0:1] * y0_ref[0] + gt[:, 1:2] * y1_ref[0]
    o_ref[...] = _layer_norm(ALPHA * x_ref[...] + f, g_ref[...], beta_ref[...])


def _combine_ln(x, y2, gates, g, beta):
    m = x.shape[0]
    tm = min(512, m)
    const = lambda i: (0, 0)
    return pl.pallas_call(
        _combine_body,
        grid=(m // tm,),
        in_specs=[pl.BlockSpec((tm, D_MODEL), lambda i: (i, 0)),
                  pl.BlockSpec((1, tm, D_MODEL), lambda i: (0, i, 0)),
                  pl.BlockSpec((1, tm, D_MODEL), lambda i: (1, i, 0)),
                  pl.BlockSpec((tm, LANES), lambda i: (i, 0)),
                  pl.BlockSpec((1, D_MODEL), const), pl.BlockSpec((1, D_MODEL), const)],
        out_specs=pl.BlockSpec((tm, D_MODEL), lambda i: (i, 0)),
        out_shape=jax.ShapeDtypeStruct((m, D_MODEL), F32),
        compiler_params=_cparams(("parallel",), 48),
    )(x, y2, y2, gates, g.reshape(1, -1), beta.reshape(1, -1))


def _moe_ln(x, w_router, wg, wu, wd, g, beta, tm):
    m = x.shape[0]
    n_asg = m * TOP_K
    idx, gates = _router(x, w_router)
    e_flat = idx[:, :TOP_K].reshape(-1)
    onehot = (e_flat[:, None] == jnp.arange(N_EXPERTS, dtype=jnp.int32)[None, :]).astype(jnp.int32)
    csum = jnp.cumsum(onehot, axis=0)
    counts = csum[-1]
    rank = jnp.sum(csum * onehot, axis=1) - 1
    padded = (counts + tm - 1) // tm * tm
    pend = jnp.cumsum(padded)
    pstart = pend - padded
    dest = (jnp.sum(pstart[None, :] * onehot, axis=1) + rank).astype(jnp.int32)
    cap = n_asg + N_EXPERTS * tm
    tok = jnp.arange(n_asg, dtype=jnp.int32) // TOP_K
    slot_tok = jnp.zeros((cap,), jnp.int32).at[dest].set(tok)
    nblk = cap // tm
    block_expert = jnp.minimum(jnp.searchsorted(pend, jnp.arange(nblk, dtype=pend.dtype) * tm, side='right'),
                               N_EXPERTS - 1).astype(jnp.int32)
    n_used = (pend[-1] // tm).astype(jnp.int32).reshape(1)

    x_slots = _gather_rows(x, slot_tok)
    y_slots = _expert_ffn(x_slots, block_expert, n_used, wg, wu, wd, tm)
    y2 = _gather_rows(y_slots, dest.reshape(m, TOP_K).T.reshape(-1)).reshape(TOP_K, m, D_MODEL)
    return _combine_ln(x, y2, gates, g, beta)


def _mixer_ln(x2, bsz, s, p):
    proj = _matmul(x2, p['w_in'], min(1024, x2.shape[0]), 896)
    proj3 = proj.reshape(bsz, s, PROJ_COLS)
    cos_slab, sin_slab, cos4, sin4 = _rope_tables(s)
    out_a = _mla(proj3, cos_slab, sin_slab, p['mla_q_norm'], p['w_q'], p['w_qr'], p['mla_kv_norm'], p['w_kv'],
                 p['mla_out_norm'])
    out_b = _retention(proj3, cos4, sin4)
    xbc_act = _dwconv(proj3, COL_XBC, SSD_CONV_CH, p['ssd_conv_w'], p['ssd_conv_b'], act=True)
    out_c = _ssd(proj3, xbc_act, p['ssd_dt_bias'], p['ssd_a_log'], p['ssd_d'], p['ssd_norm'])
    out_d = _hyena(proj3, p['hy_conv_w'], p['hy_conv_b'], p['hy_w1'], p['hy_b1'], p['hy_w2'], p['hy_b2'], p['hy_w3'],
                   p['hy_freq'], p['hy_bias'])
    m = bsz * s
    gw = GROUP_WIDTH
    return _out_proj_ln(out_a.reshape(m, gw), out_b.reshape(m, gw), out_c.reshape(m, gw), out_d.reshape(m, gw),
                        x2, p['w_out'], p['hy_out_norm'], p['ln1_g'], p['ln1_b'])


_MIXER_KEYS = ('mla_q_norm', 'mla_kv_norm', 'mla_out_norm', 'ssd_conv_w', 'ssd_conv_b', 'ssd_dt_bias', 'ssd_a_log',
               'ssd_d', 'ssd_norm', 'hy_conv_w', 'hy_conv_b', 'hy_w1', 'hy_b1', 'hy_w2', 'hy_b2', 'hy_w3', 'hy_freq',
               'hy_bias', 'hy_out_norm', 'ln1_g', 'ln1_b')

MOE_TM = 1024


def kernel(x, w_in, mla_q_norm, mla_w_uq, mla_kv_norm, mla_w_ukv, mla_out_norm, ssd_conv_w, ssd_conv_b, ssd_dt_bias, ssd_a_log, ssd_d, ssd_norm, hy_conv_w, hy_conv_b, hy_w1, hy_b1, hy_w2, hy_b2, hy_w3, hy_freq, hy_bias, hy_out_norm, w_out, ln1_g, ln1_b, ln2_g, ln2_b, ffn_w_gate, ffn_w_up, ffn_w_down, moe_router, moe_w_gate, moe_w_up, moe_w_down):
    args = dict(locals())
    bsz, s, d = x.shape
    x2 = x.reshape(bsz * s, d)
    for layer in range(DEPTH):
        p = {k: args[k][layer] for k in _MIXER_KEYS}
        p['w_in'] = _prep_w_in(w_in[layer])
        p['w_q'], p['w_qr'] = _prep_w_uq(mla_w_uq[layer])
        p['w_kv'] = _prep_w_ukv(mla_w_ukv[layer])
        p['w_out'] = w_out[layer].astype(BF16)
        x2 = _mixer_ln(x2, bsz, s, p)
        j = layer // 2
        if layer % 2 == 0:
            x2 = _ffn_ln(x2, ffn_w_gate[j].astype(BF16), ffn_w_up[j].astype(BF16), ffn_w_down[j].astype(BF16),
                         ln2_g[layer], ln2_b[layer])
        else:
            x2 = _moe_ln(x2, moe_router[j], moe_w_gate[j].astype(BF16), moe_w_up[j].astype(BF16),
                         moe_w_down[j].astype(BF16), ln2_g[layer], ln2_b[layer], MOE_TM)
    return x2.reshape(bsz, s, d)
```

```python
import functools
import math

import numpy as np
import jax
import jax.numpy as jnp
from jax import lax
from jax.experimental import pallas as pl
from jax.experimental.pallas import tpu as pltpu

F32 = jnp.float32
BF16 = jnp.bfloat16
HIGHEST = lax.Precision.HIGHEST

D_MODEL = 2048
DEPTH = 2
GROUP_WIDTH = 512
MLA_HEADS = 4
MLA_NOPE = 128
MLA_ROPE = 64
MLA_V = 128
MLA_Q_LORA = 384
MLA_KV_LORA = 256
RET_HEADS = 4
RET_DV = 128
RET_DK = 64
RET_DECAY_EXP_FWD = 5.0
RET_DECAY_EXP_BWD = 5.5
SSD_HEADDIM = 64
SSD_HEADS = 8
SSD_GROUPS = 2
SSD_STATE = 128
SSD_CONV = 5
SSD_CONV_CH = 1024
HY_ORDER = 2
HY_WIDTH = 512
HY_SHORT = 3
HY_EMB = 33
HY_FILTER_HIDDEN = 64
HY_MIN_DECAY = math.log(1e-2) / 1.5
HY_MAX_DECAY = math.log(1e-2) / 0.3
N_EXPERTS = 8
TOP_K = 2
ROPE_BASE = 10000.0
ALPHA = (2 * DEPTH) ** 0.25

V7X_VMEM_BYTES = 64 * 1024 * 1024
LANES = 128
SUBLANES = 8

PROJ_COLS = 6272
COL_HY = 0
COL_QC = 1536
COL_RQ = 2048
COL_KVPE = 4096
COL_Z = 4608
COL_XBC = 5120
COL_DT = 6144

RET_CHUNK = 256
SSD_CHUNK = 128
HY_ROW_TILE = 512
ROW_CHUNKS = D_MODEL // LANES
GATHER_ROWS = 512
COMBINE_ROWS = 256


def _cparams(semantics, vmem_mb):
    assert vmem_mb * 1024 * 1024 < V7X_VMEM_BYTES
    return pltpu.CompilerParams(dimension_semantics=semantics, vmem_limit_bytes=vmem_mb * 1024 * 1024)


def _sigmoid(x):
    return 1.0 / (1.0 + jnp.exp(-x))


def _silu(x):
    return x * _sigmoid(x)


def _rms(x, w, eps=1e-6):
    return x * lax.rsqrt(jnp.mean(x * x, axis=-1, keepdims=True) + eps) * w


def _layer_norm(y, g, b, eps=1e-5):
    mu = jnp.mean(y, axis=-1, keepdims=True)
    d = y - mu
    var = jnp.mean(d * d, axis=-1, keepdims=True)
    return d * lax.rsqrt(var + eps) * g + b


def _dot(a, b):
    return jnp.dot(a, b, preferred_element_type=F32)


def _dot_nt(a, b):
    return lax.dot_general(a, b, (((1,), (1,)), ((), ())), preferred_element_type=F32)


def _dot_tn(a, b):
    return lax.dot_general(a, b, (((0,), (0,)), ((), ())), preferred_element_type=F32)


def _dot_f32(a, b):
    return jnp.dot(a, b, preferred_element_type=F32, precision=HIGHEST)


def _mm_body(x_ref, w_ref, o_ref, xb_ref):
    @pl.when(pl.program_id(1) == 0)
    def _():
        xb_ref[...] = x_ref[...].astype(BF16)

    o_ref[...] = _dot(xb_ref[...], w_ref[...])


def _matmul(x, w, tm, tn, vmem_mb=48):
    m, k = x.shape
    n = w.shape[1]
    assert m % tm == 0 and n % tn == 0
    return pl.pallas_call(
        _mm_body,
        grid=(m // tm, n // tn),
        in_specs=[pl.BlockSpec((tm, k), lambda i, j: (i, 0)),
                  pl.BlockSpec((k, tn), lambda i, j: (0, j))],
        out_specs=pl.BlockSpec((tm, tn), lambda i, j: (i, j)),
        out_shape=jax.ShapeDtypeStruct((m, n), F32),
        scratch_shapes=[pltpu.VMEM((tm, k), BF16)],
        compiler_params=_cparams(("parallel", "arbitrary"), vmem_mb),
    )(x, w)


def _rot_half_cols(w, heads):
    k = w.shape[0]
    w = w.reshape(k, heads, 2, 32)
    return jnp.stack([-w[:, :, 1], w[:, :, 0]], axis=2).reshape(k, heads * 64)


def _prep_w_in(w):
    k = w.shape[0]

    def sl(a, b):
        return w[:, a:b]

    def zc(n):
        return jnp.zeros((k, n), w.dtype)

    q_c, kv_c, k_pe = sl(0, 384), sl(384, 640), sl(640, 704)
    r_q, r_k, r_v, r_g = sl(704, 960), sl(960, 1216), sl(1216, 1728), sl(1728, 2240)
    m_z, m_xbc, m_dt, h_u = sl(2240, 2752), sl(2752, 3776), sl(3776, 3792), sl(3792, 5328)
    cols = [h_u, q_c, zc(128),
            r_q, r_k, _rot_half_cols(r_q, RET_HEADS), _rot_half_cols(r_k, RET_HEADS), r_v, r_g,
            kv_c, k_pe, zc(64), _rot_half_cols(k_pe, 1), zc(64),
            m_z, m_xbc, m_dt, zc(112)]
    out = jnp.concatenate(cols, axis=1).astype(BF16)
    assert out.shape[1] == PROJ_COLS
    return out


def _prep_w_uq(w):
    k = w.shape[0]
    w = w.reshape(k, MLA_HEADS, MLA_NOPE + MLA_ROPE)
    nope, rope = w[:, :, :MLA_NOPE], w[:, :, MLA_NOPE:]
    z = jnp.zeros((k, MLA_HEADS, 64), w.dtype)
    main = jnp.concatenate([nope, rope, z], axis=2).reshape(k, MLA_HEADS * 256)
    rr = _rot_half_cols(rope.reshape(k, MLA_HEADS * 64), MLA_HEADS).reshape(k, MLA_HEADS, 64)
    rot = jnp.concatenate([rr, z], axis=2).reshape(k, MLA_HEADS * 128)
    return main.astype(BF16), rot.astype(BF16)


def _prep_w_ukv(w):
    k = w.shape[0]
    w = w.reshape(k, MLA_HEADS, MLA_NOPE + MLA_V)
    return jnp.concatenate([w[:, :, :MLA_NOPE].reshape(k, -1), w[:, :, MLA_NOPE:].reshape(k, -1)], axis=1).astype(BF16)


def _rope_tables(s):
    half = 32
    inv_freq = ROPE_BASE ** (-jnp.arange(half, dtype=F32) * 2.0 / 64)
    ang = jnp.arange(s, dtype=F32)[:, None] * inv_freq[None, :]
    cos, sin = jnp.cos(ang), jnp.sin(ang)
    cos64 = jnp.concatenate([cos, cos], axis=1)
    sin64 = jnp.concatenate([sin, sin], axis=1)
    z = jnp.zeros((s, 64), F32)
    return (jnp.concatenate([cos64, z], axis=1), jnp.concatenate([sin64, z], axis=1),
            jnp.tile(cos64, (1, 4)), jnp.tile(sin64, (1, 4)))


def _ret_tables(n):
    heads = jnp.arange(RET_HEADS, dtype=F32)
    lgf = jnp.log1p(-jnp.exp2(-RET_DECAY_EXP_FWD - heads))
    lgb = jnp.log1p(-jnp.exp2(-RET_DECAY_EXP_BWD - heads))
    idx = jnp.arange(n, dtype=F32)

    def lanes(tab, width):
        return jnp.repeat(tab, width, axis=1)

    kdf = lanes(jnp.exp((n - 1.0 - idx)[:, None] * lgf), RET_DK)
    kdb = lanes(jnp.exp(idx[:, None] * lgb), RET_DK)
    qdf = lanes(jnp.exp((idx + 1.0)[:, None] * lgf), RET_DK)
    qdb = lanes(jnp.exp((n - idx)[:, None] * lgb), RET_DK)
    cdf = lanes(jnp.exp(n * lgf)[None, :], RET_DV)
    cdb = lanes(jnp.exp(n * lgb)[None, :], RET_DV)
    diff = idx[:, None] - idx[None, :]
    dec = jnp.where(diff[None] >= 0,
                    jnp.exp(jnp.maximum(diff, 0.0)[None] * lgf[:, None, None]),
                    jnp.exp(jnp.maximum(-diff, 0.0)[None] * lgb[:, None, None]))
    bd = (jnp.arange(RET_HEADS * RET_DK)[:, None] // RET_DK == jnp.arange(RET_HEADS * RET_DV)[None, :] // RET_DV)
    return kdf, kdb, qdf, qdb, cdf, cdb, dec.astype(F32), bd.astype(F32)


def _ssd_tables(n):
    tri = (jnp.arange(n)[:, None] >= jnp.arange(n)[None, :]).astype(F32)
    lane_head = jnp.arange(GROUP_WIDTH) // SSD_HEADDIM
    r = jnp.arange(LANES)
    ef = (r[:, None] == lane_head[None, :]).astype(BF16)
    eb = (r[:, None] == lane_head[None, :] + SSD_HEADS).astype(BF16)
    e = jnp.concatenate([ef, eb], axis=1)
    return tri, jnp.concatenate([e, e], axis=0)


def _dft_tables(l):
    f = jnp.arange(l, dtype=jnp.int32)
    k = (f[:, None] * f[None, :]) % (2 * l)
    ang = k.astype(F32) * (math.pi / l)
    c, s = jnp.cos(ang), jnp.sin(ang)
    sc = jnp.where(f == 0, 1.0, 2.0).astype(F32)[:, None] / (2.0 * l)
    nyq = jnp.where(f % 2 == 0, 1.0, -1.0).astype(F32)[None, :] / (2.0 * l)
    c_sc = jnp.concatenate([c * sc, nyq, jnp.zeros((7, l), F32)], axis=0)
    return c.astype(BF16), s.astype(BF16), c_sc.astype(BF16), (-s * sc).astype(BF16)


def _hyena_positions(l):
    t = jnp.linspace(0.0, 1.0, l, dtype=F32)[:, None]
    bands = (HY_EMB - 1) // 2
    ang = 2.0 * math.pi * jnp.arange(l, dtype=F32)[:, None] / l
    f = jnp.linspace(1e-4, bands - 1, bands, dtype=F32)[None, :]
    z = jnp.concatenate([t, jnp.cos(f * ang), -jnp.sin(f * ang), jnp.zeros((l, LANES - HY_EMB), F32)], axis=-1)
    deltas = jnp.abs(jnp.linspace(HY_MIN_DECAY, HY_MAX_DECAY, HY_WIDTH, dtype=F32))[None, :]
    return z, deltas


def _conv_body(taps, act, x_ref, p_ref, n_ref, w_ref, b_ref, o_ref):
    r = pl.program_id(1)
    tr = x_ref.shape[1]
    half = taps // 2
    prev = jnp.where(r > 0, p_ref[0], 0.0)
    nxt = jnp.where(r < pl.num_programs(1) - 1, n_ref[0], 0.0)
    win = jnp.concatenate([prev, x_ref[0], nxt], axis=0)
    rows = tr + 2 * SUBLANES
    acc = jnp.broadcast_to(b_ref[...], (tr, x_ref.shape[2]))
    for t in range(taps):
        sh = (half - t) % rows
        shifted = win if sh == 0 else pltpu.roll(win, sh, axis=0)
        acc = acc + shifted[SUBLANES:SUBLANES + tr] * w_ref[t:t + 1, :]
    if act:
        acc = _silu(acc)
    o_ref[0] = acc


def _dwconv(proj3, col0, width, w, b, act):
    bsz, s, _ = proj3.shape
    taps = w.shape[0]
    cb = 512
    tr = min(512, s)
    off = col0 // cb
    nb8 = s // SUBLANES
    t8 = tr // SUBLANES
    return pl.pallas_call(
        functools.partial(_conv_body, taps, act),
        grid=(bsz, s // tr, width // cb),
        in_specs=[pl.BlockSpec((1, tr, cb), lambda i, r, c: (i, r, c + off)),
                  pl.BlockSpec((1, SUBLANES, cb), lambda i, r, c: (i, jnp.maximum(r * t8 - 1, 0), c + off)),
                  pl.BlockSpec((1, SUBLANES, cb), lambda i, r, c: (i, jnp.minimum((r + 1) * t8, nb8 - 1), c + off)),
                  pl.BlockSpec((taps, cb), lambda i, r, c: (0, c)),
                  pl.BlockSpec((1, cb), lambda i, r, c: (0, c))],
        out_specs=pl.BlockSpec((1, tr, cb), lambda i, r, c: (i, r, c)),
        out_shape=jax.ShapeDtypeStruct((bsz, s, width), F32),
        compiler_params=_cparams(("parallel", "parallel", "parallel"), 32),
    )(proj3, proj3, proj3, w, b.reshape(1, width))


def _mla_body(qc_ref, kvpe_ref, cq_ref, sq_ref, ck_ref, sk_ref, qn_ref, wq_ref, wqr_ref, kvn_ref, wkv_ref, on_ref,
              o_ref, k_scr, v_scr):
    @pl.when(pl.program_id(1) == 0)
    def _():
        kvpe = kvpe_ref[0]
        kvn = _rms(kvpe[:, :MLA_KV_LORA], kvn_ref[...]).astype(BF16)
        kpe = (kvpe[:, 256:384] * ck_ref[...] + kvpe[:, 384:512] * sk_ref[...]).astype(BF16)
        for h in range(MLA_HEADS):
            kn = _dot(kvn, wkv_ref[:, h * 128:(h + 1) * 128]).astype(BF16)
            k_scr[h] = jnp.concatenate([kn, kpe], axis=1)
            v_scr[h] = _dot(kvn, wkv_ref[:, 512 + h * 128:512 + (h + 1) * 128]).astype(BF16)

    scale = (MLA_NOPE + MLA_ROPE) ** -0.5
    qn = _rms(qc_ref[0][:, :MLA_Q_LORA], qn_ref[...]).astype(BF16)
    outs = []
    for h in range(MLA_HEADS):
        qm = _dot(qn, wq_ref[:, h * 256:(h + 1) * 256])
        qr = _dot(qn, wqr_ref[:, h * 128:(h + 1) * 128])
        qpe = qm[:, 128:] * cq_ref[...] + qr * sq_ref[...]
        qh = (jnp.concatenate([qm[:, :128], qpe], axis=1) * scale).astype(BF16)
        sc = _dot_nt(qh, k_scr[h])
        p = jnp.exp(sc - jnp.max(sc, axis=-1, keepdims=True))
        den = jnp.sum(p, axis=-1, keepdims=True)
        outs.append(_dot(p.astype(BF16), v_scr[h]) / den)
    o_ref[0] = _rms(jnp.concatenate(outs, axis=1), on_ref[...])


def _mla(proj3, cos_slab, sin_slab, q_norm, w_q, w_qr, kv_norm, w_kv, out_norm):
    bsz, s, _ = proj3.shape
    tq = min(512, s)
    const = lambda i, j: (0, 0)
    return pl.pallas_call(
        _mla_body,
        grid=(bsz, s // tq),
        in_specs=[pl.BlockSpec((1, tq, 512), lambda i, j: (i, j, COL_QC // 512)),
                  pl.BlockSpec((1, s, 512), lambda i, j: (i, 0, COL_KVPE // 512)),
                  pl.BlockSpec((tq, 128), lambda i, j: (j, 0)),
                  pl.BlockSpec((tq, 128), lambda i, j: (j, 0)),
                  pl.BlockSpec((s, 128), const),
                  pl.BlockSpec((s, 128), const),
                  pl.BlockSpec((1, MLA_Q_LORA), const),
                  pl.BlockSpec(w_q.shape, const),
                  pl.BlockSpec(w_qr.shape, const),
                  pl.BlockSpec((1, MLA_KV_LORA), const),
                  pl.BlockSpec(w_kv.shape, const),
                  pl.BlockSpec((1, GROUP_WIDTH), const)],
        out_specs=pl.BlockSpec((1, tq, GROUP_WIDTH), lambda i, j: (i, j, 0)),
        out_shape=jax.ShapeDtypeStruct((bsz, s, GROUP_WIDTH), F32),
        scratch_shapes=[pltpu.VMEM((MLA_HEADS, s, 256), BF16), pltpu.VMEM((MLA_HEADS, s, MLA_V), BF16)],
        compiler_params=_cparams(("parallel", "arbitrary"), 48),
    )(proj3, proj3, cos_slab, sin_slab, cos_slab, sin_slab, q_norm.reshape(1, -1), w_q, w_qr,
      kv_norm.reshape(1, -1), w_kv, out_norm.reshape(1, -1))


def _ret_state_body(nchunks, rk_ref, rkr_ref, rv_ref, cos_ref, sin_ref, kdf_ref, kdb_ref, cdf_ref, cdb_ref, bd_ref,
                    sf_ref, sb_ref, st_scr):
    t = pl.program_id(1)
    k = (rk_ref[0] * cos_ref[...] + rkr_ref[0] * sin_ref[...]) * (RET_DK ** -0.5)
    v = rv_ref[0].astype(BF16)

    @pl.when((t == 0) | (t == nchunks))
    def _():
        st_scr[...] = jnp.zeros_like(st_scr)

    def step(out_ref, kd_ref, cd_ref):
        out_ref[0, 0] = st_scr[...]
        new = _dot_tn((k * kd_ref[...]).astype(BF16), v)
        st_scr[...] = st_scr[...] * cd_ref[...] + new * bd_ref[...]

    @pl.when(t < nchunks)
    def _():
        step(sf_ref, kdf_ref, cdf_ref)

    @pl.when(t >= nchunks)
    def _():
        step(sb_ref, kdb_ref, cdb_ref)


def _ret_out_body(rq_ref, rk_ref, rqr_ref, rkr_ref, rv_ref, rg_ref, cos_ref, sin_ref, qdf_ref, qdb_ref, dec_ref,
                  sf_ref, sb_ref, o_ref):
    cos, sin = cos_ref[...], sin_ref[...]
    q = rq_ref[0] * cos + rqr_ref[0] * sin
    kb = ((rk_ref[0] * cos + rkr_ref[0] * sin) * (RET_DK ** -0.5)).astype(BF16)
    vb = rv_ref[0].astype(BF16)
    cross = (_dot((q * qdf_ref[...]).astype(BF16), sf_ref[0, 0].astype(BF16))
             + _dot((q * qdb_ref[...]).astype(BF16), sb_ref[0, 0].astype(BF16)))
    lane_head = lax.broadcasted_iota(jnp.int32, (1, RET_HEADS * RET_DK), 1) // RET_DK
    outs = []
    for h in range(RET_HEADS):
        qh = jnp.where(lane_head == h, q, 0.0).astype(BF16)
        sc = _dot_nt(qh, kb) * dec_ref[h]
        y = _dot(sc.astype(BF16), vb[:, h * RET_DV:(h + 1) * RET_DV]) + cross[:, h * RET_DV:(h + 1) * RET_DV]
        mu = jnp.mean(y, axis=-1, keepdims=True)
        d = y - mu
        var = jnp.mean(d * d, axis=-1, keepdims=True)
        outs.append(d * lax.rsqrt(var + 1e-6))
    o_ref[0] = jnp.concatenate(outs, axis=1) * _silu(rg_ref[0])


def _retention(proj3, cos4, sin4):
    bsz, s, _ = proj3.shape
    n = min(RET_CHUNK, s)
    c = s // n
    kdf, kdb, qdf, qdb, cdf, cdb, dec, bd = _ret_tables(n)
    c256 = COL_RQ // 256
    c512 = COL_RQ // 512
    const2 = lambda i, t: (0, 0)

    def chunk(t):
        return jnp.where(t < c, t, 2 * c - 1 - t)

    state_shape = jax.ShapeDtypeStruct((bsz, c, RET_HEADS * RET_DK, GROUP_WIDTH), F32)
    sf, sb = pl.pallas_call(
        functools.partial(_ret_state_body, c),
        grid=(bsz, 2 * c),
        in_specs=[pl.BlockSpec((1, n, 256), lambda i, t: (i, chunk(t), c256 + 1)),
                  pl.BlockSpec((1, n, 256), lambda i, t: (i, chunk(t), c256 + 3)),
                  pl.BlockSpec((1, n, 512), lambda i, t: (i, chunk(t), c512 + 2)),
                  pl.BlockSpec((n, 256), lambda i, t: (chunk(t), 0)),
                  pl.BlockSpec((n, 256), lambda i, t: (chunk(t), 0)),
                  pl.BlockSpec((n, 256), const2), pl.BlockSpec((n, 256), const2),
                  pl.BlockSpec((1, 512), const2), pl.BlockSpec((1, 512), const2),
                  pl.BlockSpec((256, 512), const2)],
        out_specs=[pl.BlockSpec((1, 1, 256, 512), lambda i, t: (i, jnp.minimum(t, c - 1), 0, 0)),
                   pl.BlockSpec((1, 1, 256, 512), lambda i, t: (i, jnp.minimum(2 * c - 1 - t, c - 1), 0, 0))],
        out_shape=[state_shape, state_shape],
        scratch_shapes=[pltpu.VMEM((RET_HEADS * RET_DK, GROUP_WIDTH), F32)],
        compiler_params=_cparams(("parallel", "arbitrary"), 32),
    )(proj3, proj3, proj3, cos4, sin4, kdf, kdb, cdf, cdb, bd)

    const3 = lambda i, j: (0, 0, 0)
    const2 = lambda i, j: (0, 0)
    return pl.pallas_call(
        _ret_out_body,
        grid=(bsz, c),
        in_specs=[pl.BlockSpec((1, n, 256), lambda i, j: (i, j, c256)),
                  pl.BlockSpec((1, n, 256), lambda i, j: (i, j, c256 + 1)),
                  pl.BlockSpec((1, n, 256), lambda i, j: (i, j, c256 + 2)),
                  pl.BlockSpec((1, n, 256), lambda i, j: (i, j, c256 + 3)),
                  pl.BlockSpec((1, n, 512), lambda i, j: (i, j, c512 + 2)),
                  pl.BlockSpec((1, n, 512), lambda i, j: (i, j, c512 + 3)),
                  pl.BlockSpec((n, 256), lambda i, j: (j, 0)),
                  pl.BlockSpec((n, 256), lambda i, j: (j, 0)),
                  pl.BlockSpec((n, 256), const2), pl.BlockSpec((n, 256), const2),
                  pl.BlockSpec((RET_HEADS, n, n), const3),
                  pl.BlockSpec((1, 1, 256, 512), lambda i, j: (i, j, 0, 0)),
                  pl.BlockSpec((1, 1, 256, 512), lambda i, j: (i, j, 0, 0))],
        out_specs=pl.BlockSpec((1, n, GROUP_WIDTH), lambda i, j: (i, j, 0)),
        out_shape=jax.ShapeDtypeStruct((bsz, s, GROUP_WIDTH), F32),
        compiler_params=_cparams(("parallel", "parallel"), 32),
    )(proj3, proj3, proj3, proj3, proj3, proj3, cos4, sin4, qdf, qdb, dec, sf, sb)


def _ssd_decays(dt_ref, dtb_ref, a_ref, tri_ref):
    dt_raw = dt_ref[0] + dtb_ref[...]
    dt = jnp.maximum(dt_raw, 0.0) + jnp.log1p(jnp.exp(-jnp.abs(dt_raw)))
    la = dt * a_ref[...]
    return dt, la, _dot_f32(tri_ref[...], la)


def _expand_heads(v, e_ref):
    hi = v.astype(BF16)
    lo = (v - hi.astype(F32)).astype(BF16)
    return _dot(jnp.concatenate([hi, lo], axis=1), e_ref[...])


def _ssd_state_body(nchunks, xbc_ref, dt_ref, dtb_ref, a_ref, tri_ref, e_ref, sf_ref, sb_ref, st_scr):
    t = pl.program_id(1)
    n = xbc_ref.shape[1]
    w = GROUP_WIDTH
    dt, la, cs = _ssd_decays(dt_ref, dtb_ref, a_ref, tri_ref)
    tot = cs[n - 1:n, :]
    fwd_lane = lax.broadcasted_iota(jnp.int32, (1, LANES), 1) < SSD_HEADS
    wgt = dt * jnp.exp(jnp.where(fwd_lane, tot - cs, cs - la))
    both = _expand_heads(jnp.concatenate([wgt, jnp.broadcast_to(jnp.exp(tot), (SUBLANES, LANES))], axis=0), e_ref)
    wgt_e = both[:n]
    dec_e = both[n:n + 1]
    xs = xbc_ref[0][:, :w]

    @pl.when((t == 0) | (t == nchunks))
    def _():
        st_scr[...] = jnp.zeros_like(st_scr)

    def step(out_ref, lo):
        out_ref[0, 0] = st_scr[...]
        xw = (xs * wgt_e[:, lo:lo + w]).astype(BF16)
        new = []
        for g in range(SSD_GROUPS):
            bg = xbc_ref[0][:, w + g * SSD_STATE:w + (g + 1) * SSD_STATE].astype(BF16)
            new.append(_dot_tn(bg, xw[:, g * 256:(g + 1) * 256]))
        st_scr[...] = st_scr[...] * dec_e[:, lo:lo + w] + jnp.concatenate(new, axis=1)

    @pl.when(t < nchunks)
    def _():
        step(sf_ref, 0)

    @pl.when(t >= nchunks)
    def _():
        step(sb_ref, w)


def _ssd_out_body(xbc_ref, dt_ref, z_ref, dtb_ref, a_ref, tri_ref, e_ref, dskip_ref, nw_ref, sf_ref, sb_ref, o_ref):
    n = xbc_ref.shape[1]
    w = GROUP_WIDTH
    dt, la, cs = _ssd_decays(dt_ref, dtb_ref, a_ref, tri_ref)
    ecs = cs - la
    fwd_lane = lax.broadcasted_iota(jnp.int32, (1, LANES), 1) < SSD_HEADS
    carry = jnp.exp(jnp.where(fwd_lane, cs, cs[n - 1:n, :] - ecs))
    both = _expand_heads(jnp.concatenate([dt, carry], axis=0), e_ref)
    dt_e = both[:n]
    carry_e = both[n:]
    xbc = xbc_ref[0]
    xs = xbc[:, :w]
    xdt_f = xs * dt_e[:, :w]
    xdt_b = xs * dt_e[:, w:]
    cs_t = cs.T
    ecs_t = ecs.T
    ii = lax.broadcasted_iota(jnp.int32, (n, n), 0)
    jj = lax.broadcasted_iota(jnp.int32, (n, n), 1)
    low = lax.broadcasted_iota(jnp.int32, (1, LANES), 1) < SSD_HEADDIM
    neg = -1e30
    ydiag = []
    yoff = []
    for g in range(SSD_GROUPS):
        bg = xbc[:, w + g * SSD_STATE:w + (g + 1) * SSD_STATE].astype(BF16)
        cg = xbc[:, w + 256 + g * SSD_STATE:w + 256 + (g + 1) * SSD_STATE].astype(BF16)
        cb = _dot_nt(cg, bg)
        for pair in range(2):
            p = 2 * g + pair
            lhs = []
            for h in (2 * p, 2 * p + 1):
                lf = jnp.exp(jnp.where(ii >= jj, cs[:, h:h + 1] - cs_t[h:h + 1, :], neg))
                lhs.append((cb * lf).astype(BF16))
            for h in (2 * p, 2 * p + 1):
                hb = SSD_HEADS + h
                lb = jnp.exp(jnp.where(jj > ii, ecs_t[hb:hb + 1, :] - ecs[:, hb:hb + 1], neg))
                lhs.append((cb * lb).astype(BF16))
            xf = xdt_f[:, p * LANES:(p + 1) * LANES]
            xb = xdt_b[:, p * LANES:(p + 1) * LANES]
            rhs = jnp.concatenate([jnp.where(low, xf, 0.0), jnp.where(low, 0.0, xf),
                                   jnp.where(low, xb, 0.0), jnp.where(low, 0.0, xb)], axis=0).astype(BF16)
            ydiag.append(_dot(jnp.concatenate(lhs, axis=1), rhs))
        yoff.append(_dot(cg, sf_ref[0, 0][:, g * 256:(g + 1) * 256].astype(BF16)) * carry_e[:, g * 256:(g + 1) * 256]
                    + _dot(cg, sb_ref[0, 0][:, g * 256:(g + 1) * 256].astype(BF16))
                    * carry_e[:, w + g * 256:w + (g + 1) * 256])
    y = jnp.concatenate(ydiag, axis=1) + jnp.concatenate(yoff, axis=1) + xs * dskip_ref[...]
    o_ref[0] = _rms(y * _silu(z_ref[0]), nw_ref[...])


def _ssd(proj3, xbc_act, dt_bias, a_log, d_skip, norm_w):
    bsz, s, _ = proj3.shape
    n = SSD_CHUNK
    c = s // n
    tri, e = _ssd_tables(n)
    pad = jnp.zeros((LANES - 2 * SSD_HEADS,), F32)
    dtb = jnp.concatenate([dt_bias.reshape(-1), pad]).reshape(1, LANES)
    a = jnp.concatenate([-jnp.exp(a_log.reshape(-1)), pad]).reshape(1, LANES)
    dskip = jnp.repeat(d_skip, SSD_HEADDIM).reshape(1, GROUP_WIDTH)
    cdt = COL_DT // LANES
    const2 = lambda i, t: (0, 0)

    def chunk(t):
        return jnp.where(t < c, t, 2 * c - 1 - t)

    state_shape = jax.ShapeDtypeStruct((bsz, c, SSD_STATE, GROUP_WIDTH), F32)
    sf, sb = pl.pallas_call(
        functools.partial(_ssd_state_body, c),
        grid=(bsz, 2 * c),
        in_specs=[pl.BlockSpec((1, n, SSD_CONV_CH), lambda i, t: (i, chunk(t), 0)),
                  pl.BlockSpec((1, n, LANES), lambda i, t: (i, chunk(t), cdt)),
                  pl.BlockSpec((1, LANES), const2), pl.BlockSpec((1, LANES), const2),
                  pl.BlockSpec((n, n), const2), pl.BlockSpec((2 * LANES, 2 * GROUP_WIDTH), const2)],
        out_specs=[pl.BlockSpec((1, 1, SSD_STATE, GROUP_WIDTH), lambda i, t: (i, jnp.minimum(t, c - 1), 0, 0)),
                   pl.BlockSpec((1, 1, SSD_STATE, GROUP_WIDTH), lambda i, t: (i, jnp.minimum(2 * c - 1 - t, c - 1), 0, 0))],
        out_shape=[state_shape, state_shape],
        scratch_shapes=[pltpu.VMEM((SSD_STATE, GROUP_WIDTH), F32)],
        compiler_params=_cparams(("parallel", "arbitrary"), 32),
    )(xbc_act, proj3, dtb, a, tri, e)

    const2 = lambda i, j: (0, 0)
    return pl.pallas_call(
        _ssd_out_body,
        grid=(bsz, c),
        in_specs=[pl.BlockSpec((1, n, SSD_CONV_CH), lambda i, j: (i, j, 0)),
                  pl.BlockSpec((1, n, LANES), lambda i, j: (i, j, cdt)),
                  pl.BlockSpec((1, n, GROUP_WIDTH), lambda i, j: (i, j, COL_Z // GROUP_WIDTH)),
                  pl.BlockSpec((1, LANES), const2), pl.BlockSpec((1, LANES), const2),
                  pl.BlockSpec((n, n), const2), pl.BlockSpec((2 * LANES, 2 * GROUP_WIDTH), const2),
                  pl.BlockSpec((1, GROUP_WIDTH), const2), pl.BlockSpec((1, GROUP_WIDTH), const2),
                  pl.BlockSpec((1, 1, SSD_STATE, GROUP_WIDTH), lambda i, j: (i, j, 0, 0)),
                  pl.BlockSpec((1, 1, SSD_STATE, GROUP_WIDTH), lambda i, j: (i, j, 0, 0))],
        out_specs=pl.BlockSpec((1, n, GROUP_WIDTH), lambda i, j: (i, j, 0)),
        out_shape=jax.ShapeDtypeStruct((bsz, s, GROUP_WIDTH), F32),
        compiler_params=_cparams(("parallel", "parallel"), 32),
    )(xbc_act, proj3, proj3, dtb, a, tri, e, dskip, norm_w.reshape(1, -1), sf, sb)


def _hy_filter_body(z_ref, w1_ref, b1_ref, w2_ref, b2_ref, w3_ref, fr_ref, dl_ref, hs_ref, hd_ref):
    tl = z_ref.shape[0]
    z = z_ref[...]
    hid = jnp.sin(fr_ref[0:1, :] * (_dot_f32(z, w1_ref[...]) + b1_ref[...]))
    hid = jnp.sin(fr_ref[1:2, :] * (_dot_f32(hid, w2_ref[...]) + b2_ref[...]))
    filt = _dot_f32(hid, w3_ref[...])
    dec = jnp.exp(-z[:, 0:1] * dl_ref[...])
    row = pl.program_id(0) * tl + lax.broadcasted_iota(jnp.int32, (tl, 1), 0)
    for o in range(HY_ORDER):
        base = o * 2 * HY_WIDTH
        hf = filt[:, base:base + HY_WIDTH] * dec
        hb = jnp.where(row == 0, 0.0, filt[:, base + HY_WIDTH:base + 2 * HY_WIDTH] * dec)
        hs_ref[:, o * HY_WIDTH:(o + 1) * HY_WIDTH] = hf + hb
        hd_ref[:, o * HY_WIDTH:(o + 1) * HY_WIDTH] = hf - hb


def _hy_filters(l, w1, b1, w2, b2, w3, freq):
    z, deltas = _hyena_positions(l)
    hid = HY_FILTER_HIDDEN
    w1p = jnp.zeros((LANES, LANES), F32).at[:HY_EMB, :hid].set(w1)
    w2p = jnp.zeros((LANES, LANES), F32).at[:hid, :hid].set(w2)
    w3p = jnp.zeros((LANES, w3.shape[1]), F32).at[:hid].set(w3)
    b1p = jnp.zeros((1, LANES), F32).at[0, :hid].set(b1)
    b2p = jnp.zeros((1, LANES), F32).at[0, :hid].set(b2)
    frp = jnp.zeros((2, LANES), F32).at[:, :hid].set(freq)
    tl = min(256, l)
    ncol = HY_ORDER * HY_WIDTH
    const = lambda i: (0, 0)
    return pl.pallas_call(
        _hy_filter_body,
        grid=(l // tl,),
        in_specs=[pl.BlockSpec((tl, LANES), lambda i: (i, 0)),
                  pl.BlockSpec((LANES, LANES), const), pl.BlockSpec((1, LANES), const),
                  pl.BlockSpec((LANES, LANES), const), pl.BlockSpec((1, LANES), const),
                  pl.BlockSpec((LANES, w3.shape[1]), const), pl.BlockSpec((2, LANES), const),
                  pl.BlockSpec((1, HY_WIDTH), const)],
        out_specs=[pl.BlockSpec((tl, ncol), lambda i: (i, 0)), pl.BlockSpec((tl, ncol), lambda i: (i, 0))],
        out_shape=[jax.ShapeDtypeStruct((l, ncol), F32), jax.ShapeDtypeStruct((l, ncol), F32)],
        compiler_params=_cparams(("parallel",), 32),
    )(z, w1p, b1p, w2p, b2p, w3p, frp, deltas)


def _hy_conv_body(u0_ref, u1_ref, u2_ref, c_ref, s_ref, hr0_ref, hi0_ref, hr1_ref, hi1_ref, b0_ref, b1_ref, o_ref,
                  xb_scr, pre_scr, pim_scr, z_scr):
    l = u0_ref.shape[1]
    tf = min(HY_ROW_TILE, l)
    sgn = jnp.where(lax.broadcasted_iota(jnp.int32, (l, 1), 0) % 2 == 0, 1.0, -1.0)

    def long_conv(x, gate_ref, hr_ref, hi_ref, b_ref, out_ref):
        xb_scr[...] = x.astype(BF16)
        x_nyq = jnp.sum(x * sgn, axis=0, keepdims=True) * hr_ref[l:l + 1, :]
        for r in range(l // tf):
            rows = pl.ds(r * tf, tf)
            xc = _dot(c_ref[rows, :], xb_scr[...])
            xs = _dot(s_ref[rows, :], xb_scr[...])
            hre = hr_ref[rows, :]
            him = hi_ref[rows, :]
            pre_scr[rows, :] = (xc * hre + xs * him).astype(BF16)
            pim_scr[rows, :] = (xc * him - xs * hre).astype(BF16)
        for r in range(l // tf):
            rows = pl.ds(r * tf, tf)
            y = _dot(c_ref[rows, :], pre_scr[...]) - _dot(s_ref[rows, :], pim_scr[...])
            y = y + sgn[r * tf:(r + 1) * tf] * x_nyq
            out_ref[rows, :] = gate_ref[0, rows, :] * (y + x[r * tf:(r + 1) * tf] * b_ref[0])

    long_conv(u0_ref[0], u1_ref, hr0_ref, hi0_ref, b0_ref, z_scr)
    long_conv(z_scr[...], u2_ref, hr1_ref, hi1_ref, b1_ref, o_ref.at[0])


def _hy_long_conv(u, cmat, smat, hre, him, bias):
    bsz, l, _ = u.shape
    tc = 256
    nb = HY_WIDTH // tc
    const = lambda j, i: (0, 0)
    single = pl.Buffered(1)
    bias3 = bias.reshape(HY_ORDER, 1, HY_WIDTH)
    return pl.pallas_call(
        _hy_conv_body,
        grid=(nb, bsz),
        in_specs=[pl.BlockSpec((1, l, tc), lambda j, i: (i, 0, j)),
                  pl.BlockSpec((1, l, tc), lambda j, i: (i, 0, nb + j)),
                  pl.BlockSpec((1, l, tc), lambda j, i: (i, 0, 2 * nb + j)),
                  pl.BlockSpec((l, l), const, pipeline_mode=single),
                  pl.BlockSpec((l, l), const, pipeline_mode=single),
                  pl.BlockSpec((l + SUBLANES, tc), lambda j, i: (0, j), pipeline_mode=single),
                  pl.BlockSpec((l, tc), lambda j, i: (0, j), pipeline_mode=single),
                  pl.BlockSpec((l + SUBLANES, tc), lambda j, i: (0, nb + j), pipeline_mode=single),
                  pl.BlockSpec((l, tc), lambda j, i: (0, nb + j), pipeline_mode=single),
                  pl.BlockSpec((1, 1, tc), lambda j, i: (0, 0, j)),
                  pl.BlockSpec((1, 1, tc), lambda j, i: (1, 0, j))],
        out_specs=pl.BlockSpec((1, l, tc), lambda j, i: (i, 0, j)),
        out_shape=jax.ShapeDtypeStruct((bsz, l, HY_WIDTH), F32),
        scratch_shapes=[pltpu.VMEM((l, tc), BF16), pltpu.VMEM((l, tc), BF16), pltpu.VMEM((l, tc), BF16),
                        pltpu.VMEM((l, tc), F32)],
        compiler_params=_cparams(("parallel", "parallel"), 56),
    )(u, u, u, cmat, smat, hre, him, hre, him, bias3, bias3)


def _hyena(proj3, conv_w, conv_b, w1, b1, w2, b2, w3, freq, bias):
    l = proj3.shape[1]
    u = _dwconv(proj3, COL_HY, (HY_ORDER + 1) * HY_WIDTH, conv_w, conv_b, act=False)
    hs, hd = _hy_filters(l, w1, b1, w2, b2, w3, freq)
    cmat, smat, c_sc, s_sc_neg = _dft_tables(l)
    tn = 256
    hre = _matmul(c_sc, hs.astype(BF16), l + SUBLANES, tn)
    him = _matmul(s_sc_neg, hd.astype(BF16), l, tn)
    return _hy_long_conv(u, cmat, smat, hre, him, bias)


def _out_proj_body(a_ref, b_ref, c_ref, d_ref, x_ref, w_ref, dn_ref, g_ref, beta_ref, o_ref, *rows_ref):
    gw = GROUP_WIDTH
    h = _dot(a_ref[...].astype(BF16), w_ref[0:gw, :])
    h += _dot(b_ref[...].astype(BF16), w_ref[gw:2 * gw, :])
    h += _dot(c_ref[...].astype(BF16), w_ref[2 * gw:3 * gw, :])
    h += _dot(_rms(d_ref[...], dn_ref[...]).astype(BF16), w_ref[3 * gw:4 * gw, :])
    y = _layer_norm(ALPHA * x_ref[...] + h, g_ref[...], beta_ref[...])
    o_ref[...] = y
    if rows_ref:
        _store_row_tiles(rows_ref[0], y)


def _store_row_tiles(dst_ref, y):
    for c in range(ROW_CHUNKS):
        dst_ref[:, c, :] = y[:, c * LANES:(c + 1) * LANES]


def _out_proj_ln(a, b, c, d, x, w_out, hy_norm, g, beta, emit_row_tiles):
    m = x.shape[0]
    tm = min(512, m)
    gw = GROUP_WIDTH
    const = lambda i: (0, 0)
    row = lambda i: (i, 0)
    out_specs = [pl.BlockSpec((tm, D_MODEL), row)]
    out_shape = [jax.ShapeDtypeStruct((m, D_MODEL), F32)]
    if emit_row_tiles:
        out_specs.append(pl.BlockSpec((tm, ROW_CHUNKS, LANES), lambda i: (i, 0, 0)))
        out_shape.append(jax.ShapeDtypeStruct((m, ROW_CHUNKS, LANES), F32))
    return pl.pallas_call(
        _out_proj_body,
        grid=(m // tm,),
        in_specs=[pl.BlockSpec((tm, gw), row), pl.BlockSpec((tm, gw), row), pl.BlockSpec((tm, gw), row),
                  pl.BlockSpec((tm, gw), row), pl.BlockSpec((tm, D_MODEL), row),
                  pl.BlockSpec((D_MODEL, D_MODEL), const), pl.BlockSpec((1, gw), const),
                  pl.BlockSpec((1, D_MODEL), const), pl.BlockSpec((1, D_MODEL), const)],
        out_specs=out_specs,
        out_shape=out_shape,
        compiler_params=_cparams(("parallel",), 48),
    )(a, b, c, d, x, w_out, hy_norm.reshape(1, -1), g.reshape(1, -1), beta.reshape(1, -1))


def _ffn_body(x_ref, wg_ref, wu_ref, wd_ref, g_ref, beta_ref, o_ref, xb_ref, acc_ref):
    j = pl.program_id(1)

    @pl.when(j == 0)
    def _():
        xb_ref[...] = x_ref[...].astype(BF16)
        acc_ref[...] = jnp.zeros_like(acc_ref)

    xb = xb_ref[...]
    hidden = (_silu(_dot(xb, wg_ref[...])) * _dot(xb, wu_ref[...])).astype(BF16)
    acc_ref[...] += _dot(hidden, wd_ref[...])

    @pl.when(j == pl.num_programs(1) - 1)
    def _():
        o_ref[...] = _layer_norm(ALPHA * x_ref[...] + acc_ref[...], g_ref[...], beta_ref[...])


def _ffn_ln(x, wg, wu, wd, g, beta):
    m = x.shape[0]
    dff = wg.shape[1]
    tm = min(512, m)
    tf = 512
    const = lambda i, j: (0, 0)
    return pl.pallas_call(
        _ffn_body,
        grid=(m // tm, dff // tf),
        in_specs=[pl.BlockSpec((tm, D_MODEL), lambda i, j: (i, 0)),
                  pl.BlockSpec((D_MODEL, tf), lambda i, j: (0, j)),
                  pl.BlockSpec((D_MODEL, tf), lambda i, j: (0, j)),
                  pl.BlockSpec((tf, D_MODEL), lambda i, j: (j, 0)),
                  pl.BlockSpec((1, D_MODEL), const), pl.BlockSpec((1, D_MODEL), const)],
        out_specs=pl.BlockSpec((tm, D_MODEL), lambda i, j: (i, 0)),
        out_shape=jax.ShapeDtypeStruct((m, D_MODEL), F32),
        scratch_shapes=[pltpu.VMEM((tm, D_MODEL), BF16), pltpu.VMEM((tm, D_MODEL), F32)],
        compiler_params=_cparams(("parallel", "arbitrary"), 48),
    )(x, wg, wu, wd, g.reshape(1, -1), beta.reshape(1, -1))


def _router_body(x_ref, w_ref, idx_ref, gate_ref):
    logits = _dot_f32(x_ref[...], w_ref[...])
    lane = lax.broadcasted_iota(jnp.int32, logits.shape, 1)
    neg = -jnp.inf
    logits = jnp.where(lane < N_EXPERTS, logits, neg)
    m1 = jnp.max(logits, axis=-1, keepdims=True)
    i1 = jnp.min(jnp.where(logits == m1, lane, LANES), axis=-1, keepdims=True)
    rest = jnp.where(lane == i1, neg, logits)
    m2 = jnp.max(rest, axis=-1, keepdims=True)
    i2 = jnp.min(jnp.where(rest == m2, lane, LANES), axis=-1, keepdims=True)
    e2 = jnp.exp(m2 - m1)
    den = 1.0 + e2
    idx_ref[...] = jnp.where(lane == 0, i1, jnp.where(lane == 1, i2, 0))
    gate_ref[...] = jnp.where(lane == 0, 1.0 / den, jnp.where(lane == 1, e2 / den, 0.0))


def _router(x, w_router):
    m = x.shape[0]
    tm = min(512, m)
    wp = jnp.zeros((D_MODEL, LANES), F32).at[:, :N_EXPERTS].set(w_router)
    return pl.pallas_call(
        _router_body,
        grid=(m // tm,),
        in_specs=[pl.BlockSpec((tm, D_MODEL), lambda i: (i, 0)), pl.BlockSpec((D_MODEL, LANES), lambda i: (0, 0))],
        out_specs=[pl.BlockSpec((tm, LANES), lambda i: (i, 0)), pl.BlockSpec((tm, LANES), lambda i: (i, 0))],
        out_shape=[jax.ShapeDtypeStruct((m, LANES), jnp.int32), jax.ShapeDtypeStruct((m, LANES), F32)],
        compiler_params=_cparams(("parallel",), 32),
    )(x, wp)


def _row_copy(src_ref, src_row, buf_ref, buf_row, sem):
    return pltpu.make_async_copy(src_ref.at[src_row], buf_ref.at[buf_row], sem)


def _gather_row_tiles(idx_ref, n, src_ref, buf_ref, sem):
    def start(r, carry):
        _row_copy(src_ref, idx_ref[0, 0, r], buf_ref, r, sem).start()
        return carry

    def wait(r, carry):
        _row_copy(src_ref, 0, buf_ref, r, sem).wait()
        return carry

    lax.fori_loop(0, n, start, 0)
    lax.fori_loop(0, n, wait, 0)


def _gather_x_body(idx_ref, src_ref, o_ref, buf, sem):
    _gather_row_tiles(idx_ref, buf.shape[0], src_ref, buf, sem)
    for c in range(ROW_CHUNKS):
        o_ref[:, c * LANES:(c + 1) * LANES] = buf[:, c, :].astype(BF16)


def _gather_x(x_tiles, slot_tok):
    cap = slot_tok.shape[0]
    rows = GATHER_ROWS
    assert cap % rows == 0
    return pl.pallas_call(
        _gather_x_body,
        grid=(cap // rows,),
        in_specs=[pl.BlockSpec((1, 1, rows), lambda i: (i, 0, 0), memory_space=pltpu.SMEM),
                  pl.BlockSpec(memory_space=pl.ANY)],
        out_specs=pl.BlockSpec((rows, D_MODEL), lambda i: (i, 0)),
        out_shape=jax.ShapeDtypeStruct((cap, D_MODEL), BF16),
        scratch_shapes=[pltpu.VMEM((rows, ROW_CHUNKS, LANES), F32), pltpu.SemaphoreType.DMA(())],
        compiler_params=pltpu.CompilerParams(dimension_semantics=("arbitrary",), disable_bounds_checks=True),
    )(slot_tok.reshape(cap // rows, 1, rows), x_tiles)


def _expert_body(be_ref, nu_ref, x_ref, wg_ref, wu_ref, wd_ref, o_ref, acc_ref):
    i = pl.program_id(0)
    j = pl.program_id(1)

    @pl.when(j == 0)
    def _():
        acc_ref[...] = jnp.zeros_like(acc_ref)

    @pl.when(i < nu_ref[0])
    def _():
        xb = x_ref[...]
        hidden = (_silu(_dot(xb, wg_ref[0])) * _dot(xb, wu_ref[0])).astype(BF16)
        acc_ref[...] += _dot(hidden, wd_ref[0])

    @pl.when(j == pl.num_programs(1) - 1)
    def _():
        _store_row_tiles(o_ref, acc_ref[...])


def _expert_ffn(x_slots, block_expert, n_used, wg, wu, wd, tm):
    cap = x_slots.shape[0]
    dff = wg.shape[2]
    tf = 512

    def ff_tile(i, j, nu):
        return jnp.where(i < nu[0], j, 0)

    grid_spec = pltpu.PrefetchScalarGridSpec(
        num_scalar_prefetch=2,
        grid=(cap // tm, dff // tf),
        in_specs=[pl.BlockSpec((tm, D_MODEL), lambda i, j, be, nu: (i, 0)),
                  pl.BlockSpec((1, D_MODEL, tf), lambda i, j, be, nu: (be[i], 0, ff_tile(i, j, nu))),
                  pl.BlockSpec((1, D_MODEL, tf), lambda i, j, be, nu: (be[i], 0, ff_tile(i, j, nu))),
                  pl.BlockSpec((1, tf, D_MODEL), lambda i, j, be, nu: (be[i], ff_tile(i, j, nu), 0))],
        out_specs=pl.BlockSpec((tm, ROW_CHUNKS, LANES), lambda i, j, be, nu: (i, 0, 0)),
        scratch_shapes=[pltpu.VMEM((tm, D_MODEL), F32)],
    )
    return pl.pallas_call(
        _expert_body,
        grid_spec=grid_spec,
        out_shape=jax.ShapeDtypeStruct((cap, ROW_CHUNKS, LANES), F32),
        compiler_params=_cparams(("parallel", "arbitrary"), 56),
    )(block_expert, n_used, x_slots, wg, wu, wd)


def _combine_body(idx_ref, x_ref, gt_ref, g_ref, beta_ref, y_ref, o_ref, buf, f_scr, sem):
    rows = x_ref.shape[0]
    _gather_row_tiles(idx_ref, TOP_K * rows, y_ref, buf, sem)
    gt = gt_ref[...]
    g0 = gt[:, 0:1]
    g1 = gt[:, 1:2]
    for c in range(ROW_CHUNKS):
        f_scr[:, c * LANES:(c + 1) * LANES] = g0 * buf[0:rows, c, :] + g1 * buf[rows:2 * rows, c, :]
    o_ref[...] = _layer_norm(ALPHA * x_ref[...] + f_scr[...], g_ref[...], beta_ref[...])


def _combine_ln(x, y_tiles, pos, gates, g, beta):
    m = x.shape[0]
    rows = min(COMBINE_ROWS, m)
    nblk = m // rows
    idx = pos.reshape(nblk, rows, TOP_K).transpose(0, 2, 1).reshape(nblk, 1, TOP_K * rows)
    const = lambda i: (0, 0)
    return pl.pallas_call(
        _combine_body,
        grid=(nblk,),
        in_specs=[pl.BlockSpec((1, 1, TOP_K * rows), lambda i: (i, 0, 0), memory_space=pltpu.SMEM),
                  pl.BlockSpec((rows, D_MODEL), lambda i: (i, 0)),
                  pl.BlockSpec((rows, LANES), lambda i: (i, 0)),
                  pl.BlockSpec((1, D_MODEL), const), pl.BlockSpec((1, D_MODEL), const),
                  pl.BlockSpec(memory_space=pl.ANY)],
        out_specs=pl.BlockSpec((rows, D_MODEL), lambda i: (i, 0)),
        out_shape=jax.ShapeDtypeStruct((m, D_MODEL), F32),
        scratch_shapes=[pltpu.VMEM((TOP_K * rows, ROW_CHUNKS, LANES), F32), pltpu.VMEM((rows, D_MODEL), F32),
                        pltpu.SemaphoreType.DMA(())],
        compiler_params=pltpu.CompilerParams(dimension_semantics=("arbitrary",), disable_bounds_checks=True),
    )(idx, x, gates, g.reshape(1, -1), beta.reshape(1, -1), y_tiles)


def _moe_ln(x, x_tiles, w_router, wg, wu, wd, g, beta, tm):
    m = x.shape[0]
    n_asg = m * TOP_K
    idx, gates = _router(x, w_router)
    e_flat = idx[:, :TOP_K].reshape(-1)
    onehot = (e_flat[:, None] == jnp.arange(N_EXPERTS, dtype=jnp.int32)[None, :]).astype(jnp.int32)
    csum = jnp.cumsum(onehot, axis=0)
    counts = csum[-1]
    rank = jnp.sum(csum * onehot, axis=1) - 1
    padded = (counts + tm - 1) // tm * tm
    pend = jnp.cumsum(padded)
    pstart = pend - padded
    dest = (jnp.sum(pstart[None, :] * onehot, axis=1) + rank).astype(jnp.int32)
    cap = n_asg + N_EXPERTS * tm
    tok = jnp.arange(n_asg, dtype=jnp.int32) // TOP_K
    slot_tok = jnp.zeros((cap,), jnp.int32).at[dest].set(tok)
    nblk = cap // tm
    block_expert = jnp.minimum(jnp.searchsorted(pend, jnp.arange(nblk, dtype=pend.dtype) * tm, side='right'),
                               N_EXPERTS - 1).astype(jnp.int32)
    n_used = (pend[-1] // tm).astype(jnp.int32).reshape(1)

    x_slots = _gather_x(x_tiles, slot_tok)
    y_tiles = _expert_ffn(x_slots, block_expert, n_used, wg, wu, wd, tm)
    return _combine_ln(x, y_tiles, dest.reshape(m, TOP_K), gates, g, beta)


def _mixer_ln(x2, bsz, s, p, emit_row_tiles):
    proj = _matmul(x2, p['w_in'], min(1024, x2.shape[0]), 896)
    proj3 = proj.reshape(bsz, s, PROJ_COLS)
    cos_slab, sin_slab, cos4, sin4 = _rope_tables(s)
    out_a = _mla(proj3, cos_slab, sin_slab, p['mla_q_norm'], p['w_q'], p['w_qr'], p['mla_kv_norm'], p['w_kv'],
                 p['mla_out_norm'])
    out_b = _retention(proj3, cos4, sin4)
    xbc_act = _dwconv(proj3, COL_XBC, SSD_CONV_CH, p['ssd_conv_w'], p['ssd_conv_b'], act=True)
    out_c = _ssd(proj3, xbc_act, p['ssd_dt_bias'], p['ssd_a_log'], p['ssd_d'], p['ssd_norm'])
    out_d = _hyena(proj3, p['hy_conv_w'], p['hy_conv_b'], p['hy_w1'], p['hy_b1'], p['hy_w2'], p['hy_b2'], p['hy_w3'],
                   p['hy_freq'], p['hy_bias'])
    m = bsz * s
    gw = GROUP_WIDTH
    return _out_proj_ln(out_a.reshape(m, gw), out_b.reshape(m, gw), out_c.reshape(m, gw), out_d.reshape(m, gw),
                        x2, p['w_out'], p['hy_out_norm'], p['ln1_g'], p['ln1_b'], emit_row_tiles)


_MIXER_KEYS = ('mla_q_norm', 'mla_kv_norm', 'mla_out_norm', 'ssd_conv_w', 'ssd_conv_b', 'ssd_dt_bias', 'ssd_a_log',
               'ssd_d', 'ssd_norm', 'hy_conv_w', 'hy_conv_b', 'hy_w1', 'hy_b1', 'hy_w2', 'hy_b2', 'hy_w3', 'hy_freq',
               'hy_bias', 'hy_out_norm', 'ln1_g', 'ln1_b')

MOE_TM = 1024


def kernel(x, w_in, mla_q_norm, mla_w_uq, mla_kv_norm, mla_w_ukv, mla_out_norm, ssd_conv_w, ssd_conv_b, ssd_dt_bias, ssd_a_log, ssd_d, ssd_norm, hy_conv_w, hy_conv_b, hy_w1, hy_b1, hy_w2, hy_b2, hy_w3, hy_freq, hy_bias, hy_out_norm, w_out, ln1_g, ln1_b, ln2_g, ln2_b, ffn_w_gate, ffn_w_up, ffn_w_down, moe_router, moe_w_gate, moe_w_up, moe_w_down):
    args = dict(locals())
    bsz, s, d = x.shape
    x2 = x.reshape(bsz * s, d)
    for layer in range(DEPTH):
        p = {k: args[k][layer] for k in _MIXER_KEYS}
        p['w_in'] = _prep_w_in(w_in[layer])
        p['w_q'], p['w_qr'] = _prep_w_uq(mla_w_uq[layer])
        p['w_kv'] = _prep_w_ukv(mla_w_ukv[layer])
        p['w_out'] = w_out[layer].astype(BF16)
        j = layer // 2
        if layer % 2 == 0:
            x2, = _mixer_ln(x2, bsz, s, p, False)
            x2 = _ffn_ln(x2, ffn_w_gate[j].astype(BF16), ffn_w_up[j].astype(BF16), ffn_w_down[j].astype(BF16),
                         ln2_g[layer], ln2_b[layer])
        else:
            x2, x_tiles = _mixer_ln(x2, bsz, s, p, True)
            x2 = _moe_ln(x2, x_tiles, moe_router[j], moe_w_gate[j].astype(BF16), moe_w_up[j].astype(BF16),
                         moe_w_down[j].astype(BF16), ln2_g[layer], ln2_b[layer], MOE_TM)
    return x2.reshape(bsz, s, d)
```

```python
import functools
import math

import numpy as np
import jax
import jax.numpy as jnp
from jax import lax
from jax.experimental import pallas as pl
from jax.experimental.pallas import tpu as pltpu

F32 = jnp.float32
BF16 = jnp.bfloat16
HIGHEST = lax.Precision.HIGHEST

D_MODEL = 2048
DEPTH = 2
GROUP_WIDTH = 512
MLA_HEADS = 4
MLA_NOPE = 128
MLA_ROPE = 64
MLA_V = 128
MLA_Q_LORA = 384
MLA_KV_LORA = 256
RET_HEADS = 4
RET_DV = 128
RET_DK = 64
RET_DECAY_EXP_FWD = 5.0
RET_DECAY_EXP_BWD = 5.5
SSD_HEADDIM = 64
SSD_HEADS = 8
SSD_GROUPS = 2
SSD_STATE = 128
SSD_CONV = 5
SSD_CONV_CH = 1024
HY_ORDER = 2
HY_WIDTH = 512
HY_SHORT = 3
HY_EMB = 33
HY_FILTER_HIDDEN = 64
HY_MIN_DECAY = math.log(1e-2) / 1.5
HY_MAX_DECAY = math.log(1e-2) / 0.3
N_EXPERTS = 8
TOP_K = 2
ROPE_BASE = 10000.0
ALPHA = (2 * DEPTH) ** 0.25

V7X_VMEM_BYTES = 64 * 1024 * 1024
LANES = 128
SUBLANES = 8

PROJ_COLS = 6272
COL_HY = 0
COL_QC = 1536
COL_RQ = 2048
COL_KVPE = 4096
COL_Z = 4608
COL_XBC = 5120
COL_DT = 6144

RET_CHUNK = 256
SSD_CHUNK = 128
HY_ROW_TILE = 512
ROW_CHUNKS = D_MODEL // LANES
GATHER_ROWS = 512
COMBINE_ROWS = 256
DMA_LOOP_UNROLL = 8


def _cparams(semantics, vmem_mb):
    assert vmem_mb * 1024 * 1024 < V7X_VMEM_BYTES
    return pltpu.CompilerParams(dimension_semantics=semantics, vmem_limit_bytes=vmem_mb * 1024 * 1024)


def _sigmoid(x):
    return 1.0 / (1.0 + jnp.exp(-x))


def _silu(x):
    return x * _sigmoid(x)


def _rms(x, w, eps=1e-6):
    return x * lax.rsqrt(jnp.mean(x * x, axis=-1, keepdims=True) + eps) * w


def _layer_norm(y, g, b, eps=1e-5):
    mu = jnp.mean(y, axis=-1, keepdims=True)
    d = y - mu
    var = jnp.mean(d * d, axis=-1, keepdims=True)
    return d * lax.rsqrt(var + eps) * g + b


def _dot(a, b):
    return jnp.dot(a, b, preferred_element_type=F32)


def _dot_nt(a, b):
    return lax.dot_general(a, b, (((1,), (1,)), ((), ())), preferred_element_type=F32)


def _dot_tn(a, b):
    return lax.dot_general(a, b, (((0,), (0,)), ((), ())), preferred_element_type=F32)


def _dot_f32(a, b):
    return jnp.dot(a, b, preferred_element_type=F32, precision=HIGHEST)


def _mm_body(x_ref, w_ref, o_ref, xb_ref):
    @pl.when(pl.program_id(1) == 0)
    def _():
        xb_ref[...] = x_ref[...].astype(BF16)

    o_ref[...] = _dot(xb_ref[...], w_ref[...])


def _matmul(x, w, tm, tn, vmem_mb=48):
    m, k = x.shape
    n = w.shape[1]
    assert m % tm == 0 and n % tn == 0
    return pl.pallas_call(
        _mm_body,
        grid=(m // tm, n // tn),
        in_specs=[pl.BlockSpec((tm, k), lambda i, j: (i, 0)),
                  pl.BlockSpec((k, tn), lambda i, j: (0, j))],
        out_specs=pl.BlockSpec((tm, tn), lambda i, j: (i, j)),
        out_shape=jax.ShapeDtypeStruct((m, n), F32),
        scratch_shapes=[pltpu.VMEM((tm, k), BF16)],
        compiler_params=_cparams(("parallel", "arbitrary"), vmem_mb),
    )(x, w)


def _rot_half_cols(w, heads):
    k = w.shape[0]
    w = w.reshape(k, heads, 2, 32)
    return jnp.stack([-w[:, :, 1], w[:, :, 0]], axis=2).reshape(k, heads * 64)


def _prep_w_in(w):
    k = w.shape[0]

    def sl(a, b):
        return w[:, a:b]

    def zc(n):
        return jnp.zeros((k, n), w.dtype)

    q_c, kv_c, k_pe = sl(0, 384), sl(384, 640), sl(640, 704)
    r_q, r_k, r_v, r_g = sl(704, 960), sl(960, 1216), sl(1216, 1728), sl(1728, 2240)
    m_z, m_xbc, m_dt, h_u = sl(2240, 2752), sl(2752, 3776), sl(3776, 3792), sl(3792, 5328)
    cols = [h_u, q_c, zc(128),
            r_q, r_k, _rot_half_cols(r_q, RET_HEADS), _rot_half_cols(r_k, RET_HEADS), r_v, r_g,
            kv_c, k_pe, zc(64), _rot_half_cols(k_pe, 1), zc(64),
            m_z, m_xbc, m_dt, zc(112)]
    out = jnp.concatenate(cols, axis=1).astype(BF16)
    assert out.shape[1] == PROJ_COLS
    return out


def _prep_w_uq(w):
    k = w.shape[0]
    w = w.reshape(k, MLA_HEADS, MLA_NOPE + MLA_ROPE)
    nope, rope = w[:, :, :MLA_NOPE], w[:, :, MLA_NOPE:]
    z = jnp.zeros((k, MLA_HEADS, 64), w.dtype)
    main = jnp.concatenate([nope, rope, z], axis=2).reshape(k, MLA_HEADS * 256)
    rr = _rot_half_cols(rope.reshape(k, MLA_HEADS * 64), MLA_HEADS).reshape(k, MLA_HEADS, 64)
    rot = jnp.concatenate([rr, z], axis=2).reshape(k, MLA_HEADS * 128)
    return main.astype(BF16), rot.astype(BF16)


def _prep_w_ukv(w):
    k = w.shape[0]
    w = w.reshape(k, MLA_HEADS, MLA_NOPE + MLA_V)
    return jnp.concatenate([w[:, :, :MLA_NOPE].reshape(k, -1), w[:, :, MLA_NOPE:].reshape(k, -1)], axis=1).astype(BF16)


def _rope_tables(s):
    half = 32
    inv_freq = ROPE_BASE ** (-jnp.arange(half, dtype=F32) * 2.0 / 64)
    ang = jnp.arange(s, dtype=F32)[:, None] * inv_freq[None, :]
    cos, sin = jnp.cos(ang), jnp.sin(ang)
    cos64 = jnp.concatenate([cos, cos], axis=1)
    sin64 = jnp.concatenate([sin, sin], axis=1)
    z = jnp.zeros((s, 64), F32)
    return (jnp.concatenate([cos64, z], axis=1), jnp.concatenate([sin64, z], axis=1),
            jnp.tile(cos64, (1, 4)), jnp.tile(sin64, (1, 4)))


def _ret_tables(n):
    heads = jnp.arange(RET_HEADS, dtype=F32)
    lgf = jnp.log1p(-jnp.exp2(-RET_DECAY_EXP_FWD - heads))
    lgb = jnp.log1p(-jnp.exp2(-RET_DECAY_EXP_BWD - heads))
    idx = jnp.arange(n, dtype=F32)

    def lanes(tab, width):
        return jnp.repeat(tab, width, axis=1)

    kdf = lanes(jnp.exp((n - 1.0 - idx)[:, None] * lgf), RET_DK)
    kdb = lanes(jnp.exp(idx[:, None] * lgb), RET_DK)
    qdf = lanes(jnp.exp((idx + 1.0)[:, None] * lgf), RET_DK)
    qdb = lanes(jnp.exp((n - idx)[:, None] * lgb), RET_DK)
    cdf = lanes(jnp.exp(n * lgf)[None, :], RET_DV)
    cdb = lanes(jnp.exp(n * lgb)[None, :], RET_DV)
    diff = idx[:, None] - idx[None, :]
    dec = jnp.where(diff[None] >= 0,
                    jnp.exp(jnp.maximum(diff, 0.0)[None] * lgf[:, None, None]),
                    jnp.exp(jnp.maximum(-diff, 0.0)[None] * lgb[:, None, None]))
    bd = (jnp.arange(RET_HEADS * RET_DK)[:, None] // RET_DK == jnp.arange(RET_HEADS * RET_DV)[None, :] // RET_DV)
    return kdf, kdb, qdf, qdb, cdf, cdb, dec.astype(F32), bd.astype(F32)


def _ssd_tables(n):
    tri = (jnp.arange(n)[:, None] >= jnp.arange(n)[None, :]).astype(F32)
    lane_head = jnp.arange(GROUP_WIDTH) // SSD_HEADDIM
    r = jnp.arange(LANES)
    ef = (r[:, None] == lane_head[None, :]).astype(BF16)
    eb = (r[:, None] == lane_head[None, :] + SSD_HEADS).astype(BF16)
    e = jnp.concatenate([ef, eb], axis=1)
    return tri, jnp.concatenate([e, e], axis=0)


def _dft_tables(l):
    f = jnp.arange(l, dtype=jnp.int32)
    k = (f[:, None] * f[None, :]) % (2 * l)
    ang = k.astype(F32) * (math.pi / l)
    c, s = jnp.cos(ang), jnp.sin(ang)
    sc = jnp.where(f == 0, 1.0, 2.0).astype(F32)[:, None] / (2.0 * l)
    nyq = jnp.where(f % 2 == 0, 1.0, -1.0).astype(F32)[None, :] / (2.0 * l)
    c_sc = jnp.concatenate([c * sc, nyq, jnp.zeros((7, l), F32)], axis=0)
    return c.astype(BF16), s.astype(BF16), c_sc.astype(BF16), (-s * sc).astype(BF16)


def _hyena_positions(l):
    t = jnp.linspace(0.0, 1.0, l, dtype=F32)[:, None]
    bands = (HY_EMB - 1) // 2
    ang = 2.0 * math.pi * jnp.arange(l, dtype=F32)[:, None] / l
    f = jnp.linspace(1e-4, bands - 1, bands, dtype=F32)[None, :]
    z = jnp.concatenate([t, jnp.cos(f * ang), -jnp.sin(f * ang), jnp.zeros((l, LANES - HY_EMB), F32)], axis=-1)
    deltas = jnp.abs(jnp.linspace(HY_MIN_DECAY, HY_MAX_DECAY, HY_WIDTH, dtype=F32))[None, :]
    return z, deltas


def _conv_body(taps, act, x_ref, p_ref, n_ref, w_ref, b_ref, o_ref):
    r = pl.program_id(1)
    tr = x_ref.shape[1]
    half = taps // 2
    prev = jnp.where(r > 0, p_ref[0], 0.0)
    nxt = jnp.where(r < pl.num_programs(1) - 1, n_ref[0], 0.0)
    win = jnp.concatenate([prev, x_ref[0], nxt], axis=0)
    rows = tr + 2 * SUBLANES
    acc = jnp.broadcast_to(b_ref[...], (tr, x_ref.shape[2]))
    for t in range(taps):
        sh = (half - t) % rows
        shifted = win if sh == 0 else pltpu.roll(win, sh, axis=0)
        acc = acc + shifted[SUBLANES:SUBLANES + tr] * w_ref[t:t + 1, :]
    if act:
        acc = _silu(acc)
    o_ref[0] = acc


def _dwconv(proj3, col0, width, w, b, act):
    bsz, s, _ = proj3.shape
    taps = w.shape[0]
    cb = 512
    tr = min(512, s)
    off = col0 // cb
    nb8 = s // SUBLANES
    t8 = tr // SUBLANES
    return pl.pallas_call(
        functools.partial(_conv_body, taps, act),
        grid=(bsz, s // tr, width // cb),
        in_specs=[pl.BlockSpec((1, tr, cb), lambda i, r, c: (i, r, c + off)),
                  pl.BlockSpec((1, SUBLANES, cb), lambda i, r, c: (i, jnp.maximum(r * t8 - 1, 0), c + off)),
                  pl.BlockSpec((1, SUBLANES, cb), lambda i, r, c: (i, jnp.minimum((r + 1) * t8, nb8 - 1), c + off)),
                  pl.BlockSpec((taps, cb), lambda i, r, c: (0, c)),
                  pl.BlockSpec((1, cb), lambda i, r, c: (0, c))],
        out_specs=pl.BlockSpec((1, tr, cb), lambda i, r, c: (i, r, c)),
        out_shape=jax.ShapeDtypeStruct((bsz, s, width), F32),
        compiler_params=_cparams(("parallel", "parallel", "parallel"), 32),
    )(proj3, proj3, proj3, w, b.reshape(1, width))


def _mla_body(qc_ref, kvpe_ref, cq_ref, sq_ref, ck_ref, sk_ref, qn_ref, wq_ref, wqr_ref, kvn_ref, wkv_ref, on_ref,
              o_ref, k_scr, v_scr):
    @pl.when(pl.program_id(1) == 0)
    def _():
        kvpe = kvpe_ref[0]
        kvn = _rms(kvpe[:, :MLA_KV_LORA], kvn_ref[...]).astype(BF16)
        kpe = (kvpe[:, 256:384] * ck_ref[...] + kvpe[:, 384:512] * sk_ref[...]).astype(BF16)
        for h in range(MLA_HEADS):
            kn = _dot(kvn, wkv_ref[:, h * 128:(h + 1) * 128]).astype(BF16)
            k_scr[h] = jnp.concatenate([kn, kpe], axis=1)
            v_scr[h] = _dot(kvn, wkv_ref[:, 512 + h * 128:512 + (h + 1) * 128]).astype(BF16)

    scale = (MLA_NOPE + MLA_ROPE) ** -0.5
    qn = _rms(qc_ref[0][:, :MLA_Q_LORA], qn_ref[...]).astype(BF16)
    outs = []
    for h in range(MLA_HEADS):
        qm = _dot(qn, wq_ref[:, h * 256:(h + 1) * 256])
        qr = _dot(qn, wqr_ref[:, h * 128:(h + 1) * 128])
        qpe = qm[:, 128:] * cq_ref[...] + qr * sq_ref[...]
        qh = (jnp.concatenate([qm[:, :128], qpe], axis=1) * scale).astype(BF16)
        sc = _dot_nt(qh, k_scr[h])
        p = jnp.exp(sc - jnp.max(sc, axis=-1, keepdims=True))
        den = jnp.sum(p, axis=-1, keepdims=True)
        outs.append(_dot(p.astype(BF16), v_scr[h]) / den)
    o_ref[0] = _rms(jnp.concatenate(outs, axis=1), on_ref[...])


def _mla(proj3, cos_slab, sin_slab, q_norm, w_q, w_qr, kv_norm, w_kv, out_norm):
    bsz, s, _ = proj3.shape
    tq = min(512, s)
    const = lambda i, j: (0, 0)
    return pl.pallas_call(
        _mla_body,
        grid=(bsz, s // tq),
        in_specs=[pl.BlockSpec((1, tq, 512), lambda i, j: (i, j, COL_QC // 512)),
                  pl.BlockSpec((1, s, 512), lambda i, j: (i, 0, COL_KVPE // 512)),
                  pl.BlockSpec((tq, 128), lambda i, j: (j, 0)),
                  pl.BlockSpec((tq, 128), lambda i, j: (j, 0)),
                  pl.BlockSpec((s, 128), const),
                  pl.BlockSpec((s, 128), const),
                  pl.BlockSpec((1, MLA_Q_LORA), const),
                  pl.BlockSpec(w_q.shape, const),
                  pl.BlockSpec(w_qr.shape, const),
                  pl.BlockSpec((1, MLA_KV_LORA), const),
                  pl.BlockSpec(w_kv.shape, const),
                  pl.BlockSpec((1, GROUP_WIDTH), const)],
        out_specs=pl.BlockSpec((1, tq, GROUP_WIDTH), lambda i, j: (i, j, 0)),
        out_shape=jax.ShapeDtypeStruct((bsz, s, GROUP_WIDTH), F32),
        scratch_shapes=[pltpu.VMEM((MLA_HEADS, s, 256), BF16), pltpu.VMEM((MLA_HEADS, s, MLA_V), BF16)],
        compiler_params=_cparams(("parallel", "arbitrary"), 48),
    )(proj3, proj3, cos_slab, sin_slab, cos_slab, sin_slab, q_norm.reshape(1, -1), w_q, w_qr,
      kv_norm.reshape(1, -1), w_kv, out_norm.reshape(1, -1))


def _ret_state_body(nchunks, rk_ref, rkr_ref, rv_ref, cos_ref, sin_ref, kdf_ref, kdb_ref, cdf_ref, cdb_ref, bd_ref,
                    sf_ref, sb_ref, st_scr):
    t = pl.program_id(1)
    k = (rk_ref[0] * cos_ref[...] + rkr_ref[0] * sin_ref[...]) * (RET_DK ** -0.5)
    v = rv_ref[0].astype(BF16)

    @pl.when((t == 0) | (t == nchunks))
    def _():
        st_scr[...] = jnp.zeros_like(st_scr)

    def step(out_ref, kd_ref, cd_ref):
        out_ref[0, 0] = st_scr[...]
        new = _dot_tn((k * kd_ref[...]).astype(BF16), v)
        st_scr[...] = st_scr[...] * cd_ref[...] + new * bd_ref[...]

    @pl.when(t < nchunks)
    def _():
        step(sf_ref, kdf_ref, cdf_ref)

    @pl.when(t >= nchunks)
    def _():
        step(sb_ref, kdb_ref, cdb_ref)


def _ret_out_body(rq_ref, rk_ref, rqr_ref, rkr_ref, rv_ref, rg_ref, cos_ref, sin_ref, qdf_ref, qdb_ref, dec_ref,
                  sf_ref, sb_ref, o_ref):
    cos, sin = cos_ref[...], sin_ref[...]
    q = rq_ref[0] * cos + rqr_ref[0] * sin
    kb = ((rk_ref[0] * cos + rkr_ref[0] * sin) * (RET_DK ** -0.5)).astype(BF16)
    vb = rv_ref[0].astype(BF16)
    cross = (_dot((q * qdf_ref[...]).astype(BF16), sf_ref[0, 0].astype(BF16))
             + _dot((q * qdb_ref[...]).astype(BF16), sb_ref[0, 0].astype(BF16)))
    lane_head = lax.broadcasted_iota(jnp.int32, (1, RET_HEADS * RET_DK), 1) // RET_DK
    outs = []
    for h in range(RET_HEADS):
        qh = jnp.where(lane_head == h, q, 0.0).astype(BF16)
        sc = _dot_nt(qh, kb) * dec_ref[h]
        y = _dot(sc.astype(BF16), vb[:, h * RET_DV:(h + 1) * RET_DV]) + cross[:, h * RET_DV:(h + 1) * RET_DV]
        mu = jnp.mean(y, axis=-1, keepdims=True)
        d = y - mu
        var = jnp.mean(d * d, axis=-1, keepdims=True)
        outs.append(d * lax.rsqrt(var + 1e-6))
    o_ref[0] = jnp.concatenate(outs, axis=1) * _silu(rg_ref[0])


def _retention(proj3, cos4, sin4):
    bsz, s, _ = proj3.shape
    n = min(RET_CHUNK, s)
    c = s // n
    kdf, kdb, qdf, qdb, cdf, cdb, dec, bd = _ret_tables(n)
    c256 = COL_RQ // 256
    c512 = COL_RQ // 512
    const2 = lambda i, t: (0, 0)

    def chunk(t):
        return jnp.where(t < c, t, 2 * c - 1 - t)

    state_shape = jax.ShapeDtypeStruct((bsz, c, RET_HEADS * RET_DK, GROUP_WIDTH), F32)
    sf, sb = pl.pallas_call(
        functools.partial(_ret_state_body, c),
        grid=(bsz, 2 * c),
        in_specs=[pl.BlockSpec((1, n, 256), lambda i, t: (i, chunk(t), c256 + 1)),
                  pl.BlockSpec((1, n, 256), lambda i, t: (i, chunk(t), c256 + 3)),
                  pl.BlockSpec((1, n, 512), lambda i, t: (i, chunk(t), c512 + 2)),
                  pl.BlockSpec((n, 256), lambda i, t: (chunk(t), 0)),
                  pl.BlockSpec((n, 256), lambda i, t: (chunk(t), 0)),
                  pl.BlockSpec((n, 256), const2), pl.BlockSpec((n, 256), const2),
                  pl.BlockSpec((1, 512), const2), pl.BlockSpec((1, 512), const2),
                  pl.BlockSpec((256, 512), const2)],
        out_specs=[pl.BlockSpec((1, 1, 256, 512), lambda i, t: (i, jnp.minimum(t, c - 1), 0, 0)),
                   pl.BlockSpec((1, 1, 256, 512), lambda i, t: (i, jnp.minimum(2 * c - 1 - t, c - 1), 0, 0))],
        out_shape=[state_shape, state_shape],
        scratch_shapes=[pltpu.VMEM((RET_HEADS * RET_DK, GROUP_WIDTH), F32)],
        compiler_params=_cparams(("parallel", "arbitrary"), 32),
    )(proj3, proj3, proj3, cos4, sin4, kdf, kdb, cdf, cdb, bd)

    const3 = lambda i, j: (0, 0, 0)
    const2 = lambda i, j: (0, 0)
    return pl.pallas_call(
        _ret_out_body,
        grid=(bsz, c),
        in_specs=[pl.BlockSpec((1, n, 256), lambda i, j: (i, j, c256)),
                  pl.BlockSpec((1, n, 256), lambda i, j: (i, j, c256 + 1)),
                  pl.BlockSpec((1, n, 256), lambda i, j: (i, j, c256 + 2)),
                  pl.BlockSpec((1, n, 256), lambda i, j: (i, j, c256 + 3)),
                  pl.BlockSpec((1, n, 512), lambda i, j: (i, j, c512 + 2)),
                  pl.BlockSpec((1, n, 512), lambda i, j: (i, j, c512 + 3)),
                  pl.BlockSpec((n, 256), lambda i, j: (j, 0)),
                  pl.BlockSpec((n, 256), lambda i, j: (j, 0)),
                  pl.BlockSpec((n, 256), const2), pl.BlockSpec((n, 256), const2),
                  pl.BlockSpec((RET_HEADS, n, n), const3),
                  pl.BlockSpec((1, 1, 256, 512), lambda i, j: (i, j, 0, 0)),
                  pl.BlockSpec((1, 1, 256, 512), lambda i, j: (i, j, 0, 0))],
        out_specs=pl.BlockSpec((1, n, GROUP_WIDTH), lambda i, j: (i, j, 0)),
        out_shape=jax.ShapeDtypeStruct((bsz, s, GROUP_WIDTH), F32),
        compiler_params=_cparams(("parallel", "parallel"), 32),
    )(proj3, proj3, proj3, proj3, proj3, proj3, cos4, sin4, qdf, qdb, dec, sf, sb)


def _ssd_decays(dt_ref, dtb_ref, a_ref, tri_ref):
    dt_raw = dt_ref[0] + dtb_ref[...]
    dt = jnp.maximum(dt_raw, 0.0) + jnp.log1p(jnp.exp(-jnp.abs(dt_raw)))
    la = dt * a_ref[...]
    return dt, la, _dot_f32(tri_ref[...], la)


def _expand_heads(v, e_ref):
    hi = v.astype(BF16)
    lo = (v - hi.astype(F32)).astype(BF16)
    return _dot(jnp.concatenate([hi, lo], axis=1), e_ref[...])


def _ssd_state_body(nchunks, xbc_ref, dt_ref, dtb_ref, a_ref, tri_ref, e_ref, sf_ref, sb_ref, st_scr):
    t = pl.program_id(1)
    n = xbc_ref.shape[1]
    w = GROUP_WIDTH
    dt, la, cs = _ssd_decays(dt_ref, dtb_ref, a_ref, tri_ref)
    tot = cs[n - 1:n, :]
    fwd_lane = lax.broadcasted_iota(jnp.int32, (1, LANES), 1) < SSD_HEADS
    wgt = dt * jnp.exp(jnp.where(fwd_lane, tot - cs, cs - la))
    both = _expand_heads(jnp.concatenate([wgt, jnp.broadcast_to(jnp.exp(tot), (SUBLANES, LANES))], axis=0), e_ref)
    wgt_e = both[:n]
    dec_e = both[n:n + 1]
    xs = xbc_ref[0][:, :w]

    @pl.when((t == 0) | (t == nchunks))
    def _():
        st_scr[...] = jnp.zeros_like(st_scr)

    def step(out_ref, lo):
        out_ref[0, 0] = st_scr[...]
        xw = (xs * wgt_e[:, lo:lo + w]).astype(BF16)
        new = []
        for g in range(SSD_GROUPS):
            bg = xbc_ref[0][:, w + g * SSD_STATE:w + (g + 1) * SSD_STATE].astype(BF16)
            new.append(_dot_tn(bg, xw[:, g * 256:(g + 1) * 256]))
        st_scr[...] = st_scr[...] * dec_e[:, lo:lo + w] + jnp.concatenate(new, axis=1)

    @pl.when(t < nchunks)
    def _():
        step(sf_ref, 0)

    @pl.when(t >= nchunks)
    def _():
        step(sb_ref, w)


def _ssd_out_body(xbc_ref, dt_ref, z_ref, dtb_ref, a_ref, tri_ref, e_ref, dskip_ref, nw_ref, sf_ref, sb_ref, o_ref):
    n = xbc_ref.shape[1]
    w = GROUP_WIDTH
    dt, la, cs = _ssd_decays(dt_ref, dtb_ref, a_ref, tri_ref)
    ecs = cs - la
    fwd_lane = lax.broadcasted_iota(jnp.int32, (1, LANES), 1) < SSD_HEADS
    carry = jnp.exp(jnp.where(fwd_lane, cs, cs[n - 1:n, :] - ecs))
    both = _expand_heads(jnp.concatenate([dt, carry], axis=0), e_ref)
    dt_e = both[:n]
    carry_e = both[n:]
    xbc = xbc_ref[0]
    xs = xbc[:, :w]
    xdt_f = xs * dt_e[:, :w]
    xdt_b = xs * dt_e[:, w:]
    cs_t = cs.T
    ecs_t = ecs.T
    ii = lax.broadcasted_iota(jnp.int32, (n, n), 0)
    jj = lax.broadcasted_iota(jnp.int32, (n, n), 1)
    low = lax.broadcasted_iota(jnp.int32, (1, LANES), 1) < SSD_HEADDIM
    neg = -1e30
    ydiag = []
    yoff = []
    for g in range(SSD_GROUPS):
        bg = xbc[:, w + g * SSD_STATE:w + (g + 1) * SSD_STATE].astype(BF16)
        cg = xbc[:, w + 256 + g * SSD_STATE:w + 256 + (g + 1) * SSD_STATE].astype(BF16)
        cb = _dot_nt(cg, bg)
        for pair in range(2):
            p = 2 * g + pair
            lhs = []
            for h in (2 * p, 2 * p + 1):
                lf = jnp.exp(jnp.where(ii >= jj, cs[:, h:h + 1] - cs_t[h:h + 1, :], neg))
                lhs.append((cb * lf).astype(BF16))
            for h in (2 * p, 2 * p + 1):
                hb = SSD_HEADS + h
                lb = jnp.exp(jnp.where(jj > ii, ecs_t[hb:hb + 1, :] - ecs[:, hb:hb + 1], neg))
                lhs.append((cb * lb).astype(BF16))
            xf = xdt_f[:, p * LANES:(p + 1) * LANES]
            xb = xdt_b[:, p * LANES:(p + 1) * LANES]
            rhs = jnp.concatenate([jnp.where(low, xf, 0.0), jnp.where(low, 0.0, xf),
                                   jnp.where(low, xb, 0.0), jnp.where(low, 0.0, xb)], axis=0).astype(BF16)
            ydiag.append(_dot(jnp.concatenate(lhs, axis=1), rhs))
        yoff.append(_dot(cg, sf_ref[0, 0][:, g * 256:(g + 1) * 256].astype(BF16)) * carry_e[:, g * 256:(g + 1) * 256]
                    + _dot(cg, sb_ref[0, 0][:, g * 256:(g + 1) * 256].astype(BF16))
                    * carry_e[:, w + g * 256:w + (g + 1) * 256])
    y = jnp.concatenate(ydiag, axis=1) + jnp.concatenate(yoff, axis=1) + xs * dskip_ref[...]
    o_ref[0] = _rms(y * _silu(z_ref[0]), nw_ref[...])


def _ssd(proj3, xbc_act, dt_bias, a_log, d_skip, norm_w):
    bsz, s, _ = proj3.shape
    n = SSD_CHUNK
    c = s // n
    tri, e = _ssd_tables(n)
    pad = jnp.zeros((LANES - 2 * SSD_HEADS,), F32)
    dtb = jnp.concatenate([dt_bias.reshape(-1), pad]).reshape(1, LANES)
    a = jnp.concatenate([-jnp.exp(a_log.reshape(-1)), pad]).reshape(1, LANES)
    dskip = jnp.repeat(d_skip, SSD_HEADDIM).reshape(1, GROUP_WIDTH)
    cdt = COL_DT // LANES
    const2 = lambda i, t: (0, 0)

    def chunk(t):
        return jnp.where(t < c, t, 2 * c - 1 - t)

    state_shape = jax.ShapeDtypeStruct((bsz, c, SSD_STATE, GROUP_WIDTH), F32)
    sf, sb = pl.pallas_call(
        functools.partial(_ssd_state_body, c),
        grid=(bsz, 2 * c),
        in_specs=[pl.BlockSpec((1, n, SSD_CONV_CH), lambda i, t: (i, chunk(t), 0)),
                  pl.BlockSpec((1, n, LANES), lambda i, t: (i, chunk(t), cdt)),
                  pl.BlockSpec((1, LANES), const2), pl.BlockSpec((1, LANES), const2),
                  pl.BlockSpec((n, n), const2), pl.BlockSpec((2 * LANES, 2 * GROUP_WIDTH), const2)],
        out_specs=[pl.BlockSpec((1, 1, SSD_STATE, GROUP_WIDTH), lambda i, t: (i, jnp.minimum(t, c - 1), 0, 0)),
                   pl.BlockSpec((1, 1, SSD_STATE, GROUP_WIDTH), lambda i, t: (i, jnp.minimum(2 * c - 1 - t, c - 1), 0, 0))],
        out_shape=[state_shape, state_shape],
        scratch_shapes=[pltpu.VMEM((SSD_STATE, GROUP_WIDTH), F32)],
        compiler_params=_cparams(("parallel", "arbitrary"), 32),
    )(xbc_act, proj3, dtb, a, tri, e)

    const2 = lambda i, j: (0, 0)
    return pl.pallas_call(
        _ssd_out_body,
        grid=(bsz, c),
        in_specs=[pl.BlockSpec((1, n, SSD_CONV_CH), lambda i, j: (i, j, 0)),
                  pl.BlockSpec((1, n, LANES), lambda i, j: (i, j, cdt)),
                  pl.BlockSpec((1, n, GROUP_WIDTH), lambda i, j: (i, j, COL_Z // GROUP_WIDTH)),
                  pl.BlockSpec((1, LANES), const2), pl.BlockSpec((1, LANES), const2),
                  pl.BlockSpec((n, n), const2), pl.BlockSpec((2 * LANES, 2 * GROUP_WIDTH), const2),
                  pl.BlockSpec((1, GROUP_WIDTH), const2), pl.BlockSpec((1, GROUP_WIDTH), const2),
                  pl.BlockSpec((1, 1, SSD_STATE, GROUP_WIDTH), lambda i, j: (i, j, 0, 0)),
                  pl.BlockSpec((1, 1, SSD_STATE, GROUP_WIDTH), lambda i, j: (i, j, 0, 0))],
        out_specs=pl.BlockSpec((1, n, GROUP_WIDTH), lambda i, j: (i, j, 0)),
        out_shape=jax.ShapeDtypeStruct((bsz, s, GROUP_WIDTH), F32),
        compiler_params=_cparams(("parallel", "parallel"), 32),
    )(xbc_act, proj3, proj3, dtb, a, tri, e, dskip, norm_w.reshape(1, -1), sf, sb)


def _hy_filter_body(z_ref, w1_ref, b1_ref, w2_ref, b2_ref, w3_ref, fr_ref, dl_ref, hs_ref, hd_ref):
    tl = z_ref.shape[0]
    z = z_ref[...]
    hid = jnp.sin(fr_ref[0:1, :] * (_dot_f32(z, w1_ref[...]) + b1_ref[...]))
    hid = jnp.sin(fr_ref[1:2, :] * (_dot_f32(hid, w2_ref[...]) + b2_ref[...]))
    filt = _dot_f32(hid, w3_ref[...])
    dec = jnp.exp(-z[:, 0:1] * dl_ref[...])
    row = pl.program_id(0) * tl + lax.broadcasted_iota(jnp.int32, (tl, 1), 0)
    for o in range(HY_ORDER):
        base = o * 2 * HY_WIDTH
        hf = filt[:, base:base + HY_WIDTH] * dec
        hb = jnp.where(row == 0, 0.0, filt[:, base + HY_WIDTH:base + 2 * HY_WIDTH] * dec)
        hs_ref[:, o * HY_WIDTH:(o + 1) * HY_WIDTH] = hf + hb
        hd_ref[:, o * HY_WIDTH:(o + 1) * HY_WIDTH] = hf - hb


def _hy_filters(l, w1, b1, w2, b2, w3, freq):
    z, deltas = _hyena_positions(l)
    hid = HY_FILTER_HIDDEN
    w1p = jnp.zeros((LANES, LANES), F32).at[:HY_EMB, :hid].set(w1)
    w2p = jnp.zeros((LANES, LANES), F32).at[:hid, :hid].set(w2)
    w3p = jnp.zeros((LANES, w3.shape[1]), F32).at[:hid].set(w3)
    b1p = jnp.zeros((1, LANES), F32).at[0, :hid].set(b1)
    b2p = jnp.zeros((1, LANES), F32).at[0, :hid].set(b2)
    frp = jnp.zeros((2, LANES), F32).at[:, :hid].set(freq)
    tl = min(256, l)
    ncol = HY_ORDER * HY_WIDTH
    const = lambda i: (0, 0)
    return pl.pallas_call(
        _hy_filter_body,
        grid=(l // tl,),
        in_specs=[pl.BlockSpec((tl, LANES), lambda i: (i, 0)),
                  pl.BlockSpec((LANES, LANES), const), pl.BlockSpec((1, LANES), const),
                  pl.BlockSpec((LANES, LANES), const), pl.BlockSpec((1, LANES), const),
                  pl.BlockSpec((LANES, w3.shape[1]), const), pl.BlockSpec((2, LANES), const),
                  pl.BlockSpec((1, HY_WIDTH), const)],
        out_specs=[pl.BlockSpec((tl, ncol), lambda i: (i, 0)), pl.BlockSpec((tl, ncol), lambda i: (i, 0))],
        out_shape=[jax.ShapeDtypeStruct((l, ncol), F32), jax.ShapeDtypeStruct((l, ncol), F32)],
        compiler_params=_cparams(("parallel",), 32),
    )(z, w1p, b1p, w2p, b2p, w3p, frp, deltas)


def _short_conv_rows(x_ref, w_ref, b_ref, r0, n):
    l = x_ref.shape[1]
    lo = max(r0 - SUBLANES, 0)
    hi = min(r0 + n + SUBLANES, l)
    win = x_ref[0, lo:hi, :]
    pad = jnp.zeros((SUBLANES, win.shape[1]), F32)
    if r0 == 0:
        win = jnp.concatenate([pad, win], axis=0)
    if r0 + n == l:
        win = jnp.concatenate([win, pad], axis=0)
    rows = n + 2 * SUBLANES
    prev = pltpu.roll(win, 1, axis=0)[SUBLANES:SUBLANES + n]
    nxt = pltpu.roll(win, rows - 1, axis=0)[SUBLANES:SUBLANES + n]
    return prev * w_ref[0:1, :] + win[SUBLANES:SUBLANES + n] * w_ref[1:2, :] + nxt * w_ref[2:3, :] + b_ref[...]


def _hy_conv_body(u0_ref, u1_ref, u2_ref, cw0_ref, cw1_ref, cw2_ref, cb0_ref, cb1_ref, cb2_ref, c_ref, s_ref,
                  hr0_ref, hi0_ref, hr1_ref, hi1_ref, b0_ref, b1_ref, o_ref, xb_scr, pre_scr, pim_scr, z_scr):
    l = u0_ref.shape[1]
    tf = min(HY_ROW_TILE, l)
    ntile = l // tf
    sgn = jnp.where(lax.broadcasted_iota(jnp.int32, (l, 1), 0) % 2 == 0, 1.0, -1.0)

    def long_conv(gate_ref, gw_ref, gb_ref, hr_ref, hi_ref, b_ref, out_ref):
        x = z_scr[...]
        xb_scr[...] = x.astype(BF16)
        x_nyq = jnp.sum(x * sgn, axis=0, keepdims=True) * hr_ref[l:l + 1, :]
        for r in range(ntile):
            rows = pl.ds(r * tf, tf)
            xc = _dot(c_ref[rows, :], xb_scr[...])
            xs = _dot(s_ref[rows, :], xb_scr[...])
            hre = hr_ref[rows, :]
            him = hi_ref[rows, :]
            pre_scr[rows, :] = (xc * hre + xs * him).astype(BF16)
            pim_scr[rows, :] = (xc * him - xs * hre).astype(BF16)
        for r in range(ntile):
            rows = pl.ds(r * tf, tf)
            y = _dot(c_ref[rows, :], pre_scr[...]) - _dot(s_ref[rows, :], pim_scr[...])
            y = y + sgn[r * tf:(r + 1) * tf] * x_nyq
            gate = _short_conv_rows(gate_ref, gw_ref, gb_ref, r * tf, tf)
            out_ref[rows, :] = gate * (y + z_scr[rows, :] * b_ref[0])

    for r in range(ntile):
        z_scr[pl.ds(r * tf, tf), :] = _short_conv_rows(u0_ref, cw0_ref, cb0_ref, r * tf, tf)
    long_conv(u1_ref, cw1_ref, cb1_ref, hr0_ref, hi0_ref, b0_ref, z_scr)
    long_conv(u2_ref, cw2_ref, cb2_ref, hr1_ref, hi1_ref, b1_ref, o_ref.at[0])


def _hy_long_conv(proj3, conv_w, conv_b, cmat, smat, hre, him, bias):
    bsz, l, _ = proj3.shape
    tc = 256
    nb = HY_WIDTH // tc
    off = COL_HY // tc
    const = lambda j, i: (0, 0)
    single = pl.Buffered(1)
    bias3 = bias.reshape(HY_ORDER, 1, HY_WIDTH)
    conv_b2 = conv_b.reshape(1, -1)

    def part(k):
        return [pl.BlockSpec((1, l, tc), lambda j, i: (i, 0, off + k * nb + j))]

    def part_w(k):
        return [pl.BlockSpec((HY_SHORT, tc), lambda j, i: (0, k * nb + j))]

    def part_b(k):
        return [pl.BlockSpec((1, tc), lambda j, i: (0, k * nb + j))]

    return pl.pallas_call(
        _hy_conv_body,
        grid=(nb, bsz),
        in_specs=part(0) + part(1) + part(2) + part_w(0) + part_w(1) + part_w(2) + part_b(0) + part_b(1) + part_b(2) + [
            pl.BlockSpec((l, l), const, pipeline_mode=single),
            pl.BlockSpec((l, l), const, pipeline_mode=single),
            pl.BlockSpec((l + SUBLANES, tc), lambda j, i: (0, j), pipeline_mode=single),
            pl.BlockSpec((l, tc), lambda j, i: (0, j), pipeline_mode=single),
            pl.BlockSpec((l + SUBLANES, tc), lambda j, i: (0, nb + j), pipeline_mode=single),
            pl.BlockSpec((l, tc), lambda j, i: (0, nb + j), pipeline_mode=single),
            pl.BlockSpec((1, 1, tc), lambda j, i: (0, 0, j)),
            pl.BlockSpec((1, 1, tc), lambda j, i: (1, 0, j))],
        out_specs=pl.BlockSpec((1, l, tc), lambda j, i: (i, 0, j)),
        out_shape=jax.ShapeDtypeStruct((bsz, l, HY_WIDTH), F32),
        scratch_shapes=[pltpu.VMEM((l, tc), BF16), pltpu.VMEM((l, tc), BF16), pltpu.VMEM((l, tc), BF16),
                        pltpu.VMEM((l, tc), F32)],
        compiler_params=_cparams(("parallel", "parallel"), 56),
    )(proj3, proj3, proj3, conv_w, conv_w, conv_w, conv_b2, conv_b2, conv_b2, cmat, smat, hre, him, hre, him,
      bias3, bias3)


def _hyena(proj3, conv_w, conv_b, w1, b1, w2, b2, w3, freq, bias):
    l = proj3.shape[1]
    hs, hd = _hy_filters(l, w1, b1, w2, b2, w3, freq)
    cmat, smat, c_sc, s_sc_neg = _dft_tables(l)
    tn = 256
    hre = _matmul(c_sc, hs.astype(BF16), l + SUBLANES, tn)
    him = _matmul(s_sc_neg, hd.astype(BF16), l, tn)
    return _hy_long_conv(proj3, conv_w, conv_b, cmat, smat, hre, him, bias)


def _out_proj_body(route, a_ref, b_ref, c_ref, d_ref, x_ref, w_ref, dn_ref, g_ref, beta_ref, *rest):
    gw = GROUP_WIDTH
    h = _dot(a_ref[...].astype(BF16), w_ref[0:gw, :])
    h += _dot(b_ref[...].astype(BF16), w_ref[gw:2 * gw, :])
    h += _dot(c_ref[...].astype(BF16), w_ref[2 * gw:3 * gw, :])
    h += _dot(_rms(d_ref[...], dn_ref[...]).astype(BF16), w_ref[3 * gw:4 * gw, :])
    y = _layer_norm(ALPHA * x_ref[...] + h, g_ref[...], beta_ref[...])
    if route:
        wr_ref, o_ref, rows_ref, idx_ref, gate_ref = rest
        _store_row_tiles(rows_ref, y)
        _route_top2(y, wr_ref, idx_ref, gate_ref)
    else:
        o_ref, = rest
    o_ref[...] = y


def _store_row_tiles(dst_ref, y):
    rows = y.shape[0]
    for c in range(ROW_CHUNKS):
        dst_ref[pl.ds(c, rows, stride=ROW_CHUNKS), :] = y[:, c * LANES:(c + 1) * LANES]


def _load_row_tiles(src_ref, first, rows):
    return jnp.concatenate([src_ref[pl.ds(first * ROW_CHUNKS + c, rows, stride=ROW_CHUNKS), :]
                            for c in range(ROW_CHUNKS)], axis=1)


def _out_proj_ln(a, b, c, d, x, w_out, hy_norm, g, beta, w_router=None):
    m = x.shape[0]
    tm = min(512, m)
    gw = GROUP_WIDTH
    const = lambda i: (0, 0)
    row = lambda i: (i, 0)
    route = w_router is not None
    operands = [a, b, c, d, x, w_out, hy_norm.reshape(1, -1), g.reshape(1, -1), beta.reshape(1, -1)]
    in_specs = [pl.BlockSpec((tm, gw), row), pl.BlockSpec((tm, gw), row), pl.BlockSpec((tm, gw), row),
                pl.BlockSpec((tm, gw), row), pl.BlockSpec((tm, D_MODEL), row),
                pl.BlockSpec((D_MODEL, D_MODEL), const), pl.BlockSpec((1, gw), const),
                pl.BlockSpec((1, D_MODEL), const), pl.BlockSpec((1, D_MODEL), const)]
    out_specs = [pl.BlockSpec((tm, D_MODEL), row)]
    out_shape = [jax.ShapeDtypeStruct((m, D_MODEL), F32)]
    if route:
        operands.append(jnp.zeros((D_MODEL, LANES), F32).at[:, :N_EXPERTS].set(w_router))
        in_specs.append(pl.BlockSpec((D_MODEL, LANES), const))
        out_specs += [pl.BlockSpec((tm * ROW_CHUNKS, LANES), row), pl.BlockSpec((tm, LANES), row),
                      pl.BlockSpec((tm, LANES), row)]
        out_shape += [jax.ShapeDtypeStruct((m * ROW_CHUNKS, LANES), F32), jax.ShapeDtypeStruct((m, LANES), jnp.int32),
                      jax.ShapeDtypeStruct((m, LANES), F32)]
    return pl.pallas_call(
        functools.partial(_out_proj_body, route),
        grid=(m // tm,),
        in_specs=in_specs,
        out_specs=out_specs,
        out_shape=out_shape,
        compiler_params=_cparams(("parallel",), 48),
    )(*operands)


def _ffn_body(x_ref, wg_ref, wu_ref, wd_ref, g_ref, beta_ref, o_ref, xb_ref, acc_ref):
    j = pl.program_id(1)

    @pl.when(j == 0)
    def _():
        xb_ref[...] = x_ref[...].astype(BF16)
        acc_ref[...] = jnp.zeros_like(acc_ref)

    xb = xb_ref[...]
    hidden = (_silu(_dot(xb, wg_ref[...])) * _dot(xb, wu_ref[...])).astype(BF16)
    acc_ref[...] += _dot(hidden, wd_ref[...])

    @pl.when(j == pl.num_programs(1) - 1)
    def _():
        o_ref[...] = _layer_norm(ALPHA * x_ref[...] + acc_ref[...], g_ref[...], beta_ref[...])


def _ffn_ln(x, wg, wu, wd, g, beta):
    m = x.shape[0]
    dff = wg.shape[1]
    tm = min(512, m)
    tf = 512
    const = lambda i, j: (0, 0)
    return pl.pallas_call(
        _ffn_body,
        grid=(m // tm, dff // tf),
        in_specs=[pl.BlockSpec((tm, D_MODEL), lambda i, j: (i, 0)),
                  pl.BlockSpec((D_MODEL, tf), lambda i, j: (0, j)),
                  pl.BlockSpec((D_MODEL, tf), lambda i, j: (0, j)),
                  pl.BlockSpec((tf, D_MODEL), lambda i, j: (j, 0)),
                  pl.BlockSpec((1, D_MODEL), const), pl.BlockSpec((1, D_MODEL), const)],
        out_specs=pl.BlockSpec((tm, D_MODEL), lambda i, j: (i, 0)),
        out_shape=jax.ShapeDtypeStruct((m, D_MODEL), F32),
        scratch_shapes=[pltpu.VMEM((tm, D_MODEL), BF16), pltpu.VMEM((tm, D_MODEL), F32)],
        compiler_params=_cparams(("parallel", "arbitrary"), 48),
    )(x, wg, wu, wd, g.reshape(1, -1), beta.reshape(1, -1))


def _route_top2(x, w_ref, idx_ref, gate_ref):
    logits = _dot_f32(x, w_ref[...])
    lane = lax.broadcasted_iota(jnp.int32, logits.shape, 1)
    neg = -jnp.inf
    logits = jnp.where(lane < N_EXPERTS, logits, neg)
    m1 = jnp.max(logits, axis=-1, keepdims=True)
    i1 = jnp.min(jnp.where(logits == m1, lane, LANES), axis=-1, keepdims=True)
    rest = jnp.where(lane == i1, neg, logits)
    m2 = jnp.max(rest, axis=-1, keepdims=True)
    i2 = jnp.min(jnp.where(rest == m2, lane, LANES), axis=-1, keepdims=True)
    e2 = jnp.exp(m2 - m1)
    den = 1.0 + e2
    idx_ref[...] = jnp.where(lane == 0, i1, jnp.where(lane == 1, i2, 0))
    gate_ref[...] = jnp.where(lane == 0, 1.0 / den, jnp.where(lane == 1, e2 / den, 0.0))


def _row_copy(src_ref, src_row, buf_ref, buf_row, sem):
    src = src_ref.at[pl.ds(pl.multiple_of(src_row * ROW_CHUNKS, ROW_CHUNKS), ROW_CHUNKS)]
    dst = buf_ref.at[pl.ds(pl.multiple_of(buf_row * ROW_CHUNKS, ROW_CHUNKS), ROW_CHUNKS)]
    return pltpu.make_async_copy(src, dst, sem)


def _start_row_gather(idx_ref, n, src_ref, buf_ref, sem):
    def start(r, carry):
        _row_copy(src_ref, idx_ref[0, 0, r], buf_ref, r, sem).start()
        return carry

    lax.fori_loop(0, n, start, 0, unroll=DMA_LOOP_UNROLL)


def _wait_row_gather(n, src_ref, buf_ref, sem):
    def wait(r, carry):
        _row_copy(src_ref, 0, buf_ref, r, sem).wait()
        return carry

    lax.fori_loop(0, n, wait, 0, unroll=DMA_LOOP_UNROLL)


def _pipelined_row_gather(idx_cur_ref, idx_next_ref, n, src_ref, buf, sems):
    i = pl.program_id(0)
    slot = i % 2

    @pl.when(i == 0)
    def _():
        _start_row_gather(idx_cur_ref, n, src_ref, buf.at[0], sems.at[0])

    @pl.when(i + 1 < pl.num_programs(0))
    def _():
        _start_row_gather(idx_next_ref, n, src_ref, buf.at[1 - slot], sems.at[1 - slot])

    _wait_row_gather(n, src_ref, buf.at[slot], sems.at[slot])
    return slot


def _idx_specs(n, nblk):
    return [pl.BlockSpec((1, 1, n), lambda i: (i, 0, 0), memory_space=pltpu.SMEM),
            pl.BlockSpec((1, 1, n), lambda i: (jnp.minimum(i + 1, nblk - 1), 0, 0), memory_space=pltpu.SMEM)]


def _gather_x_body(idx_cur_ref, idx_next_ref, src_ref, o_ref, buf, sems):
    rows = o_ref.shape[0]
    slot = _pipelined_row_gather(idx_cur_ref, idx_next_ref, rows, src_ref, buf, sems)
    o_ref[...] = _load_row_tiles(buf.at[slot], 0, rows).astype(BF16)


def _gather_x(x_tiles, slot_tok):
    cap = slot_tok.shape[0]
    rows = GATHER_ROWS
    assert cap % rows == 0
    nblk = cap // rows
    idx = slot_tok.reshape(nblk, 1, rows)
    return pl.pallas_call(
        _gather_x_body,
        grid=(nblk,),
        in_specs=_idx_specs(rows, nblk) + [pl.BlockSpec(memory_space=pl.ANY)],
        out_specs=pl.BlockSpec((rows, D_MODEL), lambda i: (i, 0)),
        out_shape=jax.ShapeDtypeStruct((cap, D_MODEL), BF16),
        scratch_shapes=[pltpu.VMEM((2, rows * ROW_CHUNKS, LANES), F32), pltpu.SemaphoreType.DMA((2,))],
        compiler_params=pltpu.CompilerParams(dimension_semantics=("arbitrary",), disable_bounds_checks=True),
    )(idx, idx, x_tiles)


def _expert_body(be_ref, nu_ref, x_ref, wg_ref, wu_ref, wd_ref, o_ref, acc_ref):
    i = pl.program_id(0)
    j = pl.program_id(1)

    @pl.when(j == 0)
    def _():
        acc_ref[...] = jnp.zeros_like(acc_ref)

    @pl.when(i < nu_ref[0])
    def _():
        xb = x_ref[...]
        hidden = (_silu(_dot(xb, wg_ref[0])) * _dot(xb, wu_ref[0])).astype(BF16)
        acc_ref[...] += _dot(hidden, wd_ref[0])

    @pl.when(j == pl.num_programs(1) - 1)
    def _():
        _store_row_tiles(o_ref, acc_ref[...])


def _expert_ffn(x_slots, block_expert, n_used, wg, wu, wd, tm):
    cap = x_slots.shape[0]
    dff = wg.shape[2]
    tf = 512

    def ff_tile(i, j, nu):
        return jnp.where(i < nu[0], j, 0)

    grid_spec = pltpu.PrefetchScalarGridSpec(
        num_scalar_prefetch=2,
        grid=(cap // tm, dff // tf),
        in_specs=[pl.BlockSpec((tm, D_MODEL), lambda i, j, be, nu: (i, 0)),
                  pl.BlockSpec((1, D_MODEL, tf), lambda i, j, be, nu: (be[i], 0, ff_tile(i, j, nu))),
                  pl.BlockSpec((1, D_MODEL, tf), lambda i, j, be, nu: (be[i], 0, ff_tile(i, j, nu))),
                  pl.BlockSpec((1, tf, D_MODEL), lambda i, j, be, nu: (be[i], ff_tile(i, j, nu), 0))],
        out_specs=pl.BlockSpec((tm * ROW_CHUNKS, LANES), lambda i, j, be, nu: (i, 0)),
        scratch_shapes=[pltpu.VMEM((tm, D_MODEL), F32)],
    )
    return pl.pallas_call(
        _expert_body,
        grid_spec=grid_spec,
        out_shape=jax.ShapeDtypeStruct((cap * ROW_CHUNKS, LANES), F32),
        compiler_params=_cparams(("parallel", "arbitrary"), 56),
    )(block_expert, n_used, x_slots, wg, wu, wd)


def _combine_body(idx_cur_ref, idx_next_ref, x_ref, gt_ref, g_ref, beta_ref, y_ref, o_ref, buf, sems):
    rows = x_ref.shape[0]
    slot = _pipelined_row_gather(idx_cur_ref, idx_next_ref, TOP_K * rows, y_ref, buf, sems)
    gt = gt_ref[...]
    f = (gt[:, 0:1] * _load_row_tiles(buf.at[slot], 0, rows)
         + gt[:, 1:2] * _load_row_tiles(buf.at[slot], rows, rows))
    o_ref[...] = _layer_norm(ALPHA * x_ref[...] + f, g_ref[...], beta_ref[...])


def _combine_ln(x, y_tiles, pos, gates, g, beta):
    m = x.shape[0]
    rows = min(COMBINE_ROWS, m)
    nblk = m // rows
    n = TOP_K * rows
    idx = pos.reshape(nblk, rows, TOP_K).transpose(0, 2, 1).reshape(nblk, 1, n)
    const = lambda i: (0, 0)
    return pl.pallas_call(
        _combine_body,
        grid=(nblk,),
        in_specs=_idx_specs(n, nblk) + [
            pl.BlockSpec((rows, D_MODEL), lambda i: (i, 0)),
            pl.BlockSpec((rows, LANES), lambda i: (i, 0)),
            pl.BlockSpec((1, D_MODEL), const), pl.BlockSpec((1, D_MODEL), const),
            pl.BlockSpec(memory_space=pl.ANY)],
        out_specs=pl.BlockSpec((rows, D_MODEL), lambda i: (i, 0)),
        out_shape=jax.ShapeDtypeStruct((m, D_MODEL), F32),
        scratch_shapes=[pltpu.VMEM((2, n * ROW_CHUNKS, LANES), F32), pltpu.SemaphoreType.DMA((2,))],
        compiler_params=pltpu.CompilerParams(dimension_semantics=("arbitrary",), disable_bounds_checks=True,
                                             vmem_limit_bytes=48 * 1024 * 1024),
    )(idx, idx, x, gates, g.reshape(1, -1), beta.reshape(1, -1), y_tiles)


def _moe_ln(x, x_tiles, idx, gates, wg, wu, wd, g, beta, tm):
    m = x.shape[0]
    n_asg = m * TOP_K
    e_flat = idx[:, :TOP_K].reshape(-1)
    onehot = (e_flat[:, None] == jnp.arange(N_EXPERTS, dtype=jnp.int32)[None, :]).astype(jnp.int32)
    csum = jnp.cumsum(onehot, axis=0)
    counts = csum[-1]
    rank = jnp.sum(csum * onehot, axis=1) - 1
    padded = (counts + tm - 1) // tm * tm
    pend = jnp.cumsum(padded)
    pstart = pend - padded
    dest = (jnp.sum(pstart[None, :] * onehot, axis=1) + rank).astype(jnp.int32)
    cap = n_asg + N_EXPERTS * tm
    tok = jnp.arange(n_asg, dtype=jnp.int32) // TOP_K
    slot_tok = jnp.zeros((cap,), jnp.int32).at[dest].set(tok)
    nblk = cap // tm
    block_expert = jnp.minimum(jnp.searchsorted(pend, jnp.arange(nblk, dtype=pend.dtype) * tm, side='right'),
                               N_EXPERTS - 1).astype(jnp.int32)
    n_used = (pend[-1] // tm).astype(jnp.int32).reshape(1)

    x_slots = _gather_x(x_tiles, slot_tok)
    y_tiles = _expert_ffn(x_slots, block_expert, n_used, wg, wu, wd, tm)
    return _combine_ln(x, y_tiles, dest.reshape(m, TOP_K), gates, g, beta)


def _mixer_ln(x2, bsz, s, p, w_router=None):
    proj = _matmul(x2, p['w_in'], min(1024, x2.shape[0]), 896)
    proj3 = proj.reshape(bsz, s, PROJ_COLS)
    cos_slab, sin_slab, cos4, sin4 = _rope_tables(s)
    out_a = _mla(proj3, cos_slab, sin_slab, p['mla_q_norm'], p['w_q'], p['w_qr'], p['mla_kv_norm'], p['w_kv'],
                 p['mla_out_norm'])
    out_b = _retention(proj3, cos4, sin4)
    xbc_act = _dwconv(proj3, COL_XBC, SSD_CONV_CH, p['ssd_conv_w'], p['ssd_conv_b'], act=True)
    out_c = _ssd(proj3, xbc_act, p['ssd_dt_bias'], p['ssd_a_log'], p['ssd_d'], p['ssd_norm'])
    out_d = _hyena(proj3, p['hy_conv_w'], p['hy_conv_b'], p['hy_w1'], p['hy_b1'], p['hy_w2'], p['hy_b2'], p['hy_w3'],
                   p['hy_freq'], p['hy_bias'])
    m = bsz * s
    gw = GROUP_WIDTH
    return _out_proj_ln(out_a.reshape(m, gw), out_b.reshape(m, gw), out_c.reshape(m, gw), out_d.reshape(m, gw),
                        x2, p['w_out'], p['hy_out_norm'], p['ln1_g'], p['ln1_b'], w_router)


_MIXER_KEYS = ('mla_q_norm', 'mla_kv_norm', 'mla_out_norm', 'ssd_conv_w', 'ssd_conv_b', 'ssd_dt_bias', 'ssd_a_log',
               'ssd_d', 'ssd_norm', 'hy_conv_w', 'hy_conv_b', 'hy_w1', 'hy_b1', 'hy_w2', 'hy_b2', 'hy_w3', 'hy_freq',
               'hy_bias', 'hy_out_norm', 'ln1_g', 'ln1_b')

MOE_TM = 1024


def kernel(x, w_in, mla_q_norm, mla_w_uq, mla_kv_norm, mla_w_ukv, mla_out_norm, ssd_conv_w, ssd_conv_b, ssd_dt_bias, ssd_a_log, ssd_d, ssd_norm, hy_conv_w, hy_conv_b, hy_w1, hy_b1, hy_w2, hy_b2, hy_w3, hy_freq, hy_bias, hy_out_norm, w_out, ln1_g, ln1_b, ln2_g, ln2_b, ffn_w_gate, ffn_w_up, ffn_w_down, moe_router, moe_w_gate, moe_w_up, moe_w_down):
    args = dict(locals())
    bsz, s, d = x.shape
    x2 = x.reshape(bsz * s, d)
    for layer in range(DEPTH):
        p = {k: args[k][layer] for k in _MIXER_KEYS}
        p['w_in'] = _prep_w_in(w_in[layer])
        p['w_q'], p['w_qr'] = _prep_w_uq(mla_w_uq[layer])
        p['w_kv'] = _prep_w_ukv(mla_w_ukv[layer])
        p['w_out'] = w_out[layer].astype(BF16)
        j = layer // 2
        if layer % 2 == 0:
            x2, = _mixer_ln(x2, bsz, s, p)
            x2 = _ffn_ln(x2, ffn_w_gate[j].astype(BF16), ffn_w_up[j].astype(BF16), ffn_w_down[j].astype(BF16),
                         ln2_g[layer], ln2_b[layer])
        else:
            x2, x_tiles, idx, gates = _mixer_ln(x2, bsz, s, p, moe_router[j])
            x2 = _moe_ln(x2, x_tiles, idx, gates, moe_w_gate[j].astype(BF16), moe_w_up[j].astype(BF16),
                         moe_w_down[j].astype(BF16), ln2_g[layer], ln2_b[layer], MOE_TM)
    return x2.reshape(bsz, s, d)
```

```python
import functools
import math

import numpy as np
import jax
import jax.numpy as jnp
from jax import lax
from jax.experimental import pallas as pl
from jax.experimental.pallas import tpu as pltpu

F32 = jnp.float32
BF16 = jnp.bfloat16
HIGHEST = lax.Precision.HIGHEST

D_MODEL = 2048
DEPTH = 2
GROUP_WIDTH = 512
MLA_HEADS = 4
MLA_NOPE = 128
MLA_ROPE = 64
MLA_V = 128
MLA_Q_LORA = 384
MLA_KV_LORA = 256
RET_HEADS = 4
RET_DV = 128
RET_DK = 64
RET_DECAY_EXP_FWD = 5.0
RET_DECAY_EXP_BWD = 5.5
SSD_HEADDIM = 64
SSD_HEADS = 8
SSD_GROUPS = 2
SSD_STATE = 128
SSD_CONV = 5
SSD_CONV_CH = 1024
HY_ORDER = 2
HY_WIDTH = 512
HY_SHORT = 3
HY_EMB = 33
HY_FILTER_HIDDEN = 64
HY_MIN_DECAY = math.log(1e-2) / 1.5
HY_MAX_DECAY = math.log(1e-2) / 0.3
N_EXPERTS = 8
TOP_K = 2
ROPE_BASE = 10000.0
ALPHA = (2 * DEPTH) ** 0.25

V7X_VMEM_BYTES = 64 * 1024 * 1024
LANES = 128
SUBLANES = 8

PROJ_COLS = 6400
COL_HY = 0
COL_QC = 1536
COL_RQ = 2048
COL_KVPE = 4096
COL_Z = 4608
COL_XBC = 5120
COL_DT = 6144
PROJ_TILE = 1280

RET_CHUNK = 256
SSD_CHUNK = 128
HY_ROW_TILE = 512
RET_OUT_CHUNKS = 2
SSD_OUT_CHUNKS = 4
ROW_CHUNKS = D_MODEL // LANES
GATHER_ROWS = 512
COMBINE_ROWS = 256
DMA_LOOP_UNROLL = 8


def _cparams(semantics, vmem_mb):
    assert vmem_mb * 1024 * 1024 < V7X_VMEM_BYTES
    return pltpu.CompilerParams(dimension_semantics=semantics, vmem_limit_bytes=vmem_mb * 1024 * 1024)


def _sigmoid(x):
    return 1.0 / (1.0 + jnp.exp(-x))


def _silu(x):
    return x * _sigmoid(x)


def _rms(x, w, eps=1e-6):
    return x * lax.rsqrt(jnp.mean(x * x, axis=-1, keepdims=True) + eps) * w


def _layer_norm(y, g, b, eps=1e-5):
    mu = jnp.mean(y, axis=-1, keepdims=True)
    d = y - mu
    var = jnp.mean(d * d, axis=-1, keepdims=True)
    return d * lax.rsqrt(var + eps) * g + b


def _dot(a, b):
    return jnp.dot(a, b, preferred_element_type=F32)


def _dot_nt(a, b):
    return lax.dot_general(a, b, (((1,), (1,)), ((), ())), preferred_element_type=F32)


def _dot_tn(a, b):
    return lax.dot_general(a, b, (((0,), (0,)), ((), ())), preferred_element_type=F32)


def _dot_f32(a, b):
    return jnp.dot(a, b, preferred_element_type=F32, precision=HIGHEST)


def _mm_body(x_ref, w_ref, o_ref, xb_ref):
    @pl.when(pl.program_id(1) == 0)
    def _():
        xb_ref[...] = x_ref[...].astype(BF16)

    o_ref[...] = _dot(xb_ref[...], w_ref[...])


def _matmul(x, w, tm, tn, vmem_mb=48):
    m, k = x.shape
    n = w.shape[1]
    assert m % tm == 0 and n % tn == 0
    return pl.pallas_call(
        _mm_body,
        grid=(m // tm, n // tn),
        in_specs=[pl.BlockSpec((tm, k), lambda i, j: (i, 0)),
                  pl.BlockSpec((k, tn), lambda i, j: (0, j))],
        out_specs=pl.BlockSpec((tm, tn), lambda i, j: (i, j)),
        out_shape=jax.ShapeDtypeStruct((m, n), F32),
        scratch_shapes=[pltpu.VMEM((tm, k), BF16)],
        compiler_params=_cparams(("parallel", "arbitrary"), vmem_mb),
    )(x, w)


def _rot_half_cols(w, heads):
    k = w.shape[0]
    w = w.reshape(k, heads, 2, 32)
    return jnp.stack([-w[:, :, 1], w[:, :, 0]], axis=2).reshape(k, heads * 64)


def _prep_w_in(w):
    k = w.shape[0]

    def sl(a, b):
        return w[:, a:b]

    def zc(n):
        return jnp.zeros((k, n), w.dtype)

    q_c, kv_c, k_pe = sl(0, 384), sl(384, 640), sl(640, 704)
    r_q, r_k, r_v, r_g = sl(704, 960), sl(960, 1216), sl(1216, 1728), sl(1728, 2240)
    m_z, m_xbc, m_dt, h_u = sl(2240, 2752), sl(2752, 3776), sl(3776, 3792), sl(3792, 5328)
    cols = [h_u, q_c, zc(128),
            r_q, r_k, _rot_half_cols(r_q, RET_HEADS), _rot_half_cols(r_k, RET_HEADS), r_v, r_g,
            kv_c, k_pe, zc(64), _rot_half_cols(k_pe, 1), zc(64),
            m_z, m_xbc, m_dt, zc(112), zc(PROJ_COLS - COL_DT - LANES)]
    out = jnp.concatenate(cols, axis=1).astype(BF16)
    assert out.shape[1] == PROJ_COLS
    return out


def _prep_w_uq(w):
    k = w.shape[0]
    w = w.reshape(k, MLA_HEADS, MLA_NOPE + MLA_ROPE)
    nope, rope = w[:, :, :MLA_NOPE], w[:, :, MLA_NOPE:]
    z = jnp.zeros((k, MLA_HEADS, 64), w.dtype)
    main = jnp.concatenate([nope, rope, z], axis=2).reshape(k, MLA_HEADS * 256)
    rr = _rot_half_cols(rope.reshape(k, MLA_HEADS * 64), MLA_HEADS).reshape(k, MLA_HEADS, 64)
    rot = jnp.concatenate([rr, z], axis=2).reshape(k, MLA_HEADS * 128)
    return main.astype(BF16), rot.astype(BF16)


def _prep_w_ukv(w):
    k = w.shape[0]
    w = w.reshape(k, MLA_HEADS, MLA_NOPE + MLA_V)
    return jnp.concatenate([w[:, :, :MLA_NOPE].reshape(k, -1), w[:, :, MLA_NOPE:].reshape(k, -1)], axis=1).astype(BF16)


def _rope_tables(s):
    half = 32
    inv_freq = ROPE_BASE ** (-jnp.arange(half, dtype=F32) * 2.0 / 64)
    ang = jnp.arange(s, dtype=F32)[:, None] * inv_freq[None, :]
    cos, sin = jnp.cos(ang), jnp.sin(ang)
    cos64 = jnp.concatenate([cos, cos], axis=1)
    sin64 = jnp.concatenate([sin, sin], axis=1)
    z = jnp.zeros((s, 64), F32)
    return (jnp.concatenate([cos64, z], axis=1), jnp.concatenate([sin64, z], axis=1),
            jnp.tile(cos64, (1, 4)), jnp.tile(sin64, (1, 4)))


def _ret_tables(n):
    heads = jnp.arange(RET_HEADS, dtype=F32)
    lgf = jnp.log1p(-jnp.exp2(-RET_DECAY_EXP_FWD - heads))
    lgb = jnp.log1p(-jnp.exp2(-RET_DECAY_EXP_BWD - heads))
    idx = jnp.arange(n, dtype=F32)

    def lanes(tab, width):
        return jnp.repeat(tab, width, axis=1)

    kdf = lanes(jnp.exp((n - 1.0 - idx)[:, None] * lgf), RET_DK)
    kdb = lanes(jnp.exp(idx[:, None] * lgb), RET_DK)
    qdf = lanes(jnp.exp((idx + 1.0)[:, None] * lgf), RET_DK)
    qdb = lanes(jnp.exp((n - idx)[:, None] * lgb), RET_DK)
    cdf = lanes(jnp.exp(n * lgf)[None, :], RET_DV)
    cdb = lanes(jnp.exp(n * lgb)[None, :], RET_DV)
    diff = idx[:, None] - idx[None, :]
    dec = jnp.where(diff[None] >= 0,
                    jnp.exp(jnp.maximum(diff, 0.0)[None] * lgf[:, None, None]),
                    jnp.exp(jnp.maximum(-diff, 0.0)[None] * lgb[:, None, None]))
    bd = (jnp.arange(RET_HEADS * RET_DK)[:, None] // RET_DK == jnp.arange(RET_HEADS * RET_DV)[None, :] // RET_DV)
    return kdf, kdb, qdf, qdb, cdf, cdb, dec.astype(F32), bd.astype(F32)


def _ssd_tables(n):
    tri = (jnp.arange(n)[:, None] >= jnp.arange(n)[None, :]).astype(F32)
    lane_head = jnp.arange(GROUP_WIDTH) // SSD_HEADDIM
    r = jnp.arange(LANES)
    ef = (r[:, None] == lane_head[None, :]).astype(BF16)
    eb = (r[:, None] == lane_head[None, :] + SSD_HEADS).astype(BF16)
    e = jnp.concatenate([ef, eb], axis=1)
    return tri, jnp.concatenate([e, e], axis=0)


def _dft_tables(l):
    f = jnp.arange(l, dtype=jnp.int32)
    k = (f[:, None] * f[None, :]) % (2 * l)
    ang = k.astype(F32) * (math.pi / l)
    c, s = jnp.cos(ang), jnp.sin(ang)
    sc = jnp.where(f == 0, 1.0, 2.0).astype(F32)[:, None] / (2.0 * l)
    nyq = jnp.where(f % 2 == 0, 1.0, -1.0).astype(F32)[None, :] / (2.0 * l)
    c_sc = jnp.concatenate([c * sc, nyq, jnp.zeros((7, l), F32)], axis=0)
    return c.astype(BF16), s.astype(BF16), c_sc.astype(BF16), (-s * sc).astype(BF16)


def _hyena_positions(l):
    t = jnp.linspace(0.0, 1.0, l, dtype=F32)[:, None]
    bands = (HY_EMB - 1) // 2
    ang = 2.0 * math.pi * jnp.arange(l, dtype=F32)[:, None] / l
    f = jnp.linspace(1e-4, bands - 1, bands, dtype=F32)[None, :]
    z = jnp.concatenate([t, jnp.cos(f * ang), -jnp.sin(f * ang), jnp.zeros((l, LANES - HY_EMB), F32)], axis=-1)
    deltas = jnp.abs(jnp.linspace(HY_MIN_DECAY, HY_MAX_DECAY, HY_WIDTH, dtype=F32))[None, :]
    return z, deltas


def _conv_body(taps, act, x_ref, p_ref, n_ref, w_ref, b_ref, o_ref):
    r = pl.program_id(1)
    tr = x_ref.shape[1]
    half = taps // 2
    prev = jnp.where(r > 0, p_ref[0], 0.0)
    nxt = jnp.where(r < pl.num_programs(1) - 1, n_ref[0], 0.0)
    win = jnp.concatenate([prev, x_ref[0], nxt], axis=0)
    rows = tr + 2 * SUBLANES
    acc = jnp.broadcast_to(b_ref[...], (tr, x_ref.shape[2]))
    for t in range(taps):
        sh = (half - t) % rows
        shifted = win if sh == 0 else pltpu.roll(win, sh, axis=0)
        acc = acc + shifted[SUBLANES:SUBLANES + tr] * w_ref[t:t + 1, :]
    if act:
        acc = _silu(acc)
    o_ref[0] = acc


def _dwconv(proj3, col0, width, w, b, act):
    bsz, s, _ = proj3.shape
    taps = w.shape[0]
    cb = 512
    tr = min(512, s)
    off = col0 // cb
    nb8 = s // SUBLANES
    t8 = tr // SUBLANES
    return pl.pallas_call(
        functools.partial(_conv_body, taps, act),
        grid=(bsz, s // tr, width // cb),
        in_specs=[pl.BlockSpec((1, tr, cb), lambda i, r, c: (i, r, c + off)),
                  pl.BlockSpec((1, SUBLANES, cb), lambda i, r, c: (i, jnp.maximum(r * t8 - 1, 0), c + off)),
                  pl.BlockSpec((1, SUBLANES, cb), lambda i, r, c: (i, jnp.minimum((r + 1) * t8, nb8 - 1), c + off)),
                  pl.BlockSpec((taps, cb), lambda i, r, c: (0, c)),
                  pl.BlockSpec((1, cb), lambda i, r, c: (0, c))],
        out_specs=pl.BlockSpec((1, tr, cb), lambda i, r, c: (i, r, c)),
        out_shape=jax.ShapeDtypeStruct((bsz, s, width), F32),
        compiler_params=_cparams(("parallel", "parallel", "parallel"), 32),
    )(proj3, proj3, proj3, w, b.reshape(1, width))


def _mla_body(qc_ref, kvpe_ref, cq_ref, sq_ref, ck_ref, sk_ref, qn_ref, wq_ref, wqr_ref, kvn_ref, wkv_ref, on_ref,
              o_ref, k_scr, v_scr):
    @pl.when(pl.program_id(1) == 0)
    def _():
        kvpe = kvpe_ref[0]
        kvn = _rms(kvpe[:, :MLA_KV_LORA], kvn_ref[...]).astype(BF16)
        kpe = (kvpe[:, 256:384] * ck_ref[...] + kvpe[:, 384:512] * sk_ref[...]).astype(BF16)
        for h in range(MLA_HEADS):
            kn = _dot(kvn, wkv_ref[:, h * 128:(h + 1) * 128]).astype(BF16)
            k_scr[h] = jnp.concatenate([kn, kpe], axis=1)
            v_scr[h] = _dot(kvn, wkv_ref[:, 512 + h * 128:512 + (h + 1) * 128]).astype(BF16)

    scale = (MLA_NOPE + MLA_ROPE) ** -0.5
    qn = _rms(qc_ref[0][:, :MLA_Q_LORA], qn_ref[...]).astype(BF16)
    outs = []
    for h in range(MLA_HEADS):
        qm = _dot(qn, wq_ref[:, h * 256:(h + 1) * 256])
        qr = _dot(qn, wqr_ref[:, h * 128:(h + 1) * 128])
        qpe = qm[:, 128:] * cq_ref[...] + qr * sq_ref[...]
        qh = (jnp.concatenate([qm[:, :128], qpe], axis=1) * scale).astype(BF16)
        sc = _dot_nt(qh, k_scr[h])
        p = jnp.exp(sc - jnp.max(sc, axis=-1, keepdims=True))
        den = jnp.sum(p, axis=-1, keepdims=True)
        outs.append(_dot(p.astype(BF16), v_scr[h]) / den)
    o_ref[0] = _rms(jnp.concatenate(outs, axis=1), on_ref[...])


def _mla(proj3, cos_slab, sin_slab, q_norm, w_q, w_qr, kv_norm, w_kv, out_norm):
    bsz, s, _ = proj3.shape
    tq = min(512, s)
    const = lambda i, j: (0, 0)
    return pl.pallas_call(
        _mla_body,
        grid=(bsz, s // tq),
        in_specs=[pl.BlockSpec((1, tq, 512), lambda i, j: (i, j, COL_QC // 512)),
                  pl.BlockSpec((1, s, 512), lambda i, j: (i, 0, COL_KVPE // 512)),
                  pl.BlockSpec((tq, 128), lambda i, j: (j, 0)),
                  pl.BlockSpec((tq, 128), lambda i, j: (j, 0)),
                  pl.BlockSpec((s, 128), const),
                  pl.BlockSpec((s, 128), const),
                  pl.BlockSpec((1, MLA_Q_LORA), const),
                  pl.BlockSpec(w_q.shape, const),
                  pl.BlockSpec(w_qr.shape, const),
                  pl.BlockSpec((1, MLA_KV_LORA), const),
                  pl.BlockSpec(w_kv.shape, const),
                  pl.BlockSpec((1, GROUP_WIDTH), const)],
        out_specs=pl.BlockSpec((1, tq, GROUP_WIDTH), lambda i, j: (i, j, 0)),
        out_shape=jax.ShapeDtypeStruct((bsz, s, GROUP_WIDTH), F32),
        scratch_shapes=[pltpu.VMEM((MLA_HEADS, s, 256), BF16), pltpu.VMEM((MLA_HEADS, s, MLA_V), BF16)],
        compiler_params=_cparams(("parallel", "arbitrary"), 48),
    )(proj3, proj3, cos_slab, sin_slab, cos_slab, sin_slab, q_norm.reshape(1, -1), w_q, w_qr,
      kv_norm.reshape(1, -1), w_kv, out_norm.reshape(1, -1))


def _ret_state_body(nchunks, rk_ref, rkr_ref, rv_ref, cos_ref, sin_ref, kdf_ref, kdb_ref, cdf_ref, cdb_ref, bd_ref,
                    sf_ref, sb_ref, st_scr):
    n = kdf_ref.shape[0]

    def sweep(out_ref, kd_ref, cd_ref, chunk_of):
        st_scr[...] = jnp.zeros_like(st_scr)

        def body(t, carry):
            c = chunk_of(t)
            rows = pl.ds(pl.multiple_of(c * n, n), n)
            k = (rk_ref[0, rows, :] * cos_ref[rows, :] + rkr_ref[0, rows, :] * sin_ref[rows, :]) * (RET_DK ** -0.5)
            out_ref[0, c] = st_scr[...]
            new = _dot_tn((k * kd_ref[...]).astype(BF16), rv_ref[0, rows, :].astype(BF16))
            st_scr[...] = st_scr[...] * cd_ref[...] + new * bd_ref[...]
            return carry

        lax.fori_loop(0, nchunks, body, 0, unroll=2)

    sweep(sf_ref, kdf_ref, cdf_ref, lambda t: t)
    sweep(sb_ref, kdb_ref, cdb_ref, lambda t: nchunks - 1 - t)


def _ret_out_chunk(rq, rk, rqr, rkr, rv, rg, cos, sin, sf, sb, qdf_ref, qdb_ref, dec_ref):
    q = rq * cos + rqr * sin
    kb = ((rk * cos + rkr * sin) * (RET_DK ** -0.5)).astype(BF16)
    vb = rv.astype(BF16)
    cross = (_dot((q * qdf_ref[...]).astype(BF16), sf.astype(BF16))
             + _dot((q * qdb_ref[...]).astype(BF16), sb.astype(BF16)))
    lane_head = lax.broadcasted_iota(jnp.int32, (1, RET_HEADS * RET_DK), 1) // RET_DK
    outs = []
    for h in range(RET_HEADS):
        qh = jnp.where(lane_head == h, q, 0.0).astype(BF16)
        sc = _dot_nt(qh, kb) * dec_ref[h]
        y = _dot(sc.astype(BF16), vb[:, h * RET_DV:(h + 1) * RET_DV]) + cross[:, h * RET_DV:(h + 1) * RET_DV]
        mu = jnp.mean(y, axis=-1, keepdims=True)
        d = y - mu
        var = jnp.mean(d * d, axis=-1, keepdims=True)
        outs.append(d * lax.rsqrt(var + 1e-6))
    return jnp.concatenate(outs, axis=1) * _silu(rg)


def _ret_out_body(rq_ref, rk_ref, rqr_ref, rkr_ref, rv_ref, rg_ref, cos_ref, sin_ref, qdf_ref, qdb_ref, dec_ref,
                  sf_ref, sb_ref, o_ref):
    n = qdf_ref.shape[0]

    def body(k, carry):
        rows = pl.ds(pl.multiple_of(k * n, n), n)
        o_ref[0, rows, :] = _ret_out_chunk(rq_ref[0, rows, :], rk_ref[0, rows, :], rqr_ref[0, rows, :],
                                           rkr_ref[0, rows, :], rv_ref[0, rows, :], rg_ref[0, rows, :],
                                           cos_ref[rows, :], sin_ref[rows, :], sf_ref[0, k], sb_ref[0, k],
                                           qdf_ref, qdb_ref, dec_ref)
        return carry

    lax.fori_loop(0, o_ref.shape[1] // n, body, 0, unroll=2)


def _retention(proj3, cos4, sin4):
    bsz, s, _ = proj3.shape
    n = min(RET_CHUNK, s)
    c = s // n
    kdf, kdb, qdf, qdb, cdf, cdb, dec, bd = _ret_tables(n)
    c256 = COL_RQ // 256
    c512 = COL_RQ // 512
    const1 = lambda i: (0, 0)

    state_shape = jax.ShapeDtypeStruct((bsz, c, RET_HEADS * RET_DK, GROUP_WIDTH), F32)
    state_spec = pl.BlockSpec((1, c, RET_HEADS * RET_DK, GROUP_WIDTH), lambda i: (i, 0, 0, 0))
    sf, sb = pl.pallas_call(
        functools.partial(_ret_state_body, c),
        grid=(bsz,),
        in_specs=[pl.BlockSpec((1, s, 256), lambda i: (i, 0, c256 + 1)),
                  pl.BlockSpec((1, s, 256), lambda i: (i, 0, c256 + 3)),
                  pl.BlockSpec((1, s, 512), lambda i: (i, 0, c512 + 2)),
                  pl.BlockSpec((s, 256), const1), pl.BlockSpec((s, 256), const1),
                  pl.BlockSpec((n, 256), const1), pl.BlockSpec((n, 256), const1),
                  pl.BlockSpec((1, 512), const1), pl.BlockSpec((1, 512), const1),
                  pl.BlockSpec((256, 512), const1)],
        out_specs=[state_spec, state_spec],
        out_shape=[state_shape, state_shape],
        scratch_shapes=[pltpu.VMEM((RET_HEADS * RET_DK, GROUP_WIDTH), F32)],
        compiler_params=_cparams(("parallel",), 48),
    )(proj3, proj3, proj3, cos4, sin4, kdf, kdb, cdf, cdb, bd)

    k = min(RET_OUT_CHUNKS, c)
    const3 = lambda i, j: (0, 0, 0)
    const2 = lambda i, j: (0, 0)
    return pl.pallas_call(
        _ret_out_body,
        grid=(bsz, c // k),
        in_specs=[pl.BlockSpec((1, k * n, 256), lambda i, j: (i, j, c256)),
                  pl.BlockSpec((1, k * n, 256), lambda i, j: (i, j, c256 + 1)),
                  pl.BlockSpec((1, k * n, 256), lambda i, j: (i, j, c256 + 2)),
                  pl.BlockSpec((1, k * n, 256), lambda i, j: (i, j, c256 + 3)),
                  pl.BlockSpec((1, k * n, 512), lambda i, j: (i, j, c512 + 2)),
                  pl.BlockSpec((1, k * n, 512), lambda i, j: (i, j, c512 + 3)),
                  pl.BlockSpec((k * n, 256), lambda i, j: (j, 0)),
                  pl.BlockSpec((k * n, 256), lambda i, j: (j, 0)),
                  pl.BlockSpec((n, 256), const2), pl.BlockSpec((n, 256), const2),
                  pl.BlockSpec((RET_HEADS, n, n), const3),
                  pl.BlockSpec((1, k, 256, 512), lambda i, j: (i, j, 0, 0)),
                  pl.BlockSpec((1, k, 256, 512), lambda i, j: (i, j, 0, 0))],
        out_specs=pl.BlockSpec((1, k * n, GROUP_WIDTH), lambda i, j: (i, j, 0)),
        out_shape=jax.ShapeDtypeStruct((bsz, s, GROUP_WIDTH), F32),
        compiler_params=_cparams(("parallel", "parallel"), 32),
    )(proj3, proj3, proj3, proj3, proj3, proj3, cos4, sin4, qdf, qdb, dec, sf, sb)


def _ssd_decays(dt_rows, dtb_ref, a_ref, tri_ref):
    dt_raw = dt_rows + dtb_ref[...]
    dt = jnp.maximum(dt_raw, 0.0) + jnp.log1p(jnp.exp(-jnp.abs(dt_raw)))
    la = dt * a_ref[...]
    return dt, la, _dot_f32(tri_ref[...], la)


def _expand_heads(v, e):
    hi = v.astype(BF16)
    lo = (v - hi.astype(F32)).astype(BF16)
    return _dot(jnp.concatenate([hi, lo], axis=1), e)


def _ssd_state_body(nchunks, xbc_ref, dt_ref, dtb_ref, a_ref, tri_ref, e_ref, sf_ref, sb_ref, st_scr):
    n = SSD_CHUNK
    w = GROUP_WIDTH
    fwd_lane = lax.broadcasted_iota(jnp.int32, (1, LANES), 1) < SSD_HEADS

    def sweep(out_ref, lo, chunk_of):
        st_scr[...] = jnp.zeros_like(st_scr)

        def body(t, carry):
            c = chunk_of(t)
            rows = pl.ds(pl.multiple_of(c * n, n), n)
            dt, la, cs = _ssd_decays(dt_ref[0, rows, :], dtb_ref, a_ref, tri_ref)
            tot = cs[n - 1:n, :]
            wgt = dt * jnp.exp(jnp.where(fwd_lane, tot - cs, cs - la))
            both = _expand_heads(jnp.concatenate([wgt, jnp.broadcast_to(jnp.exp(tot), (SUBLANES, LANES))], axis=0),
                                 e_ref[:, lo:lo + w])
            out_ref[0, c] = st_scr[...]
            xw = (xbc_ref[0, rows, 0:w] * both[:n]).astype(BF16)
            new = []
            for g in range(SSD_GROUPS):
                bg = xbc_ref[0, rows, w + g * SSD_STATE:w + (g + 1) * SSD_STATE].astype(BF16)
                new.append(_dot_tn(bg, xw[:, g * 256:(g + 1) * 256]))
            st_scr[...] = st_scr[...] * both[n:n + 1] + jnp.concatenate(new, axis=1)
            return carry

        lax.fori_loop(0, nchunks, body, 0, unroll=4)

    sweep(sf_ref, 0, lambda t: t)
    sweep(sb_ref, w, lambda t: nchunks - 1 - t)


def _ssd_out_chunk(xbc, dt_rows, sf, sb, dtb_ref, a_ref, tri_ref, e_ref, dskip_ref):
    n = xbc.shape[0]
    w = GROUP_WIDTH
    dt, la, cs = _ssd_decays(dt_rows, dtb_ref, a_ref, tri_ref)
    ecs = cs - la
    fwd_lane = lax.broadcasted_iota(jnp.int32, (1, LANES), 1) < SSD_HEADS
    carry = jnp.exp(jnp.where(fwd_lane, cs, cs[n - 1:n, :] - ecs))
    both = _expand_heads(jnp.concatenate([dt, carry], axis=0), e_ref[...])
    dt_e = both[:n]
    carry_e = both[n:]
    xs = xbc[:, :w]
    xdt_f = xs * dt_e[:, :w]
    xdt_b = xs * dt_e[:, w:]
    cs_t = cs.T
    ecs_t = ecs.T
    ii = lax.broadcasted_iota(jnp.int32, (n, n), 0)
    jj = lax.broadcasted_iota(jnp.int32, (n, n), 1)
    low = lax.broadcasted_iota(jnp.int32, (1, LANES), 1) < SSD_HEADDIM
    neg = -1e30
    ydiag = []
    yoff = []
    for g in range(SSD_GROUPS):
        bg = xbc[:, w + g * SSD_STATE:w + (g + 1) * SSD_STATE].astype(BF16)
        cg = xbc[:, w + 256 + g * SSD_STATE:w + 256 + (g + 1) * SSD_STATE].astype(BF16)
        cb = _dot_nt(cg, bg)
        for pair in range(2):
            p = 2 * g + pair
            lhs = []
            for h in (2 * p, 2 * p + 1):
                lf = jnp.exp(jnp.where(ii >= jj, cs[:, h:h + 1] - cs_t[h:h + 1, :], neg))
                lhs.append((cb * lf).astype(BF16))
            for h in (2 * p, 2 * p + 1):
                hb = SSD_HEADS + h
                lb = jnp.exp(jnp.where(jj > ii, ecs_t[hb:hb + 1, :] - ecs[:, hb:hb + 1], neg))
                lhs.append((cb * lb).astype(BF16))
            xf = xdt_f[:, p * LANES:(p + 1) * LANES]
            xb = xdt_b[:, p * LANES:(p + 1) * LANES]
            rhs = jnp.concatenate([jnp.where(low, xf, 0.0), jnp.where(low, 0.0, xf),
                                   jnp.where(low, xb, 0.0), jnp.where(low, 0.0, xb)], axis=0).astype(BF16)
            ydiag.append(_dot(jnp.concatenate(lhs, axis=1), rhs))
        yoff.append(_dot(cg, sf[:, g * 256:(g + 1) * 256].astype(BF16)) * carry_e[:, g * 256:(g + 1) * 256]
                    + _dot(cg, sb[:, g * 256:(g + 1) * 256].astype(BF16)) * carry_e[:, w + g * 256:w + (g + 1) * 256])
    return jnp.concatenate(ydiag, axis=1) + jnp.concatenate(yoff, axis=1) + xs * dskip_ref[...]


def _ssd_out_body(xbc_ref, dt_ref, z_ref, dtb_ref, a_ref, tri_ref, e_ref, dskip_ref, nw_ref, sf_ref, sb_ref, o_ref):
    n = SSD_CHUNK

    def body(k, carry):
        rows = pl.ds(pl.multiple_of(k * n, n), n)
        y = _ssd_out_chunk(xbc_ref[0, rows, :], dt_ref[0, rows, :], sf_ref[0, k], sb_ref[0, k], dtb_ref, a_ref, tri_ref,
                           e_ref, dskip_ref)
        o_ref[0, rows, :] = _rms(y * _silu(z_ref[0, rows, :]), nw_ref[...])
        return carry

    lax.fori_loop(0, o_ref.shape[1] // n, body, 0, unroll=2)


def _ssd(proj3, xbc_act, dt_bias, a_log, d_skip, norm_w):
    bsz, s, _ = proj3.shape
    n = SSD_CHUNK
    c = s // n
    tri, e = _ssd_tables(n)
    pad = jnp.zeros((LANES - 2 * SSD_HEADS,), F32)
    dtb = jnp.concatenate([dt_bias.reshape(-1), pad]).reshape(1, LANES)
    a = jnp.concatenate([-jnp.exp(a_log.reshape(-1)), pad]).reshape(1, LANES)
    dskip = jnp.repeat(d_skip, SSD_HEADDIM).reshape(1, GROUP_WIDTH)
    cdt = COL_DT // LANES
    const1 = lambda i: (0, 0)

    state_shape = jax.ShapeDtypeStruct((bsz, c, SSD_STATE, GROUP_WIDTH), F32)
    state_spec = pl.BlockSpec((1, c, SSD_STATE, GROUP_WIDTH), lambda i: (i, 0, 0, 0))
    sf, sb = pl.pallas_call(
        functools.partial(_ssd_state_body, c),
        grid=(bsz,),
        in_specs=[pl.BlockSpec((1, s, SSD_CONV_CH), lambda i: (i, 0, 0)),
                  pl.BlockSpec((1, s, LANES), lambda i: (i, 0, cdt)),
                  pl.BlockSpec((1, LANES), const1), pl.BlockSpec((1, LANES), const1),
                  pl.BlockSpec((n, n), const1), pl.BlockSpec((2 * LANES, 2 * GROUP_WIDTH), const1)],
        out_specs=[state_spec, state_spec],
        out_shape=[state_shape, state_shape],
        scratch_shapes=[pltpu.VMEM((SSD_STATE, GROUP_WIDTH), F32)],
        compiler_params=_cparams(("parallel",), 48),
    )(xbc_act, proj3, dtb, a, tri, e)

    k = min(SSD_OUT_CHUNKS, c)
    const2 = lambda i, j: (0, 0)
    return pl.pallas_call(
        _ssd_out_body,
        grid=(bsz, c // k),
        in_specs=[pl.BlockSpec((1, k * n, SSD_CONV_CH), lambda i, j: (i, j, 0)),
                  pl.BlockSpec((1, k * n, LANES), lambda i, j: (i, j, cdt)),
                  pl.BlockSpec((1, k * n, GROUP_WIDTH), lambda i, j: (i, j, COL_Z // GROUP_WIDTH)),
                  pl.BlockSpec((1, LANES), const2), pl.BlockSpec((1, LANES), const2),
                  pl.BlockSpec((n, n), const2), pl.BlockSpec((2 * LANES, 2 * GROUP_WIDTH), const2),
                  pl.BlockSpec((1, GROUP_WIDTH), const2), pl.BlockSpec((1, GROUP_WIDTH), const2),
                  pl.BlockSpec((1, k, SSD_STATE, GROUP_WIDTH), lambda i, j: (i, j, 0, 0)),
                  pl.BlockSpec((1, k, SSD_STATE, GROUP_WIDTH), lambda i, j: (i, j, 0, 0))],
        out_specs=pl.BlockSpec((1, k * n, GROUP_WIDTH), lambda i, j: (i, j, 0)),
        out_shape=jax.ShapeDtypeStruct((bsz, s, GROUP_WIDTH), F32),
        compiler_params=_cparams(("parallel", "parallel"), 32),
    )(xbc_act, proj3, proj3, dtb, a, tri, e, dskip, norm_w.reshape(1, -1), sf, sb)


def _hy_filter_body(z_ref, w1_ref, b1_ref, w2_ref, b2_ref, w3_ref, fr_ref, dl_ref, hs_ref, hd_ref):
    tl = z_ref.shape[0]
    z = z_ref[...]
    hid = jnp.sin(fr_ref[0:1, :] * (_dot_f32(z, w1_ref[...]) + b1_ref[...]))
    hid = jnp.sin(fr_ref[1:2, :] * (_dot_f32(hid, w2_ref[...]) + b2_ref[...]))
    filt = _dot_f32(hid, w3_ref[...])
    dec = jnp.exp(-z[:, 0:1] * dl_ref[...])
    row = pl.program_id(0) * tl + lax.broadcasted_iota(jnp.int32, (tl, 1), 0)
    for o in range(HY_ORDER):
        base = o * 2 * HY_WIDTH
        hf = filt[:, base:base + HY_WIDTH] * dec
        hb = jnp.where(row == 0, 0.0, filt[:, base + HY_WIDTH:base + 2 * HY_WIDTH] * dec)
        hs_ref[:, o * HY_WIDTH:(o + 1) * HY_WIDTH] = hf + hb
        hd_ref[:, o * HY_WIDTH:(o + 1) * HY_WIDTH] = hf - hb


def _hy_filters(l, w1, b1, w2, b2, w3, freq):
    z, deltas = _hyena_positions(l)
    hid = HY_FILTER_HIDDEN
    w1p = jnp.zeros((LANES, LANES), F32).at[:HY_EMB, :hid].set(w1)
    w2p = jnp.zeros((LANES, LANES), F32).at[:hid, :hid].set(w2)
    w3p = jnp.zeros((LANES, w3.shape[1]), F32).at[:hid].set(w3)
    b1p = jnp.zeros((1, LANES), F32).at[0, :hid].set(b1)
    b2p = jnp.zeros((1, LANES), F32).at[0, :hid].set(b2)
    frp = jnp.zeros((2, LANES), F32).at[:, :hid].set(freq)
    tl = min(256, l)
    ncol = HY_ORDER * HY_WIDTH
    const = lambda i: (0, 0)
    return pl.pallas_call(
        _hy_filter_body,
        grid=(l // tl,),
        in_specs=[pl.BlockSpec((tl, LANES), lambda i: (i, 0)),
                  pl.BlockSpec((LANES, LANES), const), pl.BlockSpec((1, LANES), const),
                  pl.BlockSpec((LANES, LANES), const), pl.BlockSpec((1, LANES), const),
                  pl.BlockSpec((LANES, w3.shape[1]), const), pl.BlockSpec((2, LANES), const),
                  pl.BlockSpec((1, HY_WIDTH), const)],
        out_specs=[pl.BlockSpec((tl, ncol), lambda i: (i, 0)), pl.BlockSpec((tl, ncol), lambda i: (i, 0))],
        out_shape=[jax.ShapeDtypeStruct((l, ncol), F32), jax.ShapeDtypeStruct((l, ncol), F32)],
        compiler_params=_cparams(("parallel",), 32),
    )(z, w1p, b1p, w2p, b2p, w3p, frp, deltas)


def _short_conv_rows(x_ref, w_ref, b_ref, r0, n):
    l = x_ref.shape[1]
    lo = max(r0 - SUBLANES, 0)
    hi = min(r0 + n + SUBLANES, l)
    win = x_ref[0, lo:hi, :]
    pad = jnp.zeros((SUBLANES, win.shape[1]), F32)
    if r0 == 0:
        win = jnp.concatenate([pad, win], axis=0)
    if r0 + n == l:
        win = jnp.concatenate([win, pad], axis=0)
    rows = n + 2 * SUBLANES
    prev = pltpu.roll(win, 1, axis=0)[SUBLANES:SUBLANES + n]
    nxt = pltpu.roll(win, rows - 1, axis=0)[SUBLANES:SUBLANES + n]
    return prev * w_ref[0:1, :] + win[SUBLANES:SUBLANES + n] * w_ref[1:2, :] + nxt * w_ref[2:3, :] + b_ref[...]


def _hy_conv_body(u0_ref, u1_ref, u2_ref, cw0_ref, cw1_ref, cw2_ref, cb0_ref, cb1_ref, cb2_ref, c_ref, s_ref,
                  hr0_ref, hi0_ref, hr1_ref, hi1_ref, b0_ref, b1_ref, o_ref, xb_scr, pre_scr, pim_scr, z_scr):
    l = u0_ref.shape[1]
    tf = min(HY_ROW_TILE, l)
    ntile = l // tf
    sgn = jnp.where(lax.broadcasted_iota(jnp.int32, (l, 1), 0) % 2 == 0, 1.0, -1.0)

    def long_conv(gate_ref, gw_ref, gb_ref, hr_ref, hi_ref, b_ref, out_ref):
        x = z_scr[...]
        xb_scr[...] = x.astype(BF16)
        x_nyq = jnp.sum(x * sgn, axis=0, keepdims=True) * hr_ref[l:l + 1, :]
        for r in range(ntile):
            rows = pl.ds(r * tf, tf)
            xc = _dot(c_ref[rows, :], xb_scr[...])
            xs = _dot(s_ref[rows, :], xb_scr[...])
            hre = hr_ref[rows, :]
            him = hi_ref[rows, :]
            pre_scr[rows, :] = (xc * hre + xs * him).astype(BF16)
            pim_scr[rows, :] = (xc * him - xs * hre).astype(BF16)
        for r in range(ntile):
            rows = pl.ds(r * tf, tf)
            y = _dot(c_ref[rows, :], pre_scr[...]) - _dot(s_ref[rows, :], pim_scr[...])
            y = y + sgn[r * tf:(r + 1) * tf] * x_nyq
            gate = _short_conv_rows(gate_ref, gw_ref, gb_ref, r * tf, tf)
            out_ref[rows, :] = gate * (y + z_scr[rows, :] * b_ref[0])

    for r in range(ntile):
        z_scr[pl.ds(r * tf, tf), :] = _short_conv_rows(u0_ref, cw0_ref, cb0_ref, r * tf, tf)
    long_conv(u1_ref, cw1_ref, cb1_ref, hr0_ref, hi0_ref, b0_ref, z_scr)
    long_conv(u2_ref, cw2_ref, cb2_ref, hr1_ref, hi1_ref, b1_ref, o_ref.at[0])


def _hy_long_conv(proj3, conv_w, conv_b, cmat, smat, hre, him, bias):
    bsz, l, _ = proj3.shape
    tc = 256
    nb = HY_WIDTH // tc
    off = COL_HY // tc
    const = lambda j, i: (0, 0)
    single = pl.Buffered(1)
    bias3 = bias.reshape(HY_ORDER, 1, HY_WIDTH)
    conv_b2 = conv_b.reshape(1, -1)

    def part(k):
        return [pl.BlockSpec((1, l, tc), lambda j, i: (i, 0, off + k * nb + j))]

    def part_w(k):
        return [pl.BlockSpec((HY_SHORT, tc), lambda j, i: (0, k * nb + j))]

    def part_b(k):
        return [pl.BlockSpec((1, tc), lambda j, i: (0, k * nb + j))]

    return pl.pallas_call(
        _hy_conv_body,
        grid=(nb, bsz),
        in_specs=part(0) + part(1) + part(2) + part_w(0) + part_w(1) + part_w(2) + part_b(0) + part_b(1) + part_b(2) + [
            pl.BlockSpec((l, l), const, pipeline_mode=single),
            pl.BlockSpec((l, l), const, pipeline_mode=single),
            pl.BlockSpec((l + SUBLANES, tc), lambda j, i: (0, j), pipeline_mode=single),
            pl.BlockSpec((l, tc), lambda j, i: (0, j), pipeline_mode=single),
            pl.BlockSpec((l + SUBLANES, tc), lambda j, i: (0, nb + j), pipeline_mode=single),
            pl.BlockSpec((l, tc), lambda j, i: (0, nb + j), pipeline_mode=single),
            pl.BlockSpec((1, 1, tc), lambda j, i: (0, 0, j)),
            pl.BlockSpec((1, 1, tc), lambda j, i: (1, 0, j))],
        out_specs=pl.BlockSpec((1, l, tc), lambda j, i: (i, 0, j)),
        out_shape=jax.ShapeDtypeStruct((bsz, l, HY_WIDTH), F32),
        scratch_shapes=[pltpu.VMEM((l, tc), BF16), pltpu.VMEM((l, tc), BF16), pltpu.VMEM((l, tc), BF16),
                        pltpu.VMEM((l, tc), F32)],
        compiler_params=_cparams(("parallel", "parallel"), 56),
    )(proj3, proj3, proj3, conv_w, conv_w, conv_w, conv_b2, conv_b2, conv_b2, cmat, smat, hre, him, hre, him,
      bias3, bias3)


def _hyena(proj3, conv_w, conv_b, w1, b1, w2, b2, w3, freq, bias):
    l = proj3.shape[1]
    hs, hd = _hy_filters(l, w1, b1, w2, b2, w3, freq)
    cmat, smat, c_sc, s_sc_neg = _dft_tables(l)
    tn = 256
    hre = _matmul(c_sc, hs.astype(BF16), l + SUBLANES, tn)
    him = _matmul(s_sc_neg, hd.astype(BF16), l, tn)
    return _hy_long_conv(proj3, conv_w, conv_b, cmat, smat, hre, him, bias)


def _out_proj_body(a_ref, b_ref, c_ref, d_ref, x_ref, w_ref, dn_ref, g_ref, beta_ref, o_ref, *rows_ref):
    gw = GROUP_WIDTH
    h = _dot(a_ref[...].astype(BF16), w_ref[0:gw, :])
    h += _dot(b_ref[...].astype(BF16), w_ref[gw:2 * gw, :])
    h += _dot(c_ref[...].astype(BF16), w_ref[2 * gw:3 * gw, :])
    h += _dot(_rms(d_ref[...], dn_ref[...]).astype(BF16), w_ref[3 * gw:4 * gw, :])
    y = _layer_norm(ALPHA * x_ref[...] + h, g_ref[...], beta_ref[...])
    o_ref[...] = y
    if rows_ref:
        _store_row_tiles(rows_ref[0], y)


def _store_row_tiles(dst_ref, y):
    rows = y.shape[0]
    for c in range(ROW_CHUNKS):
        dst_ref[pl.ds(c, rows, stride=ROW_CHUNKS), :] = y[:, c * LANES:(c + 1) * LANES]


def _load_row_tiles(src_ref, first, rows):
    return jnp.concatenate([src_ref[pl.ds(first * ROW_CHUNKS + c, rows, stride=ROW_CHUNKS), :]
                            for c in range(ROW_CHUNKS)], axis=1)


def _out_proj_ln(a, b, c, d, x, w_out, hy_norm, g, beta, emit_row_tiles):
    m = x.shape[0]
    tm = min(512, m)
    gw = GROUP_WIDTH
    const = lambda i: (0, 0)
    row = lambda i: (i, 0)
    out_specs = [pl.BlockSpec((tm, D_MODEL), row)]
    out_shape = [jax.ShapeDtypeStruct((m, D_MODEL), F32)]
    if emit_row_tiles:
        out_specs.append(pl.BlockSpec((tm * ROW_CHUNKS, LANES), row))
        out_shape.append(jax.ShapeDtypeStruct((m * ROW_CHUNKS, LANES), F32))
    return pl.pallas_call(
        _out_proj_body,
        grid=(m // tm,),
        in_specs=[pl.BlockSpec((tm, gw), row), pl.BlockSpec((tm, gw), row), pl.BlockSpec((tm, gw), row),
                  pl.BlockSpec((tm, gw), row), pl.BlockSpec((tm, D_MODEL), row),
                  pl.BlockSpec((D_MODEL, D_MODEL), const), pl.BlockSpec((1, gw), const),
                  pl.BlockSpec((1, D_MODEL), const), pl.BlockSpec((1, D_MODEL), const)],
        out_specs=out_specs,
        out_shape=out_shape,
        compiler_params=_cparams(("parallel",), 48),
    )(a, b, c, d, x, w_out, hy_norm.reshape(1, -1), g.reshape(1, -1), beta.reshape(1, -1))


def _ffn_body(x_ref, wg_ref, wu_ref, wd_ref, g_ref, beta_ref, o_ref, xb_ref, acc_ref):
    j = pl.program_id(1)

    @pl.when(j == 0)
    def _():
        xb_ref[...] = x_ref[...].astype(BF16)
        acc_ref[...] = jnp.zeros_like(acc_ref)

    xb = xb_ref[...]
    hidden = (_silu(_dot(xb, wg_ref[...])) * _dot(xb, wu_ref[...])).astype(BF16)
    acc_ref[...] += _dot(hidden, wd_ref[...])

    @pl.when(j == pl.num_programs(1) - 1)
    def _():
        o_ref[...] = _layer_norm(ALPHA * x_ref[...] + acc_ref[...], g_ref[...], beta_ref[...])


def _ffn_ln(x, wg, wu, wd, g, beta):
    m = x.shape[0]
    dff = wg.shape[1]
    tm = min(512, m)
    tf = 512
    const = lambda i, j: (0, 0)
    return pl.pallas_call(
        _ffn_body,
        grid=(m // tm, dff // tf),
        in_specs=[pl.BlockSpec((tm, D_MODEL), lambda i, j: (i, 0)),
                  pl.BlockSpec((D_MODEL, tf), lambda i, j: (0, j)),
                  pl.BlockSpec((D_MODEL, tf), lambda i, j: (0, j)),
                  pl.BlockSpec((tf, D_MODEL), lambda i, j: (j, 0)),
                  pl.BlockSpec((1, D_MODEL), const), pl.BlockSpec((1, D_MODEL), const)],
        out_specs=pl.BlockSpec((tm, D_MODEL), lambda i, j: (i, 0)),
        out_shape=jax.ShapeDtypeStruct((m, D_MODEL), F32),
        scratch_shapes=[pltpu.VMEM((tm, D_MODEL), BF16), pltpu.VMEM((tm, D_MODEL), F32)],
        compiler_params=_cparams(("parallel", "arbitrary"), 48),
    )(x, wg, wu, wd, g.reshape(1, -1), beta.reshape(1, -1))


def _router_body(x_ref, w_ref, idx_ref, gate_ref):
    logits = _dot_f32(x_ref[...], w_ref[...])
    lane = lax.broadcasted_iota(jnp.int32, logits.shape, 1)
    neg = -jnp.inf
    logits = jnp.where(lane < N_EXPERTS, logits, neg)
    m1 = jnp.max(logits, axis=-1, keepdims=True)
    i1 = jnp.min(jnp.where(logits == m1, lane, LANES), axis=-1, keepdims=True)
    rest = jnp.where(lane == i1, neg, logits)
    m2 = jnp.max(rest, axis=-1, keepdims=True)
    i2 = jnp.min(jnp.where(rest == m2, lane, LANES), axis=-1, keepdims=True)
    e2 = jnp.exp(m2 - m1)
    den = 1.0 + e2
    idx_ref[...] = jnp.where(lane == 0, i1, jnp.where(lane == 1, i2, 0))
    gate_ref[...] = jnp.where(lane == 0, 1.0 / den, jnp.where(lane == 1, e2 / den, 0.0))


def _router(x, w_router):
    m = x.shape[0]
    tm = min(512, m)
    wp = jnp.zeros((D_MODEL, LANES), F32).at[:, :N_EXPERTS].set(w_router)
    return pl.pallas_call(
        _router_body,
        grid=(m // tm,),
        in_specs=[pl.BlockSpec((tm, D_MODEL), lambda i: (i, 0)), pl.BlockSpec((D_MODEL, LANES), lambda i: (0, 0))],
        out_specs=[pl.BlockSpec((tm, LANES), lambda i: (i, 0)), pl.BlockSpec((tm, LANES), lambda i: (i, 0))],
        out_shape=[jax.ShapeDtypeStruct((m, LANES), jnp.int32), jax.ShapeDtypeStruct((m, LANES), F32)],
        compiler_params=_cparams(("parallel",), 32),
    )(x, wp)


def _row_copy(src_ref, src_row, buf_ref, buf_row, sem):
    src = src_ref.at[pl.ds(pl.multiple_of(src_row * ROW_CHUNKS, ROW_CHUNKS), ROW_CHUNKS)]
    dst = buf_ref.at[pl.ds(pl.multiple_of(buf_row * ROW_CHUNKS, ROW_CHUNKS), ROW_CHUNKS)]
    return pltpu.make_async_copy(src, dst, sem)


def _start_row_gather(idx_ref, n, src_ref, buf_ref, sem):
    def start(r, carry):
        _row_copy(src_ref, idx_ref[0, 0, r], buf_ref, r, sem).start()
        return carry

    lax.fori_loop(0, n, start, 0, unroll=DMA_LOOP_UNROLL)


def _wait_row_gather(n, src_ref, buf_ref, sem):
    def wait(r, carry):
        _row_copy(src_ref, 0, buf_ref, r, sem).wait()
        return carry

    lax.fori_loop(0, n, wait, 0, unroll=DMA_LOOP_UNROLL)


def _pipelined_row_gather(idx_cur_ref, idx_next_ref, n, src_ref, buf, sems):
    i = pl.program_id(0)
    slot = i % 2

    @pl.when(i == 0)
    def _():
        _start_row_gather(idx_cur_ref, n, src_ref, buf.at[0], sems.at[0])

    @pl.when(i + 1 < pl.num_programs(0))
    def _():
        _start_row_gather(idx_next_ref, n, src_ref, buf.at[1 - slot], sems.at[1 - slot])

    _wait_row_gather(n, src_ref, buf.at[slot], sems.at[slot])
    return slot


def _idx_specs(n, nblk):
    return [pl.BlockSpec((1, 1, n), lambda i: (i, 0, 0), memory_space=pltpu.SMEM),
            pl.BlockSpec((1, 1, n), lambda i: (jnp.minimum(i + 1, nblk - 1), 0, 0), memory_space=pltpu.SMEM)]


def _gather_x_body(idx_cur_ref, idx_next_ref, src_ref, o_ref, buf, sems):
    rows = o_ref.shape[0]
    slot = _pipelined_row_gather(idx_cur_ref, idx_next_ref, rows, src_ref, buf, sems)
    o_ref[...] = _load_row_tiles(buf.at[slot], 0, rows).astype(BF16)


def _gather_x(x_tiles, slot_tok):
    cap = slot_tok.shape[0]
    rows = GATHER_ROWS
    assert cap % rows == 0
    nblk = cap // rows
    idx = slot_tok.reshape(nblk, 1, rows)
    return pl.pallas_call(
        _gather_x_body,
        grid=(nblk,),
        in_specs=_idx_specs(rows, nblk) + [pl.BlockSpec(memory_space=pl.ANY)],
        out_specs=pl.BlockSpec((rows, D_MODEL), lambda i: (i, 0)),
        out_shape=jax.ShapeDtypeStruct((cap, D_MODEL), BF16),
        scratch_shapes=[pltpu.VMEM((2, rows * ROW_CHUNKS, LANES), F32), pltpu.SemaphoreType.DMA((2,))],
        compiler_params=pltpu.CompilerParams(dimension_semantics=("arbitrary",), disable_bounds_checks=True),
    )(idx, idx, x_tiles)


def _expert_body(be_ref, nu_ref, x_ref, wg_ref, wu_ref, wd_ref, o_ref, acc_ref):
    i = pl.program_id(0)
    j = pl.program_id(1)

    @pl.when(j == 0)
    def _():
        acc_ref[...] = jnp.zeros_like(acc_ref)

    @pl.when(i < nu_ref[0])
    def _():
        xb = x_ref[...]
        hidden = (_silu(_dot(xb, wg_ref[0])) * _dot(xb, wu_ref[0])).astype(BF16)
        acc_ref[...] += _dot(hidden, wd_ref[0])

    @pl.when(j == pl.num_programs(1) - 1)
    def _():
        _store_row_tiles(o_ref, acc_ref[...])


def _expert_ffn(x_slots, block_expert, n_used, wg, wu, wd, tm):
    cap = x_slots.shape[0]
    dff = wg.shape[2]
    tf = 512

    def ff_tile(i, j, nu):
        return jnp.where(i < nu[0], j, 0)

    grid_spec = pltpu.PrefetchScalarGridSpec(
        num_scalar_prefetch=2,
        grid=(cap // tm, dff // tf),
        in_specs=[pl.BlockSpec((tm, D_MODEL), lambda i, j, be, nu: (i, 0)),
                  pl.BlockSpec((1, D_MODEL, tf), lambda i, j, be, nu: (be[i], 0, ff_tile(i, j, nu))),
                  pl.BlockSpec((1, D_MODEL, tf), lambda i, j, be, nu: (be[i], 0, ff_tile(i, j, nu))),
                  pl.BlockSpec((1, tf, D_MODEL), lambda i, j, be, nu: (be[i], ff_tile(i, j, nu), 0))],
        out_specs=pl.BlockSpec((tm * ROW_CHUNKS, LANES), lambda i, j, be, nu: (i, 0)),
        scratch_shapes=[pltpu.VMEM((tm, D_MODEL), F32)],
    )
    return pl.pallas_call(
        _expert_body,
        grid_spec=grid_spec,
        out_shape=jax.ShapeDtypeStruct((cap * ROW_CHUNKS, LANES), F32),
        compiler_params=_cparams(("parallel", "arbitrary"), 56),
    )(block_expert, n_used, x_slots, wg, wu, wd)


def _combine_body(idx_cur_ref, idx_next_ref, x_ref, gt_ref, g_ref, beta_ref, y_ref, o_ref, buf, sems):
    rows = x_ref.shape[0]
    slot = _pipelined_row_gather(idx_cur_ref, idx_next_ref, TOP_K * rows, y_ref, buf, sems)
    gt = gt_ref[...]
    f = (gt[:, 0:1] * _load_row_tiles(buf.at[slot], 0, rows)
         + gt[:, 1:2] * _load_row_tiles(buf.at[slot], rows, rows))
    o_ref[...] = _layer_norm(ALPHA * x_ref[...] + f, g_ref[...], beta_ref[...])


def _combine_ln(x, y_tiles, pos, gates, g, beta):
    m = x.shape[0]
    rows = min(COMBINE_ROWS, m)
    nblk = m // rows
    n = TOP_K * rows
    idx = pos.reshape(nblk, rows, TOP_K).transpose(0, 2, 1).reshape(nblk, 1, n)
    const = lambda i: (0, 0)
    return pl.pallas_call(
        _combine_body,
        grid=(nblk,),
        in_specs=_idx_specs(n, nblk) + [
            pl.BlockSpec((rows, D_MODEL), lambda i: (i, 0)),
            pl.BlockSpec((rows, LANES), lambda i: (i, 0)),
            pl.BlockSpec((1, D_MODEL), const), pl.BlockSpec((1, D_MODEL), const),
            pl.BlockSpec(memory_space=pl.ANY)],
        out_specs=pl.BlockSpec((rows, D_MODEL), lambda i: (i, 0)),
        out_shape=jax.ShapeDtypeStruct((m, D_MODEL), F32),
        scratch_shapes=[pltpu.VMEM((2, n * ROW_CHUNKS, LANES), F32), pltpu.SemaphoreType.DMA((2,))],
        compiler_params=pltpu.CompilerParams(dimension_semantics=("arbitrary",), disable_bounds_checks=True,
                                             vmem_limit_bytes=48 * 1024 * 1024),
    )(idx, idx, x, gates, g.reshape(1, -1), beta.reshape(1, -1), y_tiles)


def _moe_ln(x, x_tiles, w_router, wg, wu, wd, g, beta, tm):
    m = x.shape[0]
    n_asg = m * TOP_K
    idx, gates = _router(x, w_router)
    e_flat = idx[:, :TOP_K].reshape(-1)
    onehot = (e_flat[:, None] == jnp.arange(N_EXPERTS, dtype=jnp.int32)[None, :]).astype(jnp.int32)
    csum = jnp.cumsum(onehot, axis=0)
    counts = csum[-1]
    rank = jnp.sum(csum * onehot, axis=1) - 1
    padded = (counts + tm - 1) // tm * tm
    pend = jnp.cumsum(padded)
    pstart = pend - padded
    dest = (jnp.sum(pstart[None, :] * onehot, axis=1) + rank).astype(jnp.int32)
    cap = n_asg + N_EXPERTS * tm
    tok = jnp.arange(n_asg, dtype=jnp.int32) // TOP_K
    slot_tok = jnp.zeros((cap,), jnp.int32).at[dest].set(tok)
    nblk = cap // tm
    block_expert = jnp.minimum(jnp.searchsorted(pend, jnp.arange(nblk, dtype=pend.dtype) * tm, side='right'),
                               N_EXPERTS - 1).astype(jnp.int32)
    n_used = (pend[-1] // tm).astype(jnp.int32).reshape(1)

    x_slots = _gather_x(x_tiles, slot_tok)
    y_tiles = _expert_ffn(x_slots, block_expert, n_used, wg, wu, wd, tm)
    return _combine_ln(x, y_tiles, dest.reshape(m, TOP_K), gates, g, beta)


def _mixer_ln(x2, bsz, s, p, emit_row_tiles):
    proj = _matmul(x2, p['w_in'], min(1024, x2.shape[0]), PROJ_TILE)
    proj3 = proj.reshape(bsz, s, PROJ_COLS)
    cos_slab, sin_slab, cos4, sin4 = _rope_tables(s)
    out_a = _mla(proj3, cos_slab, sin_slab, p['mla_q_norm'], p['w_q'], p['w_qr'], p['mla_kv_norm'], p['w_kv'],
                 p['mla_out_norm'])
    out_b = _retention(proj3, cos4, sin4)
    xbc_act = _dwconv(proj3, COL_XBC, SSD_CONV_CH, p['ssd_conv_w'], p['ssd_conv_b'], act=True)
    out_c = _ssd(proj3, xbc_act, p['ssd_dt_bias'], p['ssd_a_log'], p['ssd_d'], p['ssd_norm'])
    out_d = _hyena(proj3, p['hy_conv_w'], p['hy_conv_b'], p['hy_w1'], p['hy_b1'], p['hy_w2'], p['hy_b2'], p['hy_w3'],
                   p['hy_freq'], p['hy_bias'])
    m = bsz * s
    gw = GROUP_WIDTH
    return _out_proj_ln(out_a.reshape(m, gw), out_b.reshape(m, gw), out_c.reshape(m, gw), out_d.reshape(m, gw),
                        x2, p['w_out'], p['hy_out_norm'], p['ln1_g'], p['ln1_b'], emit_row_tiles)


_MIXER_KEYS = ('mla_q_norm', 'mla_kv_norm', 'mla_out_norm', 'ssd_conv_w', 'ssd_conv_b', 'ssd_dt_bias', 'ssd_a_log',
               'ssd_d', 'ssd_norm', 'hy_conv_w', 'hy_conv_b', 'hy_w1', 'hy_b1', 'hy_w2', 'hy_b2', 'hy_w3', 'hy_freq',
               'hy_bias', 'hy_out_norm', 'ln1_g', 'ln1_b')

MOE_TM = 1024


def kernel(x, w_in, mla_q_norm, mla_w_uq, mla_kv_norm, mla_w_ukv, mla_out_norm, ssd_conv_w, ssd_conv_b, ssd_dt_bias, ssd_a_log, ssd_d, ssd_norm, hy_conv_w, hy_conv_b, hy_w1, hy_b1, hy_w2, hy_b2, hy_w3, hy_freq, hy_bias, hy_out_norm, w_out, ln1_g, ln1_b, ln2_g, ln2_b, ffn_w_gate, ffn_w_up, ffn_w_down, moe_router, moe_w_gate, moe_w_up, moe_w_down):
    args = dict(locals())
    bsz, s, d = x.shape
    x2 = x.reshape(bsz * s, d)
    for layer in range(DEPTH):
        p = {k: args[k][layer] for k in _MIXER_KEYS}
        p['w_in'] = _prep_w_in(w_in[layer])
        p['w_q'], p['w_qr'] = _prep_w_uq(mla_w_uq[layer])
        p['w_kv'] = _prep_w_ukv(mla_w_ukv[layer])
        p['w_out'] = w_out[layer].astype(BF16)
        j = layer // 2
        if layer % 2 == 0:
            x2, = _mixer_ln(x2, bsz, s, p, False)
            x2 = _ffn_ln(x2, ffn_w_gate[j].astype(BF16), ffn_w_up[j].astype(BF16), ffn_w_down[j].astype(BF16),
                         ln2_g[layer], ln2_b[layer])
        else:
            x2, x_tiles = _mixer_ln(x2, bsz, s, p, True)
            x2 = _moe_ln(x2, x_tiles, moe_router[j], moe_w_gate[j].astype(BF16), moe_w_up[j].astype(BF16),
                         moe_w_down[j].astype(BF16), ln2_g[layer], ln2_b[layer], MOE_TM)
    return x2.reshape(bsz, s, d)
```

```python
import functools
import math

import numpy as np
import jax
import jax.numpy as jnp
from jax import lax
from jax.experimental import pallas as pl
from jax.experimental.pallas import tpu as pltpu

F32 = jnp.float32
BF16 = jnp.bfloat16
HIGHEST = lax.Precision.HIGHEST

D_MODEL = 2048
DEPTH = 2
GROUP_WIDTH = 512
MLA_HEADS = 4
MLA_NOPE = 128
MLA_ROPE = 64
MLA_V = 128
MLA_Q_LORA = 384
MLA_KV_LORA = 256
RET_HEADS = 4
RET_DV = 128
RET_DK = 64
RET_DECAY_EXP_FWD = 5.0
RET_DECAY_EXP_BWD = 5.5
SSD_HEADDIM = 64
SSD_HEADS = 8
SSD_GROUPS = 2
SSD_STATE = 128
SSD_CONV = 5
SSD_CONV_CH = 1024
HY_ORDER = 2
HY_WIDTH = 512
HY_SHORT = 3
HY_EMB = 33
HY_FILTER_HIDDEN = 64
HY_MIN_DECAY = math.log(1e-2) / 1.5
HY_MAX_DECAY = math.log(1e-2) / 0.3
N_EXPERTS = 8
TOP_K = 2
ROPE_BASE = 10000.0
ALPHA = (2 * DEPTH) ** 0.25

V7X_VMEM_BYTES = 64 * 1024 * 1024
LANES = 128
SUBLANES = 8

PROJ_COLS = 6400
COL_HY = 0
COL_QC = 1536
COL_RQ = 2048
COL_KVPE = 4096
COL_Z = 4608
COL_XBC = 5120
COL_DT = 6144
PROJ_TILE = 1280

RET_CHUNK = 256
SSD_CHUNK = 128
HY_ROW_TILE = 512
CONV_ROWS = 1024
RET_OUT_CHUNKS = 2
SSD_OUT_CHUNKS = 4
ROW_CHUNKS = D_MODEL // LANES
GATHER_ROWS = 512
COMBINE_ROWS = 256
DMA_LOOP_UNROLL = 8


def _cparams(semantics, vmem_mb):
    assert vmem_mb * 1024 * 1024 < V7X_VMEM_BYTES
    return pltpu.CompilerParams(dimension_semantics=semantics, vmem_limit_bytes=vmem_mb * 1024 * 1024)


def _sigmoid(x):
    return 1.0 / (1.0 + jnp.exp(-x))


def _silu(x):
    return x * _sigmoid(x)


def _rms(x, w, eps=1e-6):
    return x * lax.rsqrt(jnp.mean(x * x, axis=-1, keepdims=True) + eps) * w


def _layer_norm(y, g, b, eps=1e-5):
    mu = jnp.mean(y, axis=-1, keepdims=True)
    d = y - mu
    var = jnp.mean(d * d, axis=-1, keepdims=True)
    return d * lax.rsqrt(var + eps) * g + b


def _dot(a, b):
    return jnp.dot(a, b, preferred_element_type=F32)


def _dot_nt(a, b):
    return lax.dot_general(a, b, (((1,), (1,)), ((), ())), preferred_element_type=F32)


def _dot_tn(a, b):
    return lax.dot_general(a, b, (((0,), (0,)), ((), ())), preferred_element_type=F32)


def _dot_f32(a, b):
    return jnp.dot(a, b, preferred_element_type=F32, precision=HIGHEST)


def _mm_body(x_ref, w_ref, o_ref, xb_ref):
    @pl.when(pl.program_id(1) == 0)
    def _():
        xb_ref[...] = x_ref[...].astype(BF16)

    o_ref[...] = _dot(xb_ref[...], w_ref[...])


def _matmul(x, w, tm, tn, vmem_mb=48):
    m, k = x.shape
    n = w.shape[1]
    assert m % tm == 0 and n % tn == 0
    return pl.pallas_call(
        _mm_body,
        grid=(m // tm, n // tn),
        in_specs=[pl.BlockSpec((tm, k), lambda i, j: (i, 0)),
                  pl.BlockSpec((k, tn), lambda i, j: (0, j))],
        out_specs=pl.BlockSpec((tm, tn), lambda i, j: (i, j)),
        out_shape=jax.ShapeDtypeStruct((m, n), F32),
        scratch_shapes=[pltpu.VMEM((tm, k), BF16)],
        compiler_params=_cparams(("parallel", "arbitrary"), vmem_mb),
    )(x, w)


def _rot_half_cols(w, heads):
    k = w.shape[0]
    w = w.reshape(k, heads, 2, 32)
    return jnp.stack([-w[:, :, 1], w[:, :, 0]], axis=2).reshape(k, heads * 64)


def _prep_w_in(w):
    k = w.shape[0]

    def sl(a, b):
        return w[:, a:b]

    def zc(n):
        return jnp.zeros((k, n), w.dtype)

    q_c, kv_c, k_pe = sl(0, 384), sl(384, 640), sl(640, 704)
    r_q, r_k, r_v, r_g = sl(704, 960), sl(960, 1216), sl(1216, 1728), sl(1728, 2240)
    m_z, m_xbc, m_dt, h_u = sl(2240, 2752), sl(2752, 3776), sl(3776, 3792), sl(3792, 5328)
    cols = [h_u, q_c, zc(128),
            r_q, r_k, _rot_half_cols(r_q, RET_HEADS), _rot_half_cols(r_k, RET_HEADS), r_v, r_g,
            kv_c, k_pe, zc(64), _rot_half_cols(k_pe, 1), zc(64),
            m_z, m_xbc, m_dt, zc(112), zc(PROJ_COLS - COL_DT - LANES)]
    out = jnp.concatenate(cols, axis=1).astype(BF16)
    assert out.shape[1] == PROJ_COLS
    return out


def _prep_w_uq(w):
    k = w.shape[0]
    w = w.reshape(k, MLA_HEADS, MLA_NOPE + MLA_ROPE)
    nope, rope = w[:, :, :MLA_NOPE], w[:, :, MLA_NOPE:]
    z = jnp.zeros((k, MLA_HEADS, 64), w.dtype)
    main = jnp.concatenate([nope, rope, z], axis=2).reshape(k, MLA_HEADS * 256)
    rr = _rot_half_cols(rope.reshape(k, MLA_HEADS * 64), MLA_HEADS).reshape(k, MLA_HEADS, 64)
    rot = jnp.concatenate([rr, z], axis=2).reshape(k, MLA_HEADS * 128)
    return main.astype(BF16), rot.astype(BF16)


def _prep_w_ukv(w):
    k = w.shape[0]
    w = w.reshape(k, MLA_HEADS, MLA_NOPE + MLA_V)
    return jnp.concatenate([w[:, :, :MLA_NOPE].reshape(k, -1), w[:, :, MLA_NOPE:].reshape(k, -1)], axis=1).astype(BF16)


def _rope_tables(s):
    half = 32
    inv_freq = ROPE_BASE ** (-jnp.arange(half, dtype=F32) * 2.0 / 64)
    ang = jnp.arange(s, dtype=F32)[:, None] * inv_freq[None, :]
    cos, sin = jnp.cos(ang), jnp.sin(ang)
    cos64 = jnp.concatenate([cos, cos], axis=1)
    sin64 = jnp.concatenate([sin, sin], axis=1)
    z = jnp.zeros((s, 64), F32)
    return (jnp.concatenate([cos64, z], axis=1), jnp.concatenate([sin64, z], axis=1),
            jnp.tile(cos64, (1, 4)), jnp.tile(sin64, (1, 4)))


def _ret_tables(n):
    heads = jnp.arange(RET_HEADS, dtype=F32)
    lgf = jnp.log1p(-jnp.exp2(-RET_DECAY_EXP_FWD - heads))
    lgb = jnp.log1p(-jnp.exp2(-RET_DECAY_EXP_BWD - heads))
    idx = jnp.arange(n, dtype=F32)

    def lanes(tab, width):
        return jnp.repeat(tab, width, axis=1)

    kdf = lanes(jnp.exp((n - 1.0 - idx)[:, None] * lgf), RET_DK)
    kdb = lanes(jnp.exp(idx[:, None] * lgb), RET_DK)
    qdf = lanes(jnp.exp((idx + 1.0)[:, None] * lgf), RET_DK)
    qdb = lanes(jnp.exp((n - idx)[:, None] * lgb), RET_DK)
    cdf = lanes(jnp.exp(n * lgf)[None, :], RET_DV)
    cdb = lanes(jnp.exp(n * lgb)[None, :], RET_DV)
    diff = idx[:, None] - idx[None, :]
    dec = jnp.where(diff[None] >= 0,
                    jnp.exp(jnp.maximum(diff, 0.0)[None] * lgf[:, None, None]),
                    jnp.exp(jnp.maximum(-diff, 0.0)[None] * lgb[:, None, None]))
    bd = (jnp.arange(RET_HEADS * RET_DK)[:, None] // RET_DK == jnp.arange(RET_HEADS * RET_DV)[None, :] // RET_DV)
    return kdf, kdb, qdf, qdb, cdf, cdb, dec.astype(F32), bd.astype(F32)


def _ssd_tables(n):
    tri = (jnp.arange(n)[:, None] >= jnp.arange(n)[None, :]).astype(F32)
    lane_head = jnp.arange(GROUP_WIDTH) // SSD_HEADDIM
    r = jnp.arange(LANES)
    ef = (r[:, None] == lane_head[None, :]).astype(BF16)
    eb = (r[:, None] == lane_head[None, :] + SSD_HEADS).astype(BF16)
    e = jnp.concatenate([ef, eb], axis=1)
    return tri, jnp.concatenate([e, e], axis=0)


def _dft_tables(l):
    f = jnp.arange(l, dtype=jnp.int32)
    k = (f[:, None] * f[None, :]) % (2 * l)
    ang = k.astype(F32) * (math.pi / l)
    c, s = jnp.cos(ang), jnp.sin(ang)
    sc = jnp.where(f == 0, 1.0, 2.0).astype(F32)[:, None] / (2.0 * l)
    nyq = jnp.where(f % 2 == 0, 1.0, -1.0).astype(F32)[None, :] / (2.0 * l)
    c_sc = jnp.concatenate([c * sc, nyq, jnp.zeros((7, l), F32)], axis=0)
    return c.astype(BF16), s.astype(BF16), c_sc.astype(BF16), (-s * sc).astype(BF16)


def _hyena_positions(l):
    t = jnp.linspace(0.0, 1.0, l, dtype=F32)[:, None]
    bands = (HY_EMB - 1) // 2
    ang = 2.0 * math.pi * jnp.arange(l, dtype=F32)[:, None] / l
    f = jnp.linspace(1e-4, bands - 1, bands, dtype=F32)[None, :]
    z = jnp.concatenate([t, jnp.cos(f * ang), -jnp.sin(f * ang), jnp.zeros((l, LANES - HY_EMB), F32)], axis=-1)
    deltas = jnp.abs(jnp.linspace(HY_MIN_DECAY, HY_MAX_DECAY, HY_WIDTH, dtype=F32))[None, :]
    return z, deltas


def _conv_body(taps, act, x_ref, p_ref, n_ref, w_ref, b_ref, o_ref):
    r = pl.program_id(1)
    tr = x_ref.shape[1]
    half = taps // 2
    prev = jnp.where(r > 0, p_ref[0], 0.0)
    nxt = jnp.where(r < pl.num_programs(1) - 1, n_ref[0], 0.0)
    win = jnp.concatenate([prev, x_ref[0], nxt], axis=0)
    rows = tr + 2 * SUBLANES
    acc = jnp.broadcast_to(b_ref[...], (tr, x_ref.shape[2]))
    for t in range(taps):
        sh = (half - t) % rows
        shifted = win if sh == 0 else pltpu.roll(win, sh, axis=0)
        acc = acc + shifted[SUBLANES:SUBLANES + tr] * w_ref[t:t + 1, :]
    if act:
        acc = _silu(acc)
    o_ref[0] = acc


def _dwconv(proj3, col0, width, w, b, act):
    bsz, s, _ = proj3.shape
    taps = w.shape[0]
    cb = 512
    tr = min(CONV_ROWS, s)
    off = col0 // cb
    nb8 = s // SUBLANES
    t8 = tr // SUBLANES
    return pl.pallas_call(
        functools.partial(_conv_body, taps, act),
        grid=(bsz, s // tr, width // cb),
        in_specs=[pl.BlockSpec((1, tr, cb), lambda i, r, c: (i, r, c + off)),
                  pl.BlockSpec((1, SUBLANES, cb), lambda i, r, c: (i, jnp.maximum(r * t8 - 1, 0), c + off)),
                  pl.BlockSpec((1, SUBLANES, cb), lambda i, r, c: (i, jnp.minimum((r + 1) * t8, nb8 - 1), c + off)),
                  pl.BlockSpec((taps, cb), lambda i, r, c: (0, c)),
                  pl.BlockSpec((1, cb), lambda i, r, c: (0, c))],
        out_specs=pl.BlockSpec((1, tr, cb), lambda i, r, c: (i, r, c)),
        out_shape=jax.ShapeDtypeStruct((bsz, s, width), F32),
        compiler_params=_cparams(("parallel", "parallel", "parallel"), 40),
    )(proj3, proj3, proj3, w, b.reshape(1, width))


def _mla_body(qc_ref, kvpe_ref, cq_ref, sq_ref, ck_ref, sk_ref, qn_ref, wq_ref, wqr_ref, kvn_ref, wkv_ref, on_ref,
              o_ref, k_scr, v_scr):
    @pl.when(pl.program_id(1) == 0)
    def _():
        kvpe = kvpe_ref[0]
        kvn = _rms(kvpe[:, :MLA_KV_LORA], kvn_ref[...]).astype(BF16)
        kpe = (kvpe[:, 256:384] * ck_ref[...] + kvpe[:, 384:512] * sk_ref[...]).astype(BF16)
        for h in range(MLA_HEADS):
            kn = _dot(kvn, wkv_ref[:, h * 128:(h + 1) * 128]).astype(BF16)
            k_scr[h] = jnp.concatenate([kn, kpe], axis=1)
            v_scr[h] = _dot(kvn, wkv_ref[:, 512 + h * 128:512 + (h + 1) * 128]).astype(BF16)

    scale = (MLA_NOPE + MLA_ROPE) ** -0.5
    qn = _rms(qc_ref[0][:, :MLA_Q_LORA], qn_ref[...]).astype(BF16)
    outs = []
    for h in range(MLA_HEADS):
        qm = _dot(qn, wq_ref[:, h * 256:(h + 1) * 256])
        qr = _dot(qn, wqr_ref[:, h * 128:(h + 1) * 128])
        qpe = qm[:, 128:] * cq_ref[...] + qr * sq_ref[...]
        qh = (jnp.concatenate([qm[:, :128], qpe], axis=1) * scale).astype(BF16)
        sc = _dot_nt(qh, k_scr[h])
        p = jnp.exp(sc - jnp.max(sc, axis=-1, keepdims=True))
        den = jnp.sum(p, axis=-1, keepdims=True)
        outs.append(_dot(p.astype(BF16), v_scr[h]) / den)
    o_ref[0] = _rms(jnp.concatenate(outs, axis=1), on_ref[...])


def _mla(proj3, cos_slab, sin_slab, q_norm, w_q, w_qr, kv_norm, w_kv, out_norm):
    bsz, s, _ = proj3.shape
    tq = min(512, s)
    const = lambda i, j: (0, 0)
    return pl.pallas_call(
        _mla_body,
        grid=(bsz, s // tq),
        in_specs=[pl.BlockSpec((1, tq, 512), lambda i, j: (i, j, COL_QC // 512)),
                  pl.BlockSpec((1, s, 512), lambda i, j: (i, 0, COL_KVPE // 512)),
                  pl.BlockSpec((tq, 128), lambda i, j: (j, 0)),
                  pl.BlockSpec((tq, 128), lambda i, j: (j, 0)),
                  pl.BlockSpec((s, 128), const),
                  pl.BlockSpec((s, 128), const),
                  pl.BlockSpec((1, MLA_Q_LORA), const),
                  pl.BlockSpec(w_q.shape, const),
                  pl.BlockSpec(w_qr.shape, const),
                  pl.BlockSpec((1, MLA_KV_LORA), const),
                  pl.BlockSpec(w_kv.shape, const),
                  pl.BlockSpec((1, GROUP_WIDTH), const)],
        out_specs=pl.BlockSpec((1, tq, GROUP_WIDTH), lambda i, j: (i, j, 0)),
        out_shape=jax.ShapeDtypeStruct((bsz, s, GROUP_WIDTH), F32),
        scratch_shapes=[pltpu.VMEM((MLA_HEADS, s, 256), BF16), pltpu.VMEM((MLA_HEADS, s, MLA_V), BF16)],
        compiler_params=_cparams(("parallel", "arbitrary"), 48),
    )(proj3, proj3, cos_slab, sin_slab, cos_slab, sin_slab, q_norm.reshape(1, -1), w_q, w_qr,
      kv_norm.reshape(1, -1), w_kv, out_norm.reshape(1, -1))


def _ret_state_body(nchunks, rk_ref, rkr_ref, rv_ref, cos_ref, sin_ref, kdf_ref, kdb_ref, cdf_ref, cdb_ref, bd_ref,
                    sf_ref, sb_ref, st_scr):
    n = kdf_ref.shape[0]

    def sweep(out_ref, kd_ref, cd_ref, chunk_of):
        st_scr[...] = jnp.zeros_like(st_scr)

        def body(t, carry):
            c = chunk_of(t)
            rows = pl.ds(pl.multiple_of(c * n, n), n)
            k = (rk_ref[0, rows, :] * cos_ref[rows, :] + rkr_ref[0, rows, :] * sin_ref[rows, :]) * (RET_DK ** -0.5)
            out_ref[0, c] = st_scr[...]
            new = _dot_tn((k * kd_ref[...]).astype(BF16), rv_ref[0, rows, :].astype(BF16))
            st_scr[...] = st_scr[...] * cd_ref[...] + new * bd_ref[...]
            return carry

        lax.fori_loop(0, nchunks, body, 0, unroll=2)

    sweep(sf_ref, kdf_ref, cdf_ref, lambda t: t)
    sweep(sb_ref, kdb_ref, cdb_ref, lambda t: nchunks - 1 - t)


def _ret_out_chunk(rq, rk, rqr, rkr, rv, rg, cos, sin, sf, sb, qdf_ref, qdb_ref, dec_ref):
    q = rq * cos + rqr * sin
    kb = ((rk * cos + rkr * sin) * (RET_DK ** -0.5)).astype(BF16)
    vb = rv.astype(BF16)
    cross = (_dot((q * qdf_ref[...]).astype(BF16), sf.astype(BF16))
             + _dot((q * qdb_ref[...]).astype(BF16), sb.astype(BF16)))
    lane_head = lax.broadcasted_iota(jnp.int32, (1, RET_HEADS * RET_DK), 1) // RET_DK
    outs = []
    for h in range(RET_HEADS):
        qh = jnp.where(lane_head == h, q, 0.0).astype(BF16)
        sc = _dot_nt(qh, kb) * dec_ref[h]
        y = _dot(sc.astype(BF16), vb[:, h * RET_DV:(h + 1) * RET_DV]) + cross[:, h * RET_DV:(h + 1) * RET_DV]
        mu = jnp.mean(y, axis=-1, keepdims=True)
        d = y - mu
        var = jnp.mean(d * d, axis=-1, keepdims=True)
        outs.append(d * lax.rsqrt(var + 1e-6))
    return jnp.concatenate(outs, axis=1) * _silu(rg)


def _ret_out_body(rq_ref, rk_ref, rqr_ref, rkr_ref, rv_ref, rg_ref, cos_ref, sin_ref, qdf_ref, qdb_ref, dec_ref,
                  sf_ref, sb_ref, o_ref):
    n = qdf_ref.shape[0]

    def body(k, carry):
        rows = pl.ds(pl.multiple_of(k * n, n), n)
        o_ref[0, rows, :] = _ret_out_chunk(rq_ref[0, rows, :], rk_ref[0, rows, :], rqr_ref[0, rows, :],
                                           rkr_ref[0, rows, :], rv_ref[0, rows, :], rg_ref[0, rows, :],
                                           cos_ref[rows, :], sin_ref[rows, :], sf_ref[0, k], sb_ref[0, k],
                                           qdf_ref, qdb_ref, dec_ref)
        return carry

    lax.fori_loop(0, o_ref.shape[1] // n, body, 0, unroll=2)


def _retention(proj3, cos4, sin4):
    bsz, s, _ = proj3.shape
    n = min(RET_CHUNK, s)
    c = s // n
    kdf, kdb, qdf, qdb, cdf, cdb, dec, bd = _ret_tables(n)
    c256 = COL_RQ // 256
    c512 = COL_RQ // 512
    const1 = lambda i: (0, 0)

    state_shape = jax.ShapeDtypeStruct((bsz, c, RET_HEADS * RET_DK, GROUP_WIDTH), F32)
    state_spec = pl.BlockSpec((1, c, RET_HEADS * RET_DK, GROUP_WIDTH), lambda i: (i, 0, 0, 0))
    sf, sb = pl.pallas_call(
        functools.partial(_ret_state_body, c),
        grid=(bsz,),
        in_specs=[pl.BlockSpec((1, s, 256), lambda i: (i, 0, c256 + 1)),
                  pl.BlockSpec((1, s, 256), lambda i: (i, 0, c256 + 3)),
                  pl.BlockSpec((1, s, 512), lambda i: (i, 0, c512 + 2)),
                  pl.BlockSpec((s, 256), const1), pl.BlockSpec((s, 256), const1),
                  pl.BlockSpec((n, 256), const1), pl.BlockSpec((n, 256), const1),
                  pl.BlockSpec((1, 512), const1), pl.BlockSpec((1, 512), const1),
                  pl.BlockSpec((256, 512), const1)],
        out_specs=[state_spec, state_spec],
        out_shape=[state_shape, state_shape],
        scratch_shapes=[pltpu.VMEM((RET_HEADS * RET_DK, GROUP_WIDTH), F32)],
        compiler_params=_cparams(("parallel",), 48),
    )(proj3, proj3, proj3, cos4, sin4, kdf, kdb, cdf, cdb, bd)

    k = min(RET_OUT_CHUNKS, c)
    const3 = lambda i, j: (0, 0, 0)
    const2 = lambda i, j: (0, 0)
    return pl.pallas_call(
        _ret_out_body,
        grid=(bsz, c // k),
        in_specs=[pl.BlockSpec((1, k * n, 256), lambda i, j: (i, j, c256)),
                  pl.BlockSpec((1, k * n, 256), lambda i, j: (i, j, c256 + 1)),
                  pl.BlockSpec((1, k * n, 256), lambda i, j: (i, j, c256 + 2)),
                  pl.BlockSpec((1, k * n, 256), lambda i, j: (i, j, c256 + 3)),
                  pl.BlockSpec((1, k * n, 512), lambda i, j: (i, j, c512 + 2)),
                  pl.BlockSpec((1, k * n, 512), lambda i, j: (i, j, c512 + 3)),
                  pl.BlockSpec((k * n, 256), lambda i, j: (j, 0)),
                  pl.BlockSpec((k * n, 256), lambda i, j: (j, 0)),
                  pl.BlockSpec((n, 256), const2), pl.BlockSpec((n, 256), const2),
                  pl.BlockSpec((RET_HEADS, n, n), const3),
                  pl.BlockSpec((1, k, 256, 512), lambda i, j: (i, j, 0, 0)),
                  pl.BlockSpec((1, k, 256, 512), lambda i, j: (i, j, 0, 0))],
        out_specs=pl.BlockSpec((1, k * n, GROUP_WIDTH), lambda i, j: (i, j, 0)),
        out_shape=jax.ShapeDtypeStruct((bsz, s, GROUP_WIDTH), F32),
        compiler_params=_cparams(("parallel", "parallel"), 32),
    )(proj3, proj3, proj3, proj3, proj3, proj3, cos4, sin4, qdf, qdb, dec, sf, sb)


def _ssd_decays(dt_rows, dtb_ref, a_ref, tri_ref):
    dt_raw = dt_rows + dtb_ref[...]
    dt = jnp.maximum(dt_raw, 0.0) + jnp.log1p(jnp.exp(-jnp.abs(dt_raw)))
    la = dt * a_ref[...]
    return dt, la, _dot_f32(tri_ref[...], la)


def _expand_heads(v, e):
    hi = v.astype(BF16)
    lo = (v - hi.astype(F32)).astype(BF16)
    return _dot(jnp.concatenate([hi, lo], axis=1), e)


def _ssd_state_body(nchunks, xbc_ref, dt_ref, dtb_ref, a_ref, tri_ref, e_ref, sf_ref, sb_ref, st_scr):
    n = SSD_CHUNK
    w = GROUP_WIDTH
    fwd_lane = lax.broadcasted_iota(jnp.int32, (1, LANES), 1) < SSD_HEADS

    def sweep(out_ref, lo, chunk_of):
        st_scr[...] = jnp.zeros_like(st_scr)

        def body(t, carry):
            c = chunk_of(t)
            rows = pl.ds(pl.multiple_of(c * n, n), n)
            dt, la, cs = _ssd_decays(dt_ref[0, rows, :], dtb_ref, a_ref, tri_ref)
            tot = cs[n - 1:n, :]
            wgt = dt * jnp.exp(jnp.where(fwd_lane, tot - cs, cs - la))
            both = _expand_heads(jnp.concatenate([wgt, jnp.broadcast_to(jnp.exp(tot), (SUBLANES, LANES))], axis=0),
                                 e_ref[:, lo:lo + w])
            out_ref[0, c] = st_scr[...]
            xw = (xbc_ref[0, rows, 0:w] * both[:n]).astype(BF16)
            new = []
            for g in range(SSD_GROUPS):
                bg = xbc_ref[0, rows, w + g * SSD_STATE:w + (g + 1) * SSD_STATE].astype(BF16)
                new.append(_dot_tn(bg, xw[:, g * 256:(g + 1) * 256]))
            st_scr[...] = st_scr[...] * both[n:n + 1] + jnp.concatenate(new, axis=1)
            return carry

        lax.fori_loop(0, nchunks, body, 0, unroll=4)

    sweep(sf_ref, 0, lambda t: t)
    sweep(sb_ref, w, lambda t: nchunks - 1 - t)


def _ssd_out_chunk(xbc, dt_rows, sf, sb, dtb_ref, a_ref, tri_ref, e_ref, dskip_ref):
    n = xbc.shape[0]
    w = GROUP_WIDTH
    dt, la, cs = _ssd_decays(dt_rows, dtb_ref, a_ref, tri_ref)
    ecs = cs - la
    fwd_lane = lax.broadcasted_iota(jnp.int32, (1, LANES), 1) < SSD_HEADS
    carry = jnp.exp(jnp.where(fwd_lane, cs, cs[n - 1:n, :] - ecs))
    both = _expand_heads(jnp.concatenate([dt, carry], axis=0), e_ref[...])
    dt_e = both[:n]
    carry_e = both[n:]
    xs = xbc[:, :w]
    xdt_f = xs * dt_e[:, :w]
    xdt_b = xs * dt_e[:, w:]
    cs_t = cs.T
    ecs_t = ecs.T
    ii = lax.broadcasted_iota(jnp.int32, (n, n), 0)
    jj = lax.broadcasted_iota(jnp.int32, (n, n), 1)
    low = lax.broadcasted_iota(jnp.int32, (1, LANES), 1) < SSD_HEADDIM
    neg = -1e30
    ydiag = []
    yoff = []
    for g in range(SSD_GROUPS):
        bg = xbc[:, w + g * SSD_STATE:w + (g + 1) * SSD_STATE].astype(BF16)
        cg = xbc[:, w + 256 + g * SSD_STATE:w + 256 + (g + 1) * SSD_STATE].astype(BF16)
        cb = _dot_nt(cg, bg)
        for pair in range(2):
            p = 2 * g + pair
            lhs = []
            for h in (2 * p, 2 * p + 1):
                lf = jnp.exp(jnp.where(ii >= jj, cs[:, h:h + 1] - cs_t[h:h + 1, :], neg))
                lhs.append((cb * lf).astype(BF16))
            for h in (2 * p, 2 * p + 1):
                hb = SSD_HEADS + h
                lb = jnp.exp(jnp.where(jj > ii, ecs_t[hb:hb + 1, :] - ecs[:, hb:hb + 1], neg))
                lhs.append((cb * lb).astype(BF16))
            xf = xdt_f[:, p * LANES:(p + 1) * LANES]
            xb = xdt_b[:, p * LANES:(p + 1) * LANES]
            rhs = jnp.concatenate([jnp.where(low, xf, 0.0), jnp.where(low, 0.0, xf),
                                   jnp.where(low, xb, 0.0), jnp.where(low, 0.0, xb)], axis=0).astype(BF16)
            ydiag.append(_dot(jnp.concatenate(lhs, axis=1), rhs))
        yoff.append(_dot(cg, sf[:, g * 256:(g + 1) * 256].astype(BF16)) * carry_e[:, g * 256:(g + 1) * 256]
                    + _dot(cg, sb[:, g * 256:(g + 1) * 256].astype(BF16)) * carry_e[:, w + g * 256:w + (g + 1) * 256])
    return jnp.concatenate(ydiag, axis=1) + jnp.concatenate(yoff, axis=1) + xs * dskip_ref[...]


def _ssd_out_body(xbc_ref, dt_ref, z_ref, dtb_ref, a_ref, tri_ref, e_ref, dskip_ref, nw_ref, sf_ref, sb_ref, o_ref):
    n = SSD_CHUNK

    def body(k, carry):
        rows = pl.ds(pl.multiple_of(k * n, n), n)
        y = _ssd_out_chunk(xbc_ref[0, rows, :], dt_ref[0, rows, :], sf_ref[0, k], sb_ref[0, k], dtb_ref, a_ref, tri_ref,
                           e_ref, dskip_ref)
        o_ref[0, rows, :] = _rms(y * _silu(z_ref[0, rows, :]), nw_ref[...])
        return carry

    lax.fori_loop(0, o_ref.shape[1] // n, body, 0, unroll=2)


def _ssd(proj3, xbc_act, dt_bias, a_log, d_skip, norm_w):
    bsz, s, _ = proj3.shape
    n = SSD_CHUNK
    c = s // n
    tri, e = _ssd_tables(n)
    pad = jnp.zeros((LANES - 2 * SSD_HEADS,), F32)
    dtb = jnp.concatenate([dt_bias.reshape(-1), pad]).reshape(1, LANES)
    a = jnp.concatenate([-jnp.exp(a_log.reshape(-1)), pad]).reshape(1, LANES)
    dskip = jnp.repeat(d_skip, SSD_HEADDIM).reshape(1, GROUP_WIDTH)
    cdt = COL_DT // LANES
    const1 = lambda i: (0, 0)

    state_shape = jax.ShapeDtypeStruct((bsz, c, SSD_STATE, GROUP_WIDTH), F32)
    state_spec = pl.BlockSpec((1, c, SSD_STATE, GROUP_WIDTH), lambda i: (i, 0, 0, 0))
    sf, sb = pl.pallas_call(
        functools.partial(_ssd_state_body, c),
        grid=(bsz,),
        in_specs=[pl.BlockSpec((1, s, SSD_CONV_CH), lambda i: (i, 0, 0)),
                  pl.BlockSpec((1, s, LANES), lambda i: (i, 0, cdt)),
                  pl.BlockSpec((1, LANES), const1), pl.BlockSpec((1, LANES), const1),
                  pl.BlockSpec((n, n), const1), pl.BlockSpec((2 * LANES, 2 * GROUP_WIDTH), const1)],
        out_specs=[state_spec, state_spec],
        out_shape=[state_shape, state_shape],
        scratch_shapes=[pltpu.VMEM((SSD_STATE, GROUP_WIDTH), F32)],
        compiler_params=_cparams(("parallel",), 48),
    )(xbc_act, proj3, dtb, a, tri, e)

    k = min(SSD_OUT_CHUNKS, c)
    const2 = lambda i, j: (0, 0)
    return pl.pallas_call(
        _ssd_out_body,
        grid=(bsz, c // k),
        in_specs=[pl.BlockSpec((1, k * n, SSD_CONV_CH), lambda i, j: (i, j, 0)),
                  pl.BlockSpec((1, k * n, LANES), lambda i, j: (i, j, cdt)),
                  pl.BlockSpec((1, k * n, GROUP_WIDTH), lambda i, j: (i, j, COL_Z // GROUP_WIDTH)),
                  pl.BlockSpec((1, LANES), const2), pl.BlockSpec((1, LANES), const2),
                  pl.BlockSpec((n, n), const2), pl.BlockSpec((2 * LANES, 2 * GROUP_WIDTH), const2),
                  pl.BlockSpec((1, GROUP_WIDTH), const2), pl.BlockSpec((1, GROUP_WIDTH), const2),
                  pl.BlockSpec((1, k, SSD_STATE, GROUP_WIDTH), lambda i, j: (i, j, 0, 0)),
                  pl.BlockSpec((1, k, SSD_STATE, GROUP_WIDTH), lambda i, j: (i, j, 0, 0))],
        out_specs=pl.BlockSpec((1, k * n, GROUP_WIDTH), lambda i, j: (i, j, 0)),
        out_shape=jax.ShapeDtypeStruct((bsz, s, GROUP_WIDTH), F32),
        compiler_params=_cparams(("parallel", "parallel"), 32),
    )(xbc_act, proj3, proj3, dtb, a, tri, e, dskip, norm_w.reshape(1, -1), sf, sb)


def _hy_filter_body(z_ref, w1_ref, b1_ref, w2_ref, b2_ref, w3_ref, fr_ref, dl_ref, hs_ref, hd_ref):
    tl = z_ref.shape[0]
    z = z_ref[...]
    hid = jnp.sin(fr_ref[0:1, :] * (_dot_f32(z, w1_ref[...]) + b1_ref[...]))
    hid = jnp.sin(fr_ref[1:2, :] * (_dot_f32(hid, w2_ref[...]) + b2_ref[...]))
    filt = _dot_f32(hid, w3_ref[...])
    dec = jnp.exp(-z[:, 0:1] * dl_ref[...])
    row = pl.program_id(0) * tl + lax.broadcasted_iota(jnp.int32, (tl, 1), 0)
    for o in range(HY_ORDER):
        base = o * 2 * HY_WIDTH
        hf = filt[:, base:base + HY_WIDTH] * dec
        hb = jnp.where(row == 0, 0.0, filt[:, base + HY_WIDTH:base + 2 * HY_WIDTH] * dec)
        hs_ref[:, o * HY_WIDTH:(o + 1) * HY_WIDTH] = hf + hb
        hd_ref[:, o * HY_WIDTH:(o + 1) * HY_WIDTH] = hf - hb


def _hy_filters(l, w1, b1, w2, b2, w3, freq):
    z, deltas = _hyena_positions(l)
    hid = HY_FILTER_HIDDEN
    w1p = jnp.zeros((LANES, LANES), F32).at[:HY_EMB, :hid].set(w1)
    w2p = jnp.zeros((LANES, LANES), F32).at[:hid, :hid].set(w2)
    w3p = jnp.zeros((LANES, w3.shape[1]), F32).at[:hid].set(w3)
    b1p = jnp.zeros((1, LANES), F32).at[0, :hid].set(b1)
    b2p = jnp.zeros((1, LANES), F32).at[0, :hid].set(b2)
    frp = jnp.zeros((2, LANES), F32).at[:, :hid].set(freq)
    tl = min(256, l)
    ncol = HY_ORDER * HY_WIDTH
    const = lambda i: (0, 0)
    return pl.pallas_call(
        _hy_filter_body,
        grid=(l // tl,),
        in_specs=[pl.BlockSpec((tl, LANES), lambda i: (i, 0)),
                  pl.BlockSpec((LANES, LANES), const), pl.BlockSpec((1, LANES), const),
                  pl.BlockSpec((LANES, LANES), const), pl.BlockSpec((1, LANES), const),
                  pl.BlockSpec((LANES, w3.shape[1]), const), pl.BlockSpec((2, LANES), const),
                  pl.BlockSpec((1, HY_WIDTH), const)],
        out_specs=[pl.BlockSpec((tl, ncol), lambda i: (i, 0)), pl.BlockSpec((tl, ncol), lambda i: (i, 0))],
        out_shape=[jax.ShapeDtypeStruct((l, ncol), F32), jax.ShapeDtypeStruct((l, ncol), F32)],
        compiler_params=_cparams(("parallel",), 32),
    )(z, w1p, b1p, w2p, b2p, w3p, frp, deltas)


def _short_conv_rows(x_ref, w_ref, b_ref, r0, n):
    l = x_ref.shape[1]
    lo = max(r0 - SUBLANES, 0)
    hi = min(r0 + n + SUBLANES, l)
    win = x_ref[0, lo:hi, :]
    pad = jnp.zeros((SUBLANES, win.shape[1]), F32)
    if r0 == 0:
        win = jnp.concatenate([pad, win], axis=0)
    if r0 + n == l:
        win = jnp.concatenate([win, pad], axis=0)
    rows = n + 2 * SUBLANES
    prev = pltpu.roll(win, 1, axis=0)[SUBLANES:SUBLANES + n]
    nxt = pltpu.roll(win, rows - 1, axis=0)[SUBLANES:SUBLANES + n]
    return prev * w_ref[0:1, :] + win[SUBLANES:SUBLANES + n] * w_ref[1:2, :] + nxt * w_ref[2:3, :] + b_ref[...]


def _hy_conv_body(u0_ref, u1_ref, u2_ref, cw0_ref, cw1_ref, cw2_ref, cb0_ref, cb1_ref, cb2_ref, c_ref, s_ref,
                  hr0_ref, hi0_ref, hr1_ref, hi1_ref, b0_ref, b1_ref, o_ref, xb_scr, pre_scr, pim_scr, z_scr):
    l = u0_ref.shape[1]
    tf = min(HY_ROW_TILE, l)
    ntile = l // tf
    sgn = jnp.where(lax.broadcasted_iota(jnp.int32, (l, 1), 0) % 2 == 0, 1.0, -1.0)

    def long_conv(gate_ref, gw_ref, gb_ref, hr_ref, hi_ref, b_ref, out_ref):
        x = z_scr[...]
        xb_scr[...] = x.astype(BF16)
        x_nyq = jnp.sum(x * sgn, axis=0, keepdims=True) * hr_ref[l:l + 1, :]
        for r in range(ntile):
            rows = pl.ds(r * tf, tf)
            xc = _dot(c_ref[rows, :], xb_scr[...])
            xs = _dot(s_ref[rows, :], xb_scr[...])
            hre = hr_ref[rows, :]
            him = hi_ref[rows, :]
            pre_scr[rows, :] = (xc * hre + xs * him).astype(BF16)
            pim_scr[rows, :] = (xc * him - xs * hre).astype(BF16)
        for r in range(ntile):
            rows = pl.ds(r * tf, tf)
            y = _dot(c_ref[rows, :], pre_scr[...]) - _dot(s_ref[rows, :], pim_scr[...])
            y = y + sgn[r * tf:(r + 1) * tf] * x_nyq
            gate = _short_conv_rows(gate_ref, gw_ref, gb_ref, r * tf, tf)
            out_ref[rows, :] = gate * (y + z_scr[rows, :] * b_ref[0])

    for r in range(ntile):
        z_scr[pl.ds(r * tf, tf), :] = _short_conv_rows(u0_ref, cw0_ref, cb0_ref, r * tf, tf)
    long_conv(u1_ref, cw1_ref, cb1_ref, hr0_ref, hi0_ref, b0_ref, z_scr)
    long_conv(u2_ref, cw2_ref, cb2_ref, hr1_ref, hi1_ref, b1_ref, o_ref.at[0])


def _hy_long_conv(proj3, conv_w, conv_b, cmat, smat, hre, him, bias):
    bsz, l, _ = proj3.shape
    tc = 256
    nb = HY_WIDTH // tc
    off = COL_HY // tc
    const = lambda j, i: (0, 0)
    single = pl.Buffered(1)
    bias3 = bias.reshape(HY_ORDER, 1, HY_WIDTH)
    conv_b2 = conv_b.reshape(1, -1)

    def part(k):
        return [pl.BlockSpec((1, l, tc), lambda j, i: (i, 0, off + k * nb + j))]

    def part_w(k):
        return [pl.BlockSpec((HY_SHORT, tc), lambda j, i: (0, k * nb + j))]

    def part_b(k):
        return [pl.BlockSpec((1, tc), lambda j, i: (0, k * nb + j))]

    return pl.pallas_call(
        _hy_conv_body,
        grid=(nb, bsz),
        in_specs=part(0) + part(1) + part(2) + part_w(0) + part_w(1) + part_w(2) + part_b(0) + part_b(1) + part_b(2) + [
            pl.BlockSpec((l, l), const, pipeline_mode=single),
            pl.BlockSpec((l, l), const, pipeline_mode=single),
            pl.BlockSpec((l + SUBLANES, tc), lambda j, i: (0, j), pipeline_mode=single),
            pl.BlockSpec((l, tc), lambda j, i: (0, j), pipeline_mode=single),
            pl.BlockSpec((l + SUBLANES, tc), lambda j, i: (0, nb + j), pipeline_mode=single),
            pl.BlockSpec((l, tc), lambda j, i: (0, nb + j), pipeline_mode=single),
            pl.BlockSpec((1, 1, tc), lambda j, i: (0, 0, j)),
            pl.BlockSpec((1, 1, tc), lambda j, i: (1, 0, j))],
        out_specs=pl.BlockSpec((1, l, tc), lambda j, i: (i, 0, j)),
        out_shape=jax.ShapeDtypeStruct((bsz, l, HY_WIDTH), F32),
        scratch_shapes=[pltpu.VMEM((l, tc), BF16), pltpu.VMEM((l, tc), BF16), pltpu.VMEM((l, tc), BF16),
                        pltpu.VMEM((l, tc), F32)],
        compiler_params=_cparams(("parallel", "parallel"), 56),
    )(proj3, proj3, proj3, conv_w, conv_w, conv_w, conv_b2, conv_b2, conv_b2, cmat, smat, hre, him, hre, him,
      bias3, bias3)


def _hyena(proj3, conv_w, conv_b, w1, b1, w2, b2, w3, freq, bias):
    l = proj3.shape[1]
    hs, hd = _hy_filters(l, w1, b1, w2, b2, w3, freq)
    cmat, smat, c_sc, s_sc_neg = _dft_tables(l)
    tn = 256
    hre = _matmul(c_sc, hs.astype(BF16), l + SUBLANES, tn)
    him = _matmul(s_sc_neg, hd.astype(BF16), l, tn)
    return _hy_long_conv(proj3, conv_w, conv_b, cmat, smat, hre, him, bias)


def _out_proj_body(a_ref, b_ref, c_ref, d_ref, x_ref, w_ref, dn_ref, g_ref, beta_ref, o_ref, *rows_ref):
    gw = GROUP_WIDTH
    h = _dot(a_ref[...].astype(BF16), w_ref[0:gw, :])
    h += _dot(b_ref[...].astype(BF16), w_ref[gw:2 * gw, :])
    h += _dot(c_ref[...].astype(BF16), w_ref[2 * gw:3 * gw, :])
    h += _dot(_rms(d_ref[...], dn_ref[...]).astype(BF16), w_ref[3 * gw:4 * gw, :])
    y = _layer_norm(ALPHA * x_ref[...] + h, g_ref[...], beta_ref[...])
    o_ref[...] = y
    if rows_ref:
        _store_row_tiles(rows_ref[0], y)


def _store_row_tiles(dst_ref, y):
    rows = y.shape[0]
    for c in range(ROW_CHUNKS):
        dst_ref[pl.ds(c, rows, stride=ROW_CHUNKS), :] = y[:, c * LANES:(c + 1) * LANES]


def _load_row_tiles(src_ref, first, rows):
    return jnp.concatenate([src_ref[pl.ds(first * ROW_CHUNKS + c, rows, stride=ROW_CHUNKS), :]
                            for c in range(ROW_CHUNKS)], axis=1)


def _out_proj_ln(a, b, c, d, x, w_out, hy_norm, g, beta, emit_row_tiles):
    m = x.shape[0]
    tm = min(512, m)
    gw = GROUP_WIDTH
    const = lambda i: (0, 0)
    row = lambda i: (i, 0)
    out_specs = [pl.BlockSpec((tm, D_MODEL), row)]
    out_shape = [jax.ShapeDtypeStruct((m, D_MODEL), F32)]
    if emit_row_tiles:
        out_specs.append(pl.BlockSpec((tm * ROW_CHUNKS, LANES), row))
        out_shape.append(jax.ShapeDtypeStruct((m * ROW_CHUNKS, LANES), F32))
    return pl.pallas_call(
        _out_proj_body,
        grid=(m // tm,),
        in_specs=[pl.BlockSpec((tm, gw), row), pl.BlockSpec((tm, gw), row), pl.BlockSpec((tm, gw), row),
                  pl.BlockSpec((tm, gw), row), pl.BlockSpec((tm, D_MODEL), row),
                  pl.BlockSpec((D_MODEL, D_MODEL), const), pl.BlockSpec((1, gw), const),
                  pl.BlockSpec((1, D_MODEL), const), pl.BlockSpec((1, D_MODEL), const)],
        out_specs=out_specs,
        out_shape=out_shape,
        compiler_params=_cparams(("parallel",), 48),
    )(a, b, c, d, x, w_out, hy_norm.reshape(1, -1), g.reshape(1, -1), beta.reshape(1, -1))


def _ffn_body(x_ref, wg_ref, wu_ref, wd_ref, g_ref, beta_ref, o_ref, xb_ref, acc_ref):
    j = pl.program_id(1)

    @pl.when(j == 0)
    def _():
        xb_ref[...] = x_ref[...].astype(BF16)
        acc_ref[...] = jnp.zeros_like(acc_ref)

    xb = xb_ref[...]
    hidden = (_silu(_dot(xb, wg_ref[...])) * _dot(xb, wu_ref[...])).astype(BF16)
    acc_ref[...] += _dot(hidden, wd_ref[...])

    @pl.when(j == pl.num_programs(1) - 1)
    def _():
        o_ref[...] = _layer_norm(ALPHA * x_ref[...] + acc_ref[...], g_ref[...], beta_ref[...])


def _ffn_ln(x, wg, wu, wd, g, beta):
    m = x.shape[0]
    dff = wg.shape[1]
    tm = min(512, m)
    tf = 512
    const = lambda i, j: (0, 0)
    return pl.pallas_call(
        _ffn_body,
        grid=(m // tm, dff // tf),
        in_specs=[pl.BlockSpec((tm, D_MODEL), lambda i, j: (i, 0)),
                  pl.BlockSpec((D_MODEL, tf), lambda i, j: (0, j)),
                  pl.BlockSpec((D_MODEL, tf), lambda i, j: (0, j)),
                  pl.BlockSpec((tf, D_MODEL), lambda i, j: (j, 0)),
                  pl.BlockSpec((1, D_MODEL), const), pl.BlockSpec((1, D_MODEL), const)],
        out_specs=pl.BlockSpec((tm, D_MODEL), lambda i, j: (i, 0)),
        out_shape=jax.ShapeDtypeStruct((m, D_MODEL), F32),
        scratch_shapes=[pltpu.VMEM((tm, D_MODEL), BF16), pltpu.VMEM((tm, D_MODEL), F32)],
        compiler_params=_cparams(("parallel", "arbitrary"), 48),
    )(x, wg, wu, wd, g.reshape(1, -1), beta.reshape(1, -1))


def _router_body(x_ref, w_ref, idx_ref, gate_ref):
    logits = _dot_f32(x_ref[...], w_ref[...])
    lane = lax.broadcasted_iota(jnp.int32, logits.shape, 1)
    neg = -jnp.inf
    logits = jnp.where(lane < N_EXPERTS, logits, neg)
    m1 = jnp.max(logits, axis=-1, keepdims=True)
    i1 = jnp.min(jnp.where(logits == m1, lane, LANES), axis=-1, keepdims=True)
    rest = jnp.where(lane == i1, neg, logits)
    m2 = jnp.max(rest, axis=-1, keepdims=True)
    i2 = jnp.min(jnp.where(rest == m2, lane, LANES), axis=-1, keepdims=True)
    e2 = jnp.exp(m2 - m1)
    den = 1.0 + e2
    idx_ref[...] = jnp.where(lane == 0, i1, jnp.where(lane == 1, i2, 0))
    gate_ref[...] = jnp.where(lane == 0, 1.0 / den, jnp.where(lane == 1, e2 / den, 0.0))


def _router(x, w_router):
    m = x.shape[0]
    tm = min(512, m)
    wp = jnp.zeros((D_MODEL, LANES), F32).at[:, :N_EXPERTS].set(w_router)
    return pl.pallas_call(
        _router_body,
        grid=(m // tm,),
        in_specs=[pl.BlockSpec((tm, D_MODEL), lambda i: (i, 0)), pl.BlockSpec((D_MODEL, LANES), lambda i: (0, 0))],
        out_specs=[pl.BlockSpec((tm, LANES), lambda i: (i, 0)), pl.BlockSpec((tm, LANES), lambda i: (i, 0))],
        out_shape=[jax.ShapeDtypeStruct((m, LANES), jnp.int32), jax.ShapeDtypeStruct((m, LANES), F32)],
        compiler_params=_cparams(("parallel",), 32),
    )(x, wp)


def _row_copy(src_ref, src_row, buf_ref, buf_row, sem):
    src = src_ref.at[pl.ds(pl.multiple_of(src_row * ROW_CHUNKS, ROW_CHUNKS), ROW_CHUNKS)]
    dst = buf_ref.at[pl.ds(pl.multiple_of(buf_row * ROW_CHUNKS, ROW_CHUNKS), ROW_CHUNKS)]
    return pltpu.make_async_copy(src, dst, sem)


def _start_row_gather(idx_ref, n, src_ref, buf_ref, sem):
    def start(r, carry):
        _row_copy(src_ref, idx_ref[0, 0, r], buf_ref, r, sem).start()
        return carry

    lax.fori_loop(0, n, start, 0, unroll=DMA_LOOP_UNROLL)


def _wait_row_gather(n, src_ref, buf_ref, sem):
    def wait(r, carry):
        _row_copy(src_ref, 0, buf_ref, r, sem).wait()
        return carry

    lax.fori_loop(0, n, wait, 0, unroll=DMA_LOOP_UNROLL)


def _pipelined_row_gather(idx_cur_ref, idx_next_ref, n, src_ref, buf, sems):
    i = pl.program_id(0)
    slot = i % 2

    @pl.when(i == 0)
    def _():
        _start_row_gather(idx_cur_ref, n, src_ref, buf.at[0], sems.at[0])

    @pl.when(i + 1 < pl.num_programs(0))
    def _():
        _start_row_gather(idx_next_ref, n, src_ref, buf.at[1 - slot], sems.at[1 - slot])

    _wait_row_gather(n, src_ref, buf.at[slot], sems.at[slot])
    return slot


def _idx_specs(n, nblk):
    return [pl.BlockSpec((1, 1, n), lambda i: (i, 0, 0), memory_space=pltpu.SMEM),
            pl.BlockSpec((1, 1, n), lambda i: (jnp.minimum(i + 1, nblk - 1), 0, 0), memory_space=pltpu.SMEM)]


def _gather_x_body(idx_cur_ref, idx_next_ref, src_ref, o_ref, buf, sems):
    rows = o_ref.shape[0]
    slot = _pipelined_row_gather(idx_cur_ref, idx_next_ref, rows, src_ref, buf, sems)
    o_ref[...] = _load_row_tiles(buf.at[slot], 0, rows).astype(BF16)


def _gather_x(x_tiles, slot_tok):
    cap = slot_tok.shape[0]
    rows = GATHER_ROWS
    assert cap % rows == 0
    nblk = cap // rows
    idx = slot_tok.reshape(nblk, 1, rows)
    return pl.pallas_call(
        _gather_x_body,
        grid=(nblk,),
        in_specs=_idx_specs(rows, nblk) + [pl.BlockSpec(memory_space=pl.ANY)],
        out_specs=pl.BlockSpec((rows, D_MODEL), lambda i: (i, 0)),
        out_shape=jax.ShapeDtypeStruct((cap, D_MODEL), BF16),
        scratch_shapes=[pltpu.VMEM((2, rows * ROW_CHUNKS, LANES), F32), pltpu.SemaphoreType.DMA((2,))],
        compiler_params=pltpu.CompilerParams(dimension_semantics=("arbitrary",), disable_bounds_checks=True),
    )(idx, idx, x_tiles)


def _expert_body(be_ref, nv_ref, x_ref, wg_ref, wu_ref, wd_ref, o_ref, acc_ref):
    i = pl.program_id(0)
    j = pl.program_id(1)
    half = x_ref.shape[0] // 2

    @pl.when(j == 0)
    def _():
        acc_ref[...] = jnp.zeros_like(acc_ref)

    def swiglu_rows(r0):
        xb = x_ref[r0:r0 + half, :]
        hidden = (_silu(_dot(xb, wg_ref[0])) * _dot(xb, wu_ref[0])).astype(BF16)
        acc_ref[r0:r0 + half, :] += _dot(hidden, wd_ref[0])

    @pl.when(nv_ref[i] > 0)
    def _():
        swiglu_rows(0)

    @pl.when(nv_ref[i] > half)
    def _():
        swiglu_rows(half)

    @pl.when(j == pl.num_programs(1) - 1)
    def _():
        _store_row_tiles(o_ref, acc_ref[...])


def _expert_ffn(x_slots, block_expert, block_valid, wg, wu, wd, tm):
    cap = x_slots.shape[0]
    dff = wg.shape[2]
    tf = 512

    def ff_tile(i, j, nv):
        return jnp.where(nv[i] > 0, j, 0)

    grid_spec = pltpu.PrefetchScalarGridSpec(
        num_scalar_prefetch=2,
        grid=(cap // tm, dff // tf),
        in_specs=[pl.BlockSpec((tm, D_MODEL), lambda i, j, be, nv: (i, 0)),
                  pl.BlockSpec((1, D_MODEL, tf), lambda i, j, be, nv: (be[i], 0, ff_tile(i, j, nv))),
                  pl.BlockSpec((1, D_MODEL, tf), lambda i, j, be, nv: (be[i], 0, ff_tile(i, j, nv))),
                  pl.BlockSpec((1, tf, D_MODEL), lambda i, j, be, nv: (be[i], ff_tile(i, j, nv), 0))],
        out_specs=pl.BlockSpec((tm * ROW_CHUNKS, LANES), lambda i, j, be, nv: (i, 0)),
        scratch_shapes=[pltpu.VMEM((tm, D_MODEL), F32)],
    )
    return pl.pallas_call(
        _expert_body,
        grid_spec=grid_spec,
        out_shape=jax.ShapeDtypeStruct((cap * ROW_CHUNKS, LANES), F32),
        compiler_params=_cparams(("parallel", "arbitrary"), 56),
    )(block_expert, block_valid, x_slots, wg, wu, wd)


def _combine_body(idx_cur_ref, idx_next_ref, x_ref, gt_ref, g_ref, beta_ref, y_ref, o_ref, buf, sems):
    rows = x_ref.shape[0]
    slot = _pipelined_row_gather(idx_cur_ref, idx_next_ref, TOP_K * rows, y_ref, buf, sems)
    gt = gt_ref[...]
    f = (gt[:, 0:1] * _load_row_tiles(buf.at[slot], 0, rows)
         + gt[:, 1:2] * _load_row_tiles(buf.at[slot], rows, rows))
    o_ref[...] = _layer_norm(ALPHA * x_ref[...] + f, g_ref[...], beta_ref[...])


def _combine_ln(x, y_tiles, pos, gates, g, beta):
    m = x.shape[0]
    rows = min(COMBINE_ROWS, m)
    nblk = m // rows
    n = TOP_K * rows
    idx = pos.reshape(nblk, rows, TOP_K).transpose(0, 2, 1).reshape(nblk, 1, n)
    const = lambda i: (0, 0)
    return pl.pallas_call(
        _combine_body,
        grid=(nblk,),
        in_specs=_idx_specs(n, nblk) + [
            pl.BlockSpec((rows, D_MODEL), lambda i: (i, 0)),
            pl.BlockSpec((rows, LANES), lambda i: (i, 0)),
            pl.BlockSpec((1, D_MODEL), const), pl.BlockSpec((1, D_MODEL), const),
            pl.BlockSpec(memory_space=pl.ANY)],
        out_specs=pl.BlockSpec((rows, D_MODEL), lambda i: (i, 0)),
        out_shape=jax.ShapeDtypeStruct((m, D_MODEL), F32),
        scratch_shapes=[pltpu.VMEM((2, n * ROW_CHUNKS, LANES), F32), pltpu.SemaphoreType.DMA((2,))],
        compiler_params=pltpu.CompilerParams(dimension_semantics=("arbitrary",), disable_bounds_checks=True,
                                             vmem_limit_bytes=48 * 1024 * 1024),
    )(idx, idx, x, gates, g.reshape(1, -1), beta.reshape(1, -1), y_tiles)


def _moe_ln(x, x_tiles, w_router, wg, wu, wd, g, beta, tm):
    m = x.shape[0]
    n_asg = m * TOP_K
    idx, gates = _router(x, w_router)
    e_flat = idx[:, :TOP_K].reshape(-1)
    onehot = (e_flat[:, None] == jnp.arange(N_EXPERTS, dtype=jnp.int32)[None, :]).astype(jnp.int32)
    csum = jnp.cumsum(onehot, axis=0)
    counts = csum[-1]
    rank = jnp.sum(csum * onehot, axis=1) - 1
    padded = (counts + tm - 1) // tm * tm
    pend = jnp.cumsum(padded)
    pstart = pend - padded
    dest = (jnp.sum(pstart[None, :] * onehot, axis=1) + rank).astype(jnp.int32)
    cap = n_asg + N_EXPERTS * tm
    tok = jnp.arange(n_asg, dtype=jnp.int32) // TOP_K
    slot_tok = jnp.zeros((cap,), jnp.int32).at[dest].set(tok, unique_indices=True)
    nblk = cap // tm
    block_start = jnp.arange(nblk, dtype=pend.dtype) * tm
    block_expert = jnp.minimum(jnp.searchsorted(pend, block_start, side='right'), N_EXPERTS - 1).astype(jnp.int32)
    block_valid = jnp.clip((pstart + counts)[block_expert] - block_start, 0, tm).astype(jnp.int32)

    x_slots = _gather_x(x_tiles, slot_tok)
    y_tiles = _expert_ffn(x_slots, block_expert, block_valid, wg, wu, wd, tm)
    return _combine_ln(x, y_tiles, dest.reshape(m, TOP_K), gates, g, beta)


def _mixer_ln(x2, bsz, s, p, emit_row_tiles):
    proj = _matmul(x2, p['w_in'], min(1024, x2.shape[0]), PROJ_TILE)
    proj3 = proj.reshape(bsz, s, PROJ_COLS)
    cos_slab, sin_slab, cos4, sin4 = _rope_tables(s)
    out_a = _mla(proj3, cos_slab, sin_slab, p['mla_q_norm'], p['w_q'], p['w_qr'], p['mla_kv_norm'], p['w_kv'],
                 p['mla_out_norm'])
    out_b = _retention(proj3, cos4, sin4)
    xbc_act = _dwconv(proj3, COL_XBC, SSD_CONV_CH, p['ssd_conv_w'], p['ssd_conv_b'], act=True)
    out_c = _ssd(proj3, xbc_act, p['ssd_dt_bias'], p['ssd_a_log'], p['ssd_d'], p['ssd_norm'])
    out_d = _hyena(proj3, p['hy_conv_w'], p['hy_conv_b'], p['hy_w1'], p['hy_b1'], p['hy_w2'], p['hy_b2'], p['hy_w3'],
                   p['hy_freq'], p['hy_bias'])
    m = bsz * s
    gw = GROUP_WIDTH
    return _out_proj_ln(out_a.reshape(m, gw), out_b.reshape(m, gw), out_c.reshape(m, gw), out_d.reshape(m, gw),
                        x2, p['w_out'], p['hy_out_norm'], p['ln1_g'], p['ln1_b'], emit_row_tiles)


_MIXER_KEYS = ('mla_q_norm', 'mla_kv_norm', 'mla_out_norm', 'ssd_conv_w', 'ssd_conv_b', 'ssd_dt_bias', 'ssd_a_log',
               'ssd_d', 'ssd_norm', 'hy_conv_w', 'hy_conv_b', 'hy_w1', 'hy_b1', 'hy_w2', 'hy_b2', 'hy_w3', 'hy_freq',
               'hy_bias', 'hy_out_norm', 'ln1_g', 'ln1_b')

MOE_TM = 1024


def kernel(x, w_in, mla_q_norm, mla_w_uq, mla_kv_norm, mla_w_ukv, mla_out_norm, ssd_conv_w, ssd_conv_b, ssd_dt_bias, ssd_a_log, ssd_d, ssd_norm, hy_conv_w, hy_conv_b, hy_w1, hy_b1, hy_w2, hy_b2, hy_w3, hy_freq, hy_bias, hy_out_norm, w_out, ln1_g, ln1_b, ln2_g, ln2_b, ffn_w_gate, ffn_w_up, ffn_w_down, moe_router, moe_w_gate, moe_w_up, moe_w_down):
    args = dict(locals())
    bsz, s, d = x.shape
    x2 = x.reshape(bsz * s, d)
    for layer in range(DEPTH):
        p = {k: args[k][layer] for k in _MIXER_KEYS}
        p['w_in'] = _prep_w_in(w_in[layer])
        p['w_q'], p['w_qr'] = _prep_w_uq(mla_w_uq[layer])
        p['w_kv'] = _prep_w_ukv(mla_w_ukv[layer])
        p['w_out'] = w_out[layer].astype(BF16)
        j = layer // 2
        if layer % 2 == 0:
            x2, = _mixer_ln(x2, bsz, s, p, False)
            x2 = _ffn_ln(x2, ffn_w_gate[j].astype(BF16), ffn_w_up[j].astype(BF16), ffn_w_down[j].astype(BF16),
                         ln2_g[layer], ln2_b[layer])
        else:
            x2, x_tiles = _mixer_ln(x2, bsz, s, p, True)
            x2 = _moe_ln(x2, x_tiles, moe_router[j], moe_w_gate[j].astype(BF16), moe_w_up[j].astype(BF16),
                         moe_w_down[j].astype(BF16), ln2_g[layer], ln2_b[layer], MOE_TM)
    return x2.reshape(bsz, s, d)
```

```python
import functools
import math

import numpy as np
import jax
import jax.numpy as jnp
from jax import lax
from jax.experimental import pallas as pl
from jax.experimental.pallas import tpu as pltpu

F32 = jnp.float32
BF16 = jnp.bfloat16
HIGHEST = lax.Precision.HIGHEST

D_MODEL = 2048
DEPTH = 2
GROUP_WIDTH = 512
MLA_HEADS = 4
MLA_NOPE = 128
MLA_ROPE = 64
MLA_V = 128
MLA_Q_LORA = 384
MLA_KV_LORA = 256
RET_HEADS = 4
RET_DV = 128
RET_DK = 64
RET_DECAY_EXP_FWD = 5.0
RET_DECAY_EXP_BWD = 5.5
SSD_HEADDIM = 64
SSD_HEADS = 8
SSD_GROUPS = 2
SSD_STATE = 128
SSD_CONV = 5
SSD_CONV_CH = 1024
HY_ORDER = 2
HY_WIDTH = 512
HY_SHORT = 3
HY_EMB = 33
HY_FILTER_HIDDEN = 64
HY_MIN_DECAY = math.log(1e-2) / 1.5
HY_MAX_DECAY = math.log(1e-2) / 0.3
N_EXPERTS = 8
TOP_K = 2
ROPE_BASE = 10000.0
ALPHA = (2 * DEPTH) ** 0.25

V7X_VMEM_BYTES = 64 * 1024 * 1024
LANES = 128
SUBLANES = 8

PROJ_COLS = 6400
COL_HY = 0
COL_QC = 1536
COL_RQ = 2048
COL_KVPE = 4096
COL_Z = 4608
COL_XBC = 5120
COL_DT = 6144
PROJ_TILE = 1280

RET_CHUNK = 256
SSD_CHUNK = 128
HY_ROW_TILE = 512
CONV_ROWS = 1024
RET_OUT_CHUNKS = 2
SSD_OUT_CHUNKS = 4
ROW_CHUNKS = D_MODEL // LANES
GATHER_ROWS = 1024
COMBINE_ROWS = 512
DMA_LOOP_UNROLL = 8


def _cparams(semantics, vmem_mb):
    assert vmem_mb * 1024 * 1024 < V7X_VMEM_BYTES
    return pltpu.CompilerParams(dimension_semantics=semantics, vmem_limit_bytes=vmem_mb * 1024 * 1024)


def _sigmoid(x):
    return 1.0 / (1.0 + jnp.exp(-x))


def _silu(x):
    return x * _sigmoid(x)


def _rms(x, w, eps=1e-6):
    return x * lax.rsqrt(jnp.mean(x * x, axis=-1, keepdims=True) + eps) * w


def _layer_norm(y, g, b, eps=1e-5):
    mu = jnp.mean(y, axis=-1, keepdims=True)
    d = y - mu
    var = jnp.mean(d * d, axis=-1, keepdims=True)
    return d * lax.rsqrt(var + eps) * g + b


def _dot(a, b):
    return jnp.dot(a, b, preferred_element_type=F32)


def _dot_nt(a, b):
    return lax.dot_general(a, b, (((1,), (1,)), ((), ())), preferred_element_type=F32)


def _dot_tn(a, b):
    return lax.dot_general(a, b, (((0,), (0,)), ((), ())), preferred_element_type=F32)


def _dot_f32(a, b):
    return jnp.dot(a, b, preferred_element_type=F32, precision=HIGHEST)


def _mm_body(x_ref, w_ref, o_ref, xb_ref):
    @pl.when(pl.program_id(1) == 0)
    def _():
        xb_ref[...] = x_ref[...].astype(BF16)

    o_ref[...] = _dot(xb_ref[...], w_ref[...])


def _matmul(x, w, tm, tn, vmem_mb=48):
    m, k = x.shape
    n = w.shape[1]
    assert m % tm == 0 and n % tn == 0
    return pl.pallas_call(
        _mm_body,
        grid=(m // tm, n // tn),
        in_specs=[pl.BlockSpec((tm, k), lambda i, j: (i, 0)),
                  pl.BlockSpec((k, tn), lambda i, j: (0, j))],
        out_specs=pl.BlockSpec((tm, tn), lambda i, j: (i, j)),
        out_shape=jax.ShapeDtypeStruct((m, n), F32),
        scratch_shapes=[pltpu.VMEM((tm, k), BF16)],
        compiler_params=_cparams(("parallel", "arbitrary"), vmem_mb),
    )(x, w)


def _rot_half_cols(w, heads):
    k = w.shape[0]
    w = w.reshape(k, heads, 2, 32)
    return jnp.stack([-w[:, :, 1], w[:, :, 0]], axis=2).reshape(k, heads * 64)


def _prep_w_in(w):
    k = w.shape[0]

    def sl(a, b):
        return w[:, a:b]

    def zc(n):
        return jnp.zeros((k, n), w.dtype)

    q_c, kv_c, k_pe = sl(0, 384), sl(384, 640), sl(640, 704)
    r_q, r_k, r_v, r_g = sl(704, 960), sl(960, 1216), sl(1216, 1728), sl(1728, 2240)
    m_z, m_xbc, m_dt, h_u = sl(2240, 2752), sl(2752, 3776), sl(3776, 3792), sl(3792, 5328)
    cols = [h_u, q_c, zc(128),
            r_q, r_k, _rot_half_cols(r_q, RET_HEADS), _rot_half_cols(r_k, RET_HEADS), r_v, r_g,
            kv_c, k_pe, zc(64), _rot_half_cols(k_pe, 1), zc(64),
            m_z, m_xbc, m_dt, zc(112), zc(PROJ_COLS - COL_DT - LANES)]
    out = jnp.concatenate(cols, axis=1).astype(BF16)
    assert out.shape[1] == PROJ_COLS
    return out


def _prep_w_uq(w):
    k = w.shape[0]
    w = w.reshape(k, MLA_HEADS, MLA_NOPE + MLA_ROPE)
    nope, rope = w[:, :, :MLA_NOPE], w[:, :, MLA_NOPE:]
    z = jnp.zeros((k, MLA_HEADS, 64), w.dtype)
    main = jnp.concatenate([nope, rope, z], axis=2).reshape(k, MLA_HEADS * 256)
    rr = _rot_half_cols(rope.reshape(k, MLA_HEADS * 64), MLA_HEADS).reshape(k, MLA_HEADS, 64)
    rot = jnp.concatenate([rr, z], axis=2).reshape(k, MLA_HEADS * 128)
    return main.astype(BF16), rot.astype(BF16)


def _prep_w_ukv(w):
    k = w.shape[0]
    w = w.reshape(k, MLA_HEADS, MLA_NOPE + MLA_V)
    return jnp.concatenate([w[:, :, :MLA_NOPE].reshape(k, -1), w[:, :, MLA_NOPE:].reshape(k, -1)], axis=1).astype(BF16)


def _rope_tables(s):
    half = 32
    inv_freq = ROPE_BASE ** (-jnp.arange(half, dtype=F32) * 2.0 / 64)
    ang = jnp.arange(s, dtype=F32)[:, None] * inv_freq[None, :]
    cos, sin = jnp.cos(ang), jnp.sin(ang)
    cos64 = jnp.concatenate([cos, cos], axis=1)
    sin64 = jnp.concatenate([sin, sin], axis=1)
    z = jnp.zeros((s, 64), F32)
    return (jnp.concatenate([cos64, z], axis=1), jnp.concatenate([sin64, z], axis=1),
            jnp.tile(cos64, (1, 4)), jnp.tile(sin64, (1, 4)))


def _ret_tables(n):
    heads = jnp.arange(RET_HEADS, dtype=F32)
    lgf = jnp.log1p(-jnp.exp2(-RET_DECAY_EXP_FWD - heads))
    lgb = jnp.log1p(-jnp.exp2(-RET_DECAY_EXP_BWD - heads))
    idx = jnp.arange(n, dtype=F32)

    def lanes(tab, width):
        return jnp.repeat(tab, width, axis=1)

    kdf = lanes(jnp.exp((n - 1.0 - idx)[:, None] * lgf), RET_DK)
    kdb = lanes(jnp.exp(idx[:, None] * lgb), RET_DK)
    qdf = lanes(jnp.exp((idx + 1.0)[:, None] * lgf), RET_DK)
    qdb = lanes(jnp.exp((n - idx)[:, None] * lgb), RET_DK)
    cdf = lanes(jnp.exp(n * lgf)[None, :], RET_DV)
    cdb = lanes(jnp.exp(n * lgb)[None, :], RET_DV)
    diff = idx[:, None] - idx[None, :]
    dec = jnp.where(diff[None] >= 0,
                    jnp.exp(jnp.maximum(diff, 0.0)[None] * lgf[:, None, None]),
                    jnp.exp(jnp.maximum(-diff, 0.0)[None] * lgb[:, None, None]))
    bd = (jnp.arange(RET_HEADS * RET_DK)[:, None] // RET_DK == jnp.arange(RET_HEADS * RET_DV)[None, :] // RET_DV)
    return kdf, kdb, qdf, qdb, cdf, cdb, dec.astype(F32), bd.astype(F32)


def _ssd_tables(n):
    tri = (jnp.arange(n)[:, None] >= jnp.arange(n)[None, :]).astype(F32)
    lane_head = jnp.arange(GROUP_WIDTH) // SSD_HEADDIM
    r = jnp.arange(LANES)
    ef = (r[:, None] == lane_head[None, :]).astype(BF16)
    eb = (r[:, None] == lane_head[None, :] + SSD_HEADS).astype(BF16)
    e = jnp.concatenate([ef, eb], axis=1)
    return tri, jnp.concatenate([e, e], axis=0)


def _dft_tables(l):
    f = jnp.arange(l, dtype=jnp.int32)
    k = (f[:, None] * f[None, :]) % (2 * l)
    ang = k.astype(F32) * (math.pi / l)
    c, s = jnp.cos(ang), jnp.sin(ang)
    sc = jnp.where(f == 0, 1.0, 2.0).astype(F32)[:, None] / (2.0 * l)
    nyq = jnp.where(f % 2 == 0, 1.0, -1.0).astype(F32)[None, :] / (2.0 * l)
    c_sc = jnp.concatenate([c * sc, nyq, jnp.zeros((7, l), F32)], axis=0)
    return c.astype(BF16), s.astype(BF16), c_sc.astype(BF16), (-s * sc).astype(BF16)


def _hyena_positions(l):
    t = jnp.linspace(0.0, 1.0, l, dtype=F32)[:, None]
    bands = (HY_EMB - 1) // 2
    ang = 2.0 * math.pi * jnp.arange(l, dtype=F32)[:, None] / l
    f = jnp.linspace(1e-4, bands - 1, bands, dtype=F32)[None, :]
    z = jnp.concatenate([t, jnp.cos(f * ang), -jnp.sin(f * ang), jnp.zeros((l, LANES - HY_EMB), F32)], axis=-1)
    deltas = jnp.abs(jnp.linspace(HY_MIN_DECAY, HY_MAX_DECAY, HY_WIDTH, dtype=F32))[None, :]
    return z, deltas


def _conv_body(taps, act, x_ref, p_ref, n_ref, w_ref, b_ref, o_ref):
    r = pl.program_id(1)
    tr = x_ref.shape[1]
    half = taps // 2
    prev = jnp.where(r > 0, p_ref[0], 0.0)
    nxt = jnp.where(r < pl.num_programs(1) - 1, n_ref[0], 0.0)
    win = jnp.concatenate([prev, x_ref[0], nxt], axis=0)
    rows = tr + 2 * SUBLANES
    acc = jnp.broadcast_to(b_ref[...], (tr, x_ref.shape[2]))
    for t in range(taps):
        sh = (half - t) % rows
        shifted = win if sh == 0 else pltpu.roll(win, sh, axis=0)
        acc = acc + shifted[SUBLANES:SUBLANES + tr] * w_ref[t:t + 1, :]
    if act:
        acc = _silu(acc)
    o_ref[0] = acc


def _dwconv(proj3, col0, width, w, b, act):
    bsz, s, _ = proj3.shape
    taps = w.shape[0]
    cb = 512
    tr = min(CONV_ROWS, s)
    off = col0 // cb
    nb8 = s // SUBLANES
    t8 = tr // SUBLANES
    return pl.pallas_call(
        functools.partial(_conv_body, taps, act),
        grid=(bsz, s // tr, width // cb),
        in_specs=[pl.BlockSpec((1, tr, cb), lambda i, r, c: (i, r, c + off)),
                  pl.BlockSpec((1, SUBLANES, cb), lambda i, r, c: (i, jnp.maximum(r * t8 - 1, 0), c + off)),
                  pl.BlockSpec((1, SUBLANES, cb), lambda i, r, c: (i, jnp.minimum((r + 1) * t8, nb8 - 1), c + off)),
                  pl.BlockSpec((taps, cb), lambda i, r, c: (0, c)),
                  pl.BlockSpec((1, cb), lambda i, r, c: (0, c))],
        out_specs=pl.BlockSpec((1, tr, cb), lambda i, r, c: (i, r, c)),
        out_shape=jax.ShapeDtypeStruct((bsz, s, width), F32),
        compiler_params=_cparams(("parallel", "parallel", "parallel"), 40),
    )(proj3, proj3, proj3, w, b.reshape(1, width))


def _mla_body(qc_ref, kvpe_ref, cq_ref, sq_ref, ck_ref, sk_ref, qn_ref, wq_ref, wqr_ref, kvn_ref, wkv_ref, on_ref,
              o_ref, k_scr, v_scr):
    @pl.when(pl.program_id(1) == 0)
    def _():
        kvpe = kvpe_ref[0]
        kvn = _rms(kvpe[:, :MLA_KV_LORA], kvn_ref[...]).astype(BF16)
        kpe = (kvpe[:, 256:384] * ck_ref[...] + kvpe[:, 384:512] * sk_ref[...]).astype(BF16)
        for h in range(MLA_HEADS):
            kn = _dot(kvn, wkv_ref[:, h * 128:(h + 1) * 128]).astype(BF16)
            k_scr[h] = jnp.concatenate([kn, kpe], axis=1)
            v_scr[h] = _dot(kvn, wkv_ref[:, 512 + h * 128:512 + (h + 1) * 128]).astype(BF16)

    scale = (MLA_NOPE + MLA_ROPE) ** -0.5 * math.log2(math.e)
    qn = _rms(qc_ref[0][:, :MLA_Q_LORA], qn_ref[...]).astype(BF16)
    outs = []
    for h in range(MLA_HEADS):
        qm = _dot(qn, wq_ref[:, h * 256:(h + 1) * 256])
        qr = _dot(qn, wqr_ref[:, h * 128:(h + 1) * 128])
        qpe = qm[:, 128:] * cq_ref[...] + qr * sq_ref[...]
        qh = (jnp.concatenate([qm[:, :128], qpe], axis=1) * scale).astype(BF16)
        sc = _dot_nt(qh, k_scr[h])
        p = jnp.exp2(sc - jnp.max(sc, axis=-1, keepdims=True))
        den = jnp.sum(p, axis=-1, keepdims=True)
        outs.append(_dot(p.astype(BF16), v_scr[h]) / den)
    o_ref[0] = _rms(jnp.concatenate(outs, axis=1), on_ref[...])


def _mla(proj3, cos_slab, sin_slab, q_norm, w_q, w_qr, kv_norm, w_kv, out_norm):
    bsz, s, _ = proj3.shape
    tq = min(512, s)
    const = lambda i, j: (0, 0)
    return pl.pallas_call(
        _mla_body,
        grid=(bsz, s // tq),
        in_specs=[pl.BlockSpec((1, tq, 512), lambda i, j: (i, j, COL_QC // 512)),
                  pl.BlockSpec((1, s, 512), lambda i, j: (i, 0, COL_KVPE // 512)),
                  pl.BlockSpec((tq, 128), lambda i, j: (j, 0)),
                  pl.BlockSpec((tq, 128), lambda i, j: (j, 0)),
                  pl.BlockSpec((s, 128), const),
                  pl.BlockSpec((s, 128), const),
                  pl.BlockSpec((1, MLA_Q_LORA), const),
                  pl.BlockSpec(w_q.shape, const),
                  pl.BlockSpec(w_qr.shape, const),
                  pl.BlockSpec((1, MLA_KV_LORA), const),
                  pl.BlockSpec(w_kv.shape, const),
                  pl.BlockSpec((1, GROUP_WIDTH), const)],
        out_specs=pl.BlockSpec((1, tq, GROUP_WIDTH), lambda i, j: (i, j, 0)),
        out_shape=jax.ShapeDtypeStruct((bsz, s, GROUP_WIDTH), F32),
        scratch_shapes=[pltpu.VMEM((MLA_HEADS, s, 256), BF16), pltpu.VMEM((MLA_HEADS, s, MLA_V), BF16)],
        compiler_params=_cparams(("parallel", "arbitrary"), 48),
    )(proj3, proj3, cos_slab, sin_slab, cos_slab, sin_slab, q_norm.reshape(1, -1), w_q, w_qr,
      kv_norm.reshape(1, -1), w_kv, out_norm.reshape(1, -1))


def _ret_state_body(nchunks, rk_ref, rkr_ref, rv_ref, cos_ref, sin_ref, kdf_ref, kdb_ref, cdf_ref, cdb_ref, bd_ref,
                    sf_ref, sb_ref, st_scr):
    n = kdf_ref.shape[0]

    def sweep(out_ref, kd_ref, cd_ref, chunk_of):
        st_scr[...] = jnp.zeros_like(st_scr)

        def body(t, carry):
            c = chunk_of(t)
            rows = pl.ds(pl.multiple_of(c * n, n), n)
            k = (rk_ref[0, rows, :] * cos_ref[rows, :] + rkr_ref[0, rows, :] * sin_ref[rows, :]) * (RET_DK ** -0.5)
            out_ref[0, c] = st_scr[...]
            new = _dot_tn((k * kd_ref[...]).astype(BF16), rv_ref[0, rows, :].astype(BF16))
            st_scr[...] = st_scr[...] * cd_ref[...] + new * bd_ref[...]
            return carry

        lax.fori_loop(0, nchunks, body, 0, unroll=2)

    sweep(sf_ref, kdf_ref, cdf_ref, lambda t: t)
    sweep(sb_ref, kdb_ref, cdb_ref, lambda t: nchunks - 1 - t)


def _ret_out_chunk(rq, rk, rqr, rkr, rv, rg, cos, sin, sf, sb, qdf_ref, qdb_ref, dec_ref):
    q = rq * cos + rqr * sin
    kb = ((rk * cos + rkr * sin) * (RET_DK ** -0.5)).astype(BF16)
    vb = rv.astype(BF16)
    cross = (_dot((q * qdf_ref[...]).astype(BF16), sf.astype(BF16))
             + _dot((q * qdb_ref[...]).astype(BF16), sb.astype(BF16)))
    lane_head = lax.broadcasted_iota(jnp.int32, (1, RET_HEADS * RET_DK), 1) // RET_DK
    outs = []
    for h in range(RET_HEADS):
        qh = jnp.where(lane_head == h, q, 0.0).astype(BF16)
        sc = _dot_nt(qh, kb) * dec_ref[h]
        y = _dot(sc.astype(BF16), vb[:, h * RET_DV:(h + 1) * RET_DV]) + cross[:, h * RET_DV:(h + 1) * RET_DV]
        mu = jnp.mean(y, axis=-1, keepdims=True)
        d = y - mu
        var = jnp.mean(d * d, axis=-1, keepdims=True)
        outs.append(d * lax.rsqrt(var + 1e-6))
    return jnp.concatenate(outs, axis=1) * _silu(rg)


def _ret_out_body(rq_ref, rk_ref, rqr_ref, rkr_ref, rv_ref, rg_ref, cos_ref, sin_ref, qdf_ref, qdb_ref, dec_ref,
                  sf_ref, sb_ref, o_ref):
    n = qdf_ref.shape[0]

    def body(k, carry):
        rows = pl.ds(pl.multiple_of(k * n, n), n)
        o_ref[0, rows, :] = _ret_out_chunk(rq_ref[0, rows, :], rk_ref[0, rows, :], rqr_ref[0, rows, :],
                                           rkr_ref[0, rows, :], rv_ref[0, rows, :], rg_ref[0, rows, :],
                                           cos_ref[rows, :], sin_ref[rows, :], sf_ref[0, k], sb_ref[0, k],
                                           qdf_ref, qdb_ref, dec_ref)
        return carry

    lax.fori_loop(0, o_ref.shape[1] // n, body, 0, unroll=2)


def _retention(proj3, cos4, sin4):
    bsz, s, _ = proj3.shape
    n = min(RET_CHUNK, s)
    c = s // n
    kdf, kdb, qdf, qdb, cdf, cdb, dec, bd = _ret_tables(n)
    c256 = COL_RQ // 256
    c512 = COL_RQ // 512
    const1 = lambda i: (0, 0)

    state_shape = jax.ShapeDtypeStruct((bsz, c, RET_HEADS * RET_DK, GROUP_WIDTH), F32)
    state_spec = pl.BlockSpec((1, c, RET_HEADS * RET_DK, GROUP_WIDTH), lambda i: (i, 0, 0, 0))
    sf, sb = pl.pallas_call(
        functools.partial(_ret_state_body, c),
        grid=(bsz,),
        in_specs=[pl.BlockSpec((1, s, 256), lambda i: (i, 0, c256 + 1)),
                  pl.BlockSpec((1, s, 256), lambda i: (i, 0, c256 + 3)),
                  pl.BlockSpec((1, s, 512), lambda i: (i, 0, c512 + 2)),
                  pl.BlockSpec((s, 256), const1), pl.BlockSpec((s, 256), const1),
                  pl.BlockSpec((n, 256), const1), pl.BlockSpec((n, 256), const1),
                  pl.BlockSpec((1, 512), const1), pl.BlockSpec((1, 512), const1),
                  pl.BlockSpec((256, 512), const1)],
        out_specs=[state_spec, state_spec],
        out_shape=[state_shape, state_shape],
        scratch_shapes=[pltpu.VMEM((RET_HEADS * RET_DK, GROUP_WIDTH), F32)],
        compiler_params=_cparams(("parallel",), 48),
    )(proj3, proj3, proj3, cos4, sin4, kdf, kdb, cdf, cdb, bd)

    k = min(RET_OUT_CHUNKS, c)
    const3 = lambda i, j: (0, 0, 0)
    const2 = lambda i, j: (0, 0)
    return pl.pallas_call(
        _ret_out_body,
        grid=(bsz, c // k),
        in_specs=[pl.BlockSpec((1, k * n, 256), lambda i, j: (i, j, c256)),
                  pl.BlockSpec((1, k * n, 256), lambda i, j: (i, j, c256 + 1)),
                  pl.BlockSpec((1, k * n, 256), lambda i, j: (i, j, c256 + 2)),
                  pl.BlockSpec((1, k * n, 256), lambda i, j: (i, j, c256 + 3)),
                  pl.BlockSpec((1, k * n, 512), lambda i, j: (i, j, c512 + 2)),
                  pl.BlockSpec((1, k * n, 512), lambda i, j: (i, j, c512 + 3)),
                  pl.BlockSpec((k * n, 256), lambda i, j: (j, 0)),
                  pl.BlockSpec((k * n, 256), lambda i, j: (j, 0)),
                  pl.BlockSpec((n, 256), const2), pl.BlockSpec((n, 256), const2),
                  pl.BlockSpec((RET_HEADS, n, n), const3),
                  pl.BlockSpec((1, k, 256, 512), lambda i, j: (i, j, 0, 0)),
                  pl.BlockSpec((1, k, 256, 512), lambda i, j: (i, j, 0, 0))],
        out_specs=pl.BlockSpec((1, k * n, GROUP_WIDTH), lambda i, j: (i, j, 0)),
        out_shape=jax.ShapeDtypeStruct((bsz, s, GROUP_WIDTH), F32),
        compiler_params=_cparams(("parallel", "parallel"), 32),
    )(proj3, proj3, proj3, proj3, proj3, proj3, cos4, sin4, qdf, qdb, dec, sf, sb)


def _ssd_decays(dt_rows, dtb_ref, a_ref, tri_ref):
    dt_raw = dt_rows + dtb_ref[...]
    dt = jnp.maximum(dt_raw, 0.0) + jnp.log1p(jnp.exp(-jnp.abs(dt_raw)))
    la = dt * a_ref[...]
    return dt, la, _dot_f32(tri_ref[...], la)


def _expand_heads(v, e):
    hi = v.astype(BF16)
    lo = (v - hi.astype(F32)).astype(BF16)
    return _dot(jnp.concatenate([hi, lo], axis=1), e)


def _ssd_state_body(nchunks, xbc_ref, dt_ref, dtb_ref, a_ref, tri_ref, e_ref, sf_ref, sb_ref, st_scr):
    n = SSD_CHUNK
    w = GROUP_WIDTH
    fwd_lane = lax.broadcasted_iota(jnp.int32, (1, LANES), 1) < SSD_HEADS

    def sweep(out_ref, lo, chunk_of):
        st_scr[...] = jnp.zeros_like(st_scr)

        def body(t, carry):
            c = chunk_of(t)
            rows = pl.ds(pl.multiple_of(c * n, n), n)
            dt, la, cs = _ssd_decays(dt_ref[0, rows, :], dtb_ref, a_ref, tri_ref)
            tot = cs[n - 1:n, :]
            wgt = dt * jnp.exp(jnp.where(fwd_lane, tot - cs, cs - la))
            both = _expand_heads(jnp.concatenate([wgt, jnp.broadcast_to(jnp.exp(tot), (SUBLANES, LANES))], axis=0),
                                 e_ref[:, lo:lo + w])
            out_ref[0, c] = st_scr[...]
            xw = (xbc_ref[0, rows, 0:w] * both[:n]).astype(BF16)
            new = []
            for g in range(SSD_GROUPS):
                bg = xbc_ref[0, rows, w + g * SSD_STATE:w + (g + 1) * SSD_STATE].astype(BF16)
                new.append(_dot_tn(bg, xw[:, g * 256:(g + 1) * 256]))
            st_scr[...] = st_scr[...] * both[n:n + 1] + jnp.concatenate(new, axis=1)
            return carry

        lax.fori_loop(0, nchunks, body, 0, unroll=4)

    sweep(sf_ref, 0, lambda t: t)
    sweep(sb_ref, w, lambda t: nchunks - 1 - t)


def _ssd_out_chunk(xbc, dt_rows, sf, sb, dtb_ref, a_ref, tri_ref, e_ref, dskip_ref):
    n = xbc.shape[0]
    w = GROUP_WIDTH
    dt, la, cs = _ssd_decays(dt_rows, dtb_ref, a_ref, tri_ref)
    ecs = cs - la
    fwd_lane = lax.broadcasted_iota(jnp.int32, (1, LANES), 1) < SSD_HEADS
    carry = jnp.exp(jnp.where(fwd_lane, cs, cs[n - 1:n, :] - ecs))
    both = _expand_heads(jnp.concatenate([dt, carry], axis=0), e_ref[...])
    dt_e = both[:n]
    carry_e = both[n:]
    xs = xbc[:, :w]
    xdt_f = xs * dt_e[:, :w]
    xdt_b = xs * dt_e[:, w:]
    cs_t = cs.T
    ecs_t = ecs.T
    ii = lax.broadcasted_iota(jnp.int32, (n, n), 0)
    jj = lax.broadcasted_iota(jnp.int32, (n, n), 1)
    low = lax.broadcasted_iota(jnp.int32, (1, LANES), 1) < SSD_HEADDIM
    neg = -1e30
    ydiag = []
    yoff = []
    for g in range(SSD_GROUPS):
        bg = xbc[:, w + g * SSD_STATE:w + (g + 1) * SSD_STATE].astype(BF16)
        cg = xbc[:, w + 256 + g * SSD_STATE:w + 256 + (g + 1) * SSD_STATE].astype(BF16)
        cb = _dot_nt(cg, bg)
        for pair in range(2):
            p = 2 * g + pair
            lhs = []
            for h in (2 * p, 2 * p + 1):
                lf = jnp.exp(jnp.where(ii >= jj, cs[:, h:h + 1] - cs_t[h:h + 1, :], neg))
                lhs.append((cb * lf).astype(BF16))
            for h in (2 * p, 2 * p + 1):
                hb = SSD_HEADS + h
                lb = jnp.exp(jnp.where(jj > ii, ecs_t[hb:hb + 1, :] - ecs[:, hb:hb + 1], neg))
                lhs.append((cb * lb).astype(BF16))
            xf = xdt_f[:, p * LANES:(p + 1) * LANES]
            xb = xdt_b[:, p * LANES:(p + 1) * LANES]
            rhs = jnp.concatenate([jnp.where(low, xf, 0.0), jnp.where(low, 0.0, xf),
                                   jnp.where(low, xb, 0.0), jnp.where(low, 0.0, xb)], axis=0).astype(BF16)
            ydiag.append(_dot(jnp.concatenate(lhs, axis=1), rhs))
        yoff.append(_dot(cg, sf[:, g * 256:(g + 1) * 256].astype(BF16)) * carry_e[:, g * 256:(g + 1) * 256]
                    + _dot(cg, sb[:, g * 256:(g + 1) * 256].astype(BF16)) * carry_e[:, w + g * 256:w + (g + 1) * 256])
    return jnp.concatenate(ydiag, axis=1) + jnp.concatenate(yoff, axis=1) + xs * dskip_ref[...]


def _ssd_out_body(xbc_ref, dt_ref, z_ref, dtb_ref, a_ref, tri_ref, e_ref, dskip_ref, nw_ref, sf_ref, sb_ref, o_ref):
    n = SSD_CHUNK

    def body(k, carry):
        rows = pl.ds(pl.multiple_of(k * n, n), n)
        y = _ssd_out_chunk(xbc_ref[0, rows, :], dt_ref[0, rows, :], sf_ref[0, k], sb_ref[0, k], dtb_ref, a_ref, tri_ref,
                           e_ref, dskip_ref)
        o_ref[0, rows, :] = _rms(y * _silu(z_ref[0, rows, :]), nw_ref[...])
        return carry

    lax.fori_loop(0, o_ref.shape[1] // n, body, 0, unroll=2)


def _ssd(proj3, xbc_act, dt_bias, a_log, d_skip, norm_w):
    bsz, s, _ = proj3.shape
    n = SSD_CHUNK
    c = s // n
    tri, e = _ssd_tables(n)
    pad = jnp.zeros((LANES - 2 * SSD_HEADS,), F32)
    dtb = jnp.concatenate([dt_bias.reshape(-1), pad]).reshape(1, LANES)
    a = jnp.concatenate([-jnp.exp(a_log.reshape(-1)), pad]).reshape(1, LANES)
    dskip = jnp.repeat(d_skip, SSD_HEADDIM).reshape(1, GROUP_WIDTH)
    cdt = COL_DT // LANES
    const1 = lambda i: (0, 0)

    state_shape = jax.ShapeDtypeStruct((bsz, c, SSD_STATE, GROUP_WIDTH), F32)
    state_spec = pl.BlockSpec((1, c, SSD_STATE, GROUP_WIDTH), lambda i: (i, 0, 0, 0))
    sf, sb = pl.pallas_call(
        functools.partial(_ssd_state_body, c),
        grid=(bsz,),
        in_specs=[pl.BlockSpec((1, s, SSD_CONV_CH), lambda i: (i, 0, 0)),
                  pl.BlockSpec((1, s, LANES), lambda i: (i, 0, cdt)),
                  pl.BlockSpec((1, LANES), const1), pl.BlockSpec((1, LANES), const1),
                  pl.BlockSpec((n, n), const1), pl.BlockSpec((2 * LANES, 2 * GROUP_WIDTH), const1)],
        out_specs=[state_spec, state_spec],
        out_shape=[state_shape, state_shape],
        scratch_shapes=[pltpu.VMEM((SSD_STATE, GROUP_WIDTH), F32)],
        compiler_params=_cparams(("parallel",), 48),
    )(xbc_act, proj3, dtb, a, tri, e)

    k = min(SSD_OUT_CHUNKS, c)
    const2 = lambda i, j: (0, 0)
    return pl.pallas_call(
        _ssd_out_body,
        grid=(bsz, c // k),
        in_specs=[pl.BlockSpec((1, k * n, SSD_CONV_CH), lambda i, j: (i, j, 0)),
                  pl.BlockSpec((1, k * n, LANES), lambda i, j: (i, j, cdt)),
                  pl.BlockSpec((1, k * n, GROUP_WIDTH), lambda i, j: (i, j, COL_Z // GROUP_WIDTH)),
                  pl.BlockSpec((1, LANES), const2), pl.BlockSpec((1, LANES), const2),
                  pl.BlockSpec((n, n), const2), pl.BlockSpec((2 * LANES, 2 * GROUP_WIDTH), const2),
                  pl.BlockSpec((1, GROUP_WIDTH), const2), pl.BlockSpec((1, GROUP_WIDTH), const2),
                  pl.BlockSpec((1, k, SSD_STATE, GROUP_WIDTH), lambda i, j: (i, j, 0, 0)),
                  pl.BlockSpec((1, k, SSD_STATE, GROUP_WIDTH), lambda i, j: (i, j, 0, 0))],
        out_specs=pl.BlockSpec((1, k * n, GROUP_WIDTH), lambda i, j: (i, j, 0)),
        out_shape=jax.ShapeDtypeStruct((bsz, s, GROUP_WIDTH), F32),
        compiler_params=_cparams(("parallel", "parallel"), 32),
    )(xbc_act, proj3, proj3, dtb, a, tri, e, dskip, norm_w.reshape(1, -1), sf, sb)


def _hy_filter_body(z_ref, w1_ref, b1_ref, w2_ref, b2_ref, w3_ref, fr_ref, dl_ref, hs_ref, hd_ref):
    tl = z_ref.shape[0]
    z = z_ref[...]
    hid = jnp.sin(fr_ref[0:1, :] * (_dot_f32(z, w1_ref[...]) + b1_ref[...]))
    hid = jnp.sin(fr_ref[1:2, :] * (_dot_f32(hid, w2_ref[...]) + b2_ref[...]))
    filt = _dot_f32(hid, w3_ref[...])
    dec = jnp.exp(-z[:, 0:1] * dl_ref[...])
    row = pl.program_id(0) * tl + lax.broadcasted_iota(jnp.int32, (tl, 1), 0)
    for o in range(HY_ORDER):
        base = o * 2 * HY_WIDTH
        hf = filt[:, base:base + HY_WIDTH] * dec
        hb = jnp.where(row == 0, 0.0, filt[:, base + HY_WIDTH:base + 2 * HY_WIDTH] * dec)
        hs_ref[:, o * HY_WIDTH:(o + 1) * HY_WIDTH] = hf + hb
        hd_ref[:, o * HY_WIDTH:(o + 1) * HY_WIDTH] = hf - hb


def _hy_filters(l, w1, b1, w2, b2, w3, freq):
    z, deltas = _hyena_positions(l)
    hid = HY_FILTER_HIDDEN
    w1p = jnp.zeros((LANES, LANES), F32).at[:HY_EMB, :hid].set(w1)
    w2p = jnp.zeros((LANES, LANES), F32).at[:hid, :hid].set(w2)
    w3p = jnp.zeros((LANES, w3.shape[1]), F32).at[:hid].set(w3)
    b1p = jnp.zeros((1, LANES), F32).at[0, :hid].set(b1)
    b2p = jnp.zeros((1, LANES), F32).at[0, :hid].set(b2)
    frp = jnp.zeros((2, LANES), F32).at[:, :hid].set(freq)
    tl = min(256, l)
    ncol = HY_ORDER * HY_WIDTH
    const = lambda i: (0, 0)
    return pl.pallas_call(
        _hy_filter_body,
        grid=(l // tl,),
        in_specs=[pl.BlockSpec((tl, LANES), lambda i: (i, 0)),
                  pl.BlockSpec((LANES, LANES), const), pl.BlockSpec((1, LANES), const),
                  pl.BlockSpec((LANES, LANES), const), pl.BlockSpec((1, LANES), const),
                  pl.BlockSpec((LANES, w3.shape[1]), const), pl.BlockSpec((2, LANES), const),
                  pl.BlockSpec((1, HY_WIDTH), const)],
        out_specs=[pl.BlockSpec((tl, ncol), lambda i: (i, 0)), pl.BlockSpec((tl, ncol), lambda i: (i, 0))],
        out_shape=[jax.ShapeDtypeStruct((l, ncol), F32), jax.ShapeDtypeStruct((l, ncol), F32)],
        compiler_params=_cparams(("parallel",), 32),
    )(z, w1p, b1p, w2p, b2p, w3p, frp, deltas)


def _short_conv_rows(x_ref, w_ref, b_ref, r0, n):
    l = x_ref.shape[1]
    lo = max(r0 - SUBLANES, 0)
    hi = min(r0 + n + SUBLANES, l)
    win = x_ref[0, lo:hi, :]
    pad = jnp.zeros((SUBLANES, win.shape[1]), F32)
    if r0 == 0:
        win = jnp.concatenate([pad, win], axis=0)
    if r0 + n == l:
        win = jnp.concatenate([win, pad], axis=0)
    rows = n + 2 * SUBLANES
    prev = pltpu.roll(win, 1, axis=0)[SUBLANES:SUBLANES + n]
    nxt = pltpu.roll(win, rows - 1, axis=0)[SUBLANES:SUBLANES + n]
    return prev * w_ref[0:1, :] + win[SUBLANES:SUBLANES + n] * w_ref[1:2, :] + nxt * w_ref[2:3, :] + b_ref[...]


def _hy_conv_body(u0_ref, u1_ref, u2_ref, cw0_ref, cw1_ref, cw2_ref, cb0_ref, cb1_ref, cb2_ref, c_ref, s_ref,
                  hr0_ref, hi0_ref, hr1_ref, hi1_ref, b0_ref, b1_ref, o_ref, xb_scr, pre_scr, pim_scr, z_scr):
    l = u0_ref.shape[1]
    tf = min(HY_ROW_TILE, l)
    ntile = l // tf
    sgn = jnp.where(lax.broadcasted_iota(jnp.int32, (l, 1), 0) % 2 == 0, 1.0, -1.0)

    def long_conv(gate_ref, gw_ref, gb_ref, hr_ref, hi_ref, b_ref, out_ref):
        x = z_scr[...]
        xb_scr[...] = x.astype(BF16)
        x_nyq = jnp.sum(x * sgn, axis=0, keepdims=True) * hr_ref[l:l + 1, :]
        for r in range(ntile):
            rows = pl.ds(r * tf, tf)
            xc = _dot(c_ref[rows, :], xb_scr[...])
            xs = _dot(s_ref[rows, :], xb_scr[...])
            hre = hr_ref[rows, :]
            him = hi_ref[rows, :]
            pre_scr[rows, :] = (xc * hre + xs * him).astype(BF16)
            pim_scr[rows, :] = (xc * him - xs * hre).astype(BF16)
        for r in range(ntile):
            rows = pl.ds(r * tf, tf)
            y = _dot(c_ref[rows, :], pre_scr[...]) - _dot(s_ref[rows, :], pim_scr[...])
            y = y + sgn[r * tf:(r + 1) * tf] * x_nyq
            gate = _short_conv_rows(gate_ref, gw_ref, gb_ref, r * tf, tf)
            out_ref[rows, :] = gate * (y + z_scr[rows, :] * b_ref[0])

    for r in range(ntile):
        z_scr[pl.ds(r * tf, tf), :] = _short_conv_rows(u0_ref, cw0_ref, cb0_ref, r * tf, tf)
    long_conv(u1_ref, cw1_ref, cb1_ref, hr0_ref, hi0_ref, b0_ref, z_scr)
    long_conv(u2_ref, cw2_ref, cb2_ref, hr1_ref, hi1_ref, b1_ref, o_ref.at[0])


def _hy_long_conv(proj3, conv_w, conv_b, cmat, smat, hre, him, bias):
    bsz, l, _ = proj3.shape
    tc = 256
    nb = HY_WIDTH // tc
    off = COL_HY // tc
    const = lambda j, i: (0, 0)
    single = pl.Buffered(1)
    bias3 = bias.reshape(HY_ORDER, 1, HY_WIDTH)
    conv_b2 = conv_b.reshape(1, -1)

    def part(k):
        return [pl.BlockSpec((1, l, tc), lambda j, i: (i, 0, off + k * nb + j))]

    def part_w(k):
        return [pl.BlockSpec((HY_SHORT, tc), lambda j, i: (0, k * nb + j))]

    def part_b(k):
        return [pl.BlockSpec((1, tc), lambda j, i: (0, k * nb + j))]

    return pl.pallas_call(
        _hy_conv_body,
        grid=(nb, bsz),
        in_specs=part(0) + part(1) + part(2) + part_w(0) + part_w(1) + part_w(2) + part_b(0) + part_b(1) + part_b(2) + [
            pl.BlockSpec((l, l), const, pipeline_mode=single),
            pl.BlockSpec((l, l), const, pipeline_mode=single),
            pl.BlockSpec((l + SUBLANES, tc), lambda j, i: (0, j), pipeline_mode=single),
            pl.BlockSpec((l, tc), lambda j, i: (0, j), pipeline_mode=single),
            pl.BlockSpec((l + SUBLANES, tc), lambda j, i: (0, nb + j), pipeline_mode=single),
            pl.BlockSpec((l, tc), lambda j, i: (0, nb + j), pipeline_mode=single),
            pl.BlockSpec((1, 1, tc), lambda j, i: (0, 0, j)),
            pl.BlockSpec((1, 1, tc), lambda j, i: (1, 0, j))],
        out_specs=pl.BlockSpec((1, l, tc), lambda j, i: (i, 0, j)),
        out_shape=jax.ShapeDtypeStruct((bsz, l, HY_WIDTH), F32),
        scratch_shapes=[pltpu.VMEM((l, tc), BF16), pltpu.VMEM((l, tc), BF16), pltpu.VMEM((l, tc), BF16),
                        pltpu.VMEM((l, tc), F32)],
        compiler_params=_cparams(("parallel", "parallel"), 56),
    )(proj3, proj3, proj3, conv_w, conv_w, conv_w, conv_b2, conv_b2, conv_b2, cmat, smat, hre, him, hre, him,
      bias3, bias3)


def _hyena(proj3, conv_w, conv_b, w1, b1, w2, b2, w3, freq, bias):
    l = proj3.shape[1]
    hs, hd = _hy_filters(l, w1, b1, w2, b2, w3, freq)
    cmat, smat, c_sc, s_sc_neg = _dft_tables(l)
    tn = 256
    hre = _matmul(c_sc, hs.astype(BF16), l + SUBLANES, tn)
    him = _matmul(s_sc_neg, hd.astype(BF16), l, tn)
    return _hy_long_conv(proj3, conv_w, conv_b, cmat, smat, hre, him, bias)


def _out_proj_body(a_ref, b_ref, c_ref, d_ref, x_ref, w_ref, dn_ref, g_ref, beta_ref, o_ref, *rows_ref):
    gw = GROUP_WIDTH
    h = _dot(a_ref[...].astype(BF16), w_ref[0:gw, :])
    h += _dot(b_ref[...].astype(BF16), w_ref[gw:2 * gw, :])
    h += _dot(c_ref[...].astype(BF16), w_ref[2 * gw:3 * gw, :])
    h += _dot(_rms(d_ref[...], dn_ref[...]).astype(BF16), w_ref[3 * gw:4 * gw, :])
    y = _layer_norm(ALPHA * x_ref[...] + h, g_ref[...], beta_ref[...])
    o_ref[...] = y
    if rows_ref:
        _store_row_tiles(rows_ref[0], y)


def _store_row_tiles(dst_ref, y):
    rows = y.shape[0]
    for c in range(ROW_CHUNKS):
        dst_ref[pl.ds(c, rows, stride=ROW_CHUNKS), :] = y[:, c * LANES:(c + 1) * LANES]


def _load_row_tiles(src_ref, first, rows):
    return jnp.concatenate([src_ref[pl.ds(first * ROW_CHUNKS + c, rows, stride=ROW_CHUNKS), :]
                            for c in range(ROW_CHUNKS)], axis=1)


def _out_proj_ln(a, b, c, d, x, w_out, hy_norm, g, beta, emit_row_tiles):
    m = x.shape[0]
    tm = min(512, m)
    gw = GROUP_WIDTH
    const = lambda i: (0, 0)
    row = lambda i: (i, 0)
    out_specs = [pl.BlockSpec((tm, D_MODEL), row)]
    out_shape = [jax.ShapeDtypeStruct((m, D_MODEL), F32)]
    if emit_row_tiles:
        out_specs.append(pl.BlockSpec((tm * ROW_CHUNKS, LANES), row))
        out_shape.append(jax.ShapeDtypeStruct((m * ROW_CHUNKS, LANES), F32))
    return pl.pallas_call(
        _out_proj_body,
        grid=(m // tm,),
        in_specs=[pl.BlockSpec((tm, gw), row), pl.BlockSpec((tm, gw), row), pl.BlockSpec((tm, gw), row),
                  pl.BlockSpec((tm, gw), row), pl.BlockSpec((tm, D_MODEL), row),
                  pl.BlockSpec((D_MODEL, D_MODEL), const), pl.BlockSpec((1, gw), const),
                  pl.BlockSpec((1, D_MODEL), const), pl.BlockSpec((1, D_MODEL), const)],
        out_specs=out_specs,
        out_shape=out_shape,
        compiler_params=_cparams(("parallel",), 48),
    )(a, b, c, d, x, w_out, hy_norm.reshape(1, -1), g.reshape(1, -1), beta.reshape(1, -1))


def _ffn_body(x_ref, wg_ref, wu_ref, wd_ref, g_ref, beta_ref, o_ref, xb_ref, acc_ref):
    j = pl.program_id(1)

    @pl.when(j == 0)
    def _():
        xb_ref[...] = x_ref[...].astype(BF16)
        acc_ref[...] = jnp.zeros_like(acc_ref)

    xb = xb_ref[...]
    hidden = (_silu(_dot(xb, wg_ref[...])) * _dot(xb, wu_ref[...])).astype(BF16)
    acc_ref[...] += _dot(hidden, wd_ref[...])

    @pl.when(j == pl.num_programs(1) - 1)
    def _():
        o_ref[...] = _layer_norm(ALPHA * x_ref[...] + acc_ref[...], g_ref[...], beta_ref[...])


def _ffn_ln(x, wg, wu, wd, g, beta):
    m = x.shape[0]
    dff = wg.shape[1]
    tm = min(512, m)
    tf = 512
    const = lambda i, j: (0, 0)
    return pl.pallas_call(
        _ffn_body,
        grid=(m // tm, dff // tf),
        in_specs=[pl.BlockSpec((tm, D_MODEL), lambda i, j: (i, 0)),
                  pl.BlockSpec((D_MODEL, tf), lambda i, j: (0, j)),
                  pl.BlockSpec((D_MODEL, tf), lambda i, j: (0, j)),
                  pl.BlockSpec((tf, D_MODEL), lambda i, j: (j, 0)),
                  pl.BlockSpec((1, D_MODEL), const), pl.BlockSpec((1, D_MODEL), const)],
        out_specs=pl.BlockSpec((tm, D_MODEL), lambda i, j: (i, 0)),
        out_shape=jax.ShapeDtypeStruct((m, D_MODEL), F32),
        scratch_shapes=[pltpu.VMEM((tm, D_MODEL), BF16), pltpu.VMEM((tm, D_MODEL), F32)],
        compiler_params=_cparams(("parallel", "arbitrary"), 48),
    )(x, wg, wu, wd, g.reshape(1, -1), beta.reshape(1, -1))


def _router_body(x_ref, w_ref, idx_ref, gate_ref):
    logits = _dot_f32(x_ref[...], w_ref[...])
    lane = lax.broadcasted_iota(jnp.int32, logits.shape, 1)
    neg = -jnp.inf
    logits = jnp.where(lane < N_EXPERTS, logits, neg)
    m1 = jnp.max(logits, axis=-1, keepdims=True)
    i1 = jnp.min(jnp.where(logits == m1, lane, LANES), axis=-1, keepdims=True)
    rest = jnp.where(lane == i1, neg, logits)
    m2 = jnp.max(rest, axis=-1, keepdims=True)
    i2 = jnp.min(jnp.where(rest == m2, lane, LANES), axis=-1, keepdims=True)
    e2 = jnp.exp(m2 - m1)
    den = 1.0 + e2
    idx_ref[...] = jnp.where(lane == 0, i1, jnp.where(lane == 1, i2, 0))
    gate_ref[...] = jnp.where(lane == 0, 1.0 / den, jnp.where(lane == 1, e2 / den, 0.0))


def _router(x, w_router):
    m = x.shape[0]
    tm = min(512, m)
    wp = jnp.zeros((D_MODEL, LANES), F32).at[:, :N_EXPERTS].set(w_router)
    return pl.pallas_call(
        _router_body,
        grid=(m // tm,),
        in_specs=[pl.BlockSpec((tm, D_MODEL), lambda i: (i, 0)), pl.BlockSpec((D_MODEL, LANES), lambda i: (0, 0))],
        out_specs=[pl.BlockSpec((tm, LANES), lambda i: (i, 0)), pl.BlockSpec((tm, LANES), lambda i: (i, 0))],
        out_shape=[jax.ShapeDtypeStruct((m, LANES), jnp.int32), jax.ShapeDtypeStruct((m, LANES), F32)],
        compiler_params=_cparams(("parallel",), 32),
    )(x, wp)


def _row_copy(src_ref, src_row, buf_ref, buf_row, sem):
    src = src_ref.at[pl.ds(pl.multiple_of(src_row * ROW_CHUNKS, ROW_CHUNKS), ROW_CHUNKS)]
    dst = buf_ref.at[pl.ds(pl.multiple_of(buf_row * ROW_CHUNKS, ROW_CHUNKS), ROW_CHUNKS)]
    return pltpu.make_async_copy(src, dst, sem)


def _start_row_gather(idx_ref, n, src_ref, buf_ref, sem):
    def start(r, carry):
        _row_copy(src_ref, idx_ref[0, 0, r], buf_ref, r, sem).start()
        return carry

    lax.fori_loop(0, n, start, 0, unroll=DMA_LOOP_UNROLL)


def _wait_row_gather(n, src_ref, buf_ref, sem):
    def wait(r, carry):
        _row_copy(src_ref, 0, buf_ref, r, sem).wait()
        return carry

    lax.fori_loop(0, n, wait, 0, unroll=DMA_LOOP_UNROLL)


def _pipelined_row_gather(idx_cur_ref, idx_next_ref, n, src_ref, buf, sems):
    i = pl.program_id(0)
    slot = i % 2

    @pl.when(i == 0)
    def _():
        _start_row_gather(idx_cur_ref, n, src_ref, buf.at[0], sems.at[0])

    @pl.when(i + 1 < pl.num_programs(0))
    def _():
        _start_row_gather(idx_next_ref, n, src_ref, buf.at[1 - slot], sems.at[1 - slot])

    _wait_row_gather(n, src_ref, buf.at[slot], sems.at[slot])
    return slot


def _idx_specs(n, nblk):
    return [pl.BlockSpec((1, 1, n), lambda i: (i, 0, 0), memory_space=pltpu.SMEM),
            pl.BlockSpec((1, 1, n), lambda i: (jnp.minimum(i + 1, nblk - 1), 0, 0), memory_space=pltpu.SMEM)]


def _gather_x_body(idx_cur_ref, idx_next_ref, src_ref, o_ref, buf, sems):
    rows = o_ref.shape[0]
    slot = _pipelined_row_gather(idx_cur_ref, idx_next_ref, rows, src_ref, buf, sems)
    o_ref[...] = _load_row_tiles(buf.at[slot], 0, rows).astype(BF16)


def _gather_x(x_tiles, slot_tok):
    cap = slot_tok.shape[0]
    rows = GATHER_ROWS
    assert cap % rows == 0
    nblk = cap // rows
    idx = slot_tok.reshape(nblk, 1, rows)
    return pl.pallas_call(
        _gather_x_body,
        grid=(nblk,),
        in_specs=_idx_specs(rows, nblk) + [pl.BlockSpec(memory_space=pl.ANY)],
        out_specs=pl.BlockSpec((rows, D_MODEL), lambda i: (i, 0)),
        out_shape=jax.ShapeDtypeStruct((cap, D_MODEL), BF16),
        scratch_shapes=[pltpu.VMEM((2, rows * ROW_CHUNKS, LANES), F32), pltpu.SemaphoreType.DMA((2,))],
        compiler_params=pltpu.CompilerParams(dimension_semantics=("arbitrary",), disable_bounds_checks=True,
                                             vmem_limit_bytes=48 * 1024 * 1024),
    )(idx, idx, x_tiles)


def _expert_body(be_ref, nv_ref, x_ref, wg_ref, wu_ref, wd_ref, o_ref, acc_ref):
    i = pl.program_id(0)
    j = pl.program_id(1)
    half = x_ref.shape[0] // 2

    @pl.when(j == 0)
    def _():
        acc_ref[...] = jnp.zeros_like(acc_ref)

    def swiglu_rows(r0):
        xb = x_ref[r0:r0 + half, :]
        hidden = (_silu(_dot(xb, wg_ref[0])) * _dot(xb, wu_ref[0])).astype(BF16)
        acc_ref[r0:r0 + half, :] += _dot(hidden, wd_ref[0])

    @pl.when(nv_ref[i] > 0)
    def _():
        swiglu_rows(0)

    @pl.when(nv_ref[i] > half)
    def _():
        swiglu_rows(half)

    @pl.when(j == pl.num_programs(1) - 1)
    def _():
        _store_row_tiles(o_ref, acc_ref[...])


def _expert_ffn(x_slots, block_expert, block_valid, wg, wu, wd, tm):
    cap = x_slots.shape[0]
    dff = wg.shape[2]
    tf = 512

    def ff_tile(i, j, nv):
        return jnp.where(nv[i] > 0, j, 0)

    grid_spec = pltpu.PrefetchScalarGridSpec(
        num_scalar_prefetch=2,
        grid=(cap // tm, dff // tf),
        in_specs=[pl.BlockSpec((tm, D_MODEL), lambda i, j, be, nv: (i, 0)),
                  pl.BlockSpec((1, D_MODEL, tf), lambda i, j, be, nv: (be[i], 0, ff_tile(i, j, nv))),
                  pl.BlockSpec((1, D_MODEL, tf), lambda i, j, be, nv: (be[i], 0, ff_tile(i, j, nv))),
                  pl.BlockSpec((1, tf, D_MODEL), lambda i, j, be, nv: (be[i], ff_tile(i, j, nv), 0))],
        out_specs=pl.BlockSpec((tm * ROW_CHUNKS, LANES), lambda i, j, be, nv: (i, 0)),
        scratch_shapes=[pltpu.VMEM((tm, D_MODEL), F32)],
    )
    return pl.pallas_call(
        _expert_body,
        grid_spec=grid_spec,
        out_shape=jax.ShapeDtypeStruct((cap * ROW_CHUNKS, LANES), F32),
        compiler_params=_cparams(("parallel", "arbitrary"), 56),
    )(block_expert, block_valid, x_slots, wg, wu, wd)


def _combine_body(idx_cur_ref, idx_next_ref, x_ref, gt_ref, g_ref, beta_ref, y_ref, o_ref, buf, sems):
    rows = x_ref.shape[0]
    slot = _pipelined_row_gather(idx_cur_ref, idx_next_ref, TOP_K * rows, y_ref, buf, sems)
    gt = gt_ref[...]
    f = (gt[:, 0:1] * _load_row_tiles(buf.at[slot], 0, rows)
         + gt[:, 1:2] * _load_row_tiles(buf.at[slot], rows, rows))
    o_ref[...] = _layer_norm(ALPHA * x_ref[...] + f, g_ref[...], beta_ref[...])


def _combine_ln(x, y_tiles, pos, gates, g, beta):
    m = x.shape[0]
    rows = min(COMBINE_ROWS, m)
    nblk = m // rows
    n = TOP_K * rows
    idx = pos.reshape(nblk, rows, TOP_K).transpose(0, 2, 1).reshape(nblk, 1, n)
    const = lambda i: (0, 0)
    return pl.pallas_call(
        _combine_body,
        grid=(nblk,),
        in_specs=_idx_specs(n, nblk) + [
            pl.BlockSpec((rows, D_MODEL), lambda i: (i, 0)),
            pl.BlockSpec((rows, LANES), lambda i: (i, 0)),
            pl.BlockSpec((1, D_MODEL), const), pl.BlockSpec((1, D_MODEL), const),
            pl.BlockSpec(memory_space=pl.ANY)],
        out_specs=pl.BlockSpec((rows, D_MODEL), lambda i: (i, 0)),
        out_shape=jax.ShapeDtypeStruct((m, D_MODEL), F32),
        scratch_shapes=[pltpu.VMEM((2, n * ROW_CHUNKS, LANES), F32), pltpu.SemaphoreType.DMA((2,))],
        compiler_params=pltpu.CompilerParams(dimension_semantics=("arbitrary",), disable_bounds_checks=True,
                                             vmem_limit_bytes=48 * 1024 * 1024),
    )(idx, idx, x, gates, g.reshape(1, -1), beta.reshape(1, -1), y_tiles)


def _moe_ln(x, x_tiles, w_router, wg, wu, wd, g, beta, tm):
    m = x.shape[0]
    n_asg = m * TOP_K
    idx, gates = _router(x, w_router)
    e_flat = idx[:, :TOP_K].reshape(-1)
    onehot = (jnp.arange(N_EXPERTS, dtype=jnp.int32)[:, None] == e_flat[None, :]).astype(jnp.int32)
    csum = jnp.cumsum(onehot, axis=1)
    counts = csum[:, -1]
    rank = jnp.sum(csum * onehot, axis=0) - 1
    padded = (counts + tm - 1) // tm * tm
    pend = jnp.cumsum(padded)
    pstart = pend - padded
    dest = (jnp.sum(pstart[:, None] * onehot, axis=0) + rank).astype(jnp.int32)
    cap = n_asg + N_EXPERTS * tm
    tok = jnp.arange(n_asg, dtype=jnp.int32) // TOP_K
    slot_tok = jnp.zeros((cap,), jnp.int32).at[dest].set(tok, unique_indices=True)
    nblk = cap // tm
    block_start = jnp.arange(nblk, dtype=pend.dtype) * tm
    block_expert = jnp.minimum(jnp.searchsorted(pend, block_start, side='right'), N_EXPERTS - 1).astype(jnp.int32)
    block_valid = jnp.clip((pstart + counts)[block_expert] - block_start, 0, tm).astype(jnp.int32)

    x_slots = _gather_x(x_tiles, slot_tok)
    y_tiles = _expert_ffn(x_slots, block_expert, block_valid, wg, wu, wd, tm)
    return _combine_ln(x, y_tiles, dest.reshape(m, TOP_K), gates, g, beta)


def _mixer_ln(x2, bsz, s, p, emit_row_tiles):
    proj = _matmul(x2, p['w_in'], min(1024, x2.shape[0]), PROJ_TILE)
    proj3 = proj.reshape(bsz, s, PROJ_COLS)
    cos_slab, sin_slab, cos4, sin4 = _rope_tables(s)
    out_a = _mla(proj3, cos_slab, sin_slab, p['mla_q_norm'], p['w_q'], p['w_qr'], p['mla_kv_norm'], p['w_kv'],
                 p['mla_out_norm'])
    out_b = _retention(proj3, cos4, sin4)
    xbc_act = _dwconv(proj3, COL_XBC, SSD_CONV_CH, p['ssd_conv_w'], p['ssd_conv_b'], act=True)
    out_c = _ssd(proj3, xbc_act, p['ssd_dt_bias'], p['ssd_a_log'], p['ssd_d'], p['ssd_norm'])
    out_d = _hyena(proj3, p['hy_conv_w'], p['hy_conv_b'], p['hy_w1'], p['hy_b1'], p['hy_w2'], p['hy_b2'], p['hy_w3'],
                   p['hy_freq'], p['hy_bias'])
    m = bsz * s
    gw = GROUP_WIDTH
    return _out_proj_ln(out_a.reshape(m, gw), out_b.reshape(m, gw), out_c.reshape(m, gw), out_d.reshape(m, gw),
                        x2, p['w_out'], p['hy_out_norm'], p['ln1_g'], p['ln1_b'], emit_row_tiles)


_MIXER_KEYS = ('mla_q_norm', 'mla_kv_norm', 'mla_out_norm', 'ssd_conv_w', 'ssd_conv_b', 'ssd_dt_bias', 'ssd_a_log',
               'ssd_d', 'ssd_norm', 'hy_conv_w', 'hy_conv_b', 'hy_w1', 'hy_b1', 'hy_w2', 'hy_b2', 'hy_w3', 'hy_freq',
               'hy_bias', 'hy_out_norm', 'ln1_g', 'ln1_b')

MOE_TM = 1024


def kernel(x, w_in, mla_q_norm, mla_w_uq, mla_kv_norm, mla_w_ukv, mla_out_norm, ssd_conv_w, ssd_conv_b, ssd_dt_bias, ssd_a_log, ssd_d, ssd_norm, hy_conv_w, hy_conv_b, hy_w1, hy_b1, hy_w2, hy_b2, hy_w3, hy_freq, hy_bias, hy_out_norm, w_out, ln1_g, ln1_b, ln2_g, ln2_b, ffn_w_gate, ffn_w_up, ffn_w_down, moe_router, moe_w_gate, moe_w_up, moe_w_down):
    args = dict(locals())
    bsz, s, d = x.shape
    x2 = x.reshape(bsz * s, d)
    for layer in range(DEPTH):
        p = {k: args[k][layer] for k in _MIXER_KEYS}
        p['w_in'] = _prep_w_in(w_in[layer])
        p['w_q'], p['w_qr'] = _prep_w_uq(mla_w_uq[layer])
        p['w_kv'] = _prep_w_ukv(mla_w_ukv[layer])
        p['w_out'] = w_out[layer].astype(BF16)
        j = layer // 2
        if layer % 2 == 0:
            x2, = _mixer_ln(x2, bsz, s, p, False)
            x2 = _ffn_ln(x2, ffn_w_gate[j].astype(BF16), ffn_w_up[j].astype(BF16), ffn_w_down[j].astype(BF16),
                         ln2_g[layer], ln2_b[layer])
        else:
            x2, x_tiles = _mixer_ln(x2, bsz, s, p, True)
            x2 = _moe_ln(x2, x_tiles, moe_router[j], moe_w_gate[j].astype(BF16), moe_w_up[j].astype(BF16),
                         moe_w_down[j].astype(BF16), ln2_g[layer], ln2_b[layer], MOE_TM)
    return x2.reshape(bsz, s, d)
```

```python
import functools
import math

import numpy as np
import jax
import jax.numpy as jnp
from jax import lax
from jax.experimental import pallas as pl
from jax.experimental.pallas import tpu as pltpu

F32 = jnp.float32
BF16 = jnp.bfloat16
HIGHEST = lax.Precision.HIGHEST

D_MODEL = 2048
DEPTH = 2
GROUP_WIDTH = 512
MLA_HEADS = 4
MLA_NOPE = 128
MLA_ROPE = 64
MLA_V = 128
MLA_Q_LORA = 384
MLA_KV_LORA = 256
RET_HEADS = 4
RET_DV = 128
RET_DK = 64
RET_DECAY_EXP_FWD = 5.0
RET_DECAY_EXP_BWD = 5.5
SSD_HEADDIM = 64
SSD_HEADS = 8
SSD_GROUPS = 2
SSD_STATE = 128
SSD_CONV = 5
SSD_CONV_CH = 1024
HY_ORDER = 2
HY_WIDTH = 512
HY_SHORT = 3
HY_EMB = 33
HY_FILTER_HIDDEN = 64
HY_MIN_DECAY = math.log(1e-2) / 1.5
HY_MAX_DECAY = math.log(1e-2) / 0.3
N_EXPERTS = 8
TOP_K = 2
ROPE_BASE = 10000.0
ALPHA = (2 * DEPTH) ** 0.25

V7X_VMEM_BYTES = 64 * 1024 * 1024
LANES = 128
SUBLANES = 8

PROJ_COLS = 6400
COL_HY = 0
COL_QC = 1536
COL_RQ = 2048
COL_KVPE = 4096
COL_Z = 4608
COL_XBC = 5120
COL_DT = 6144
PROJ_TILE = 1280

RET_CHUNK = 256
SSD_CHUNK = 128
HY_ROW_TILE = 512
CONV_ROWS = 1024
RET_OUT_CHUNKS = 4
SSD_OUT_CHUNKS = 8
ROW_CHUNKS = D_MODEL // LANES
GATHER_ROWS = 512
COMBINE_ROWS = 256
DMA_LOOP_UNROLL = 8


def _cparams(semantics, vmem_mb):
    assert vmem_mb * 1024 * 1024 < V7X_VMEM_BYTES
    return pltpu.CompilerParams(dimension_semantics=semantics, vmem_limit_bytes=vmem_mb * 1024 * 1024)


def _sigmoid(x):
    return 1.0 / (1.0 + jnp.exp(-x))


def _silu(x):
    return x * _sigmoid(x)


def _rms(x, w, eps=1e-6):
    return x * lax.rsqrt(jnp.mean(x * x, axis=-1, keepdims=True) + eps) * w


def _layer_norm(y, g, b, eps=1e-5):
    mu = jnp.mean(y, axis=-1, keepdims=True)
    d = y - mu
    var = jnp.mean(d * d, axis=-1, keepdims=True)
    return d * lax.rsqrt(var + eps) * g + b


def _dot(a, b):
    return jnp.dot(a, b, preferred_element_type=F32)


def _dot_nt(a, b):
    return lax.dot_general(a, b, (((1,), (1,)), ((), ())), preferred_element_type=F32)


def _dot_tn(a, b):
    return lax.dot_general(a, b, (((0,), (0,)), ((), ())), preferred_element_type=F32)


def _dot_f32(a, b):
    return jnp.dot(a, b, preferred_element_type=F32, precision=HIGHEST)


def _mm_body(x_ref, w_ref, o_ref, xb_ref):
    @pl.when(pl.program_id(1) == 0)
    def _():
        xb_ref[...] = x_ref[...].astype(BF16)

    o_ref[...] = _dot(xb_ref[...], w_ref[...])


def _matmul(x, w, tm, tn, vmem_mb=48):
    m, k = x.shape
    n = w.shape[1]
    assert m % tm == 0 and n % tn == 0
    return pl.pallas_call(
        _mm_body,
        grid=(m // tm, n // tn),
        in_specs=[pl.BlockSpec((tm, k), lambda i, j: (i, 0)),
                  pl.BlockSpec((k, tn), lambda i, j: (0, j))],
        out_specs=pl.BlockSpec((tm, tn), lambda i, j: (i, j)),
        out_shape=jax.ShapeDtypeStruct((m, n), F32),
        scratch_shapes=[pltpu.VMEM((tm, k), BF16)],
        compiler_params=_cparams(("parallel", "arbitrary"), vmem_mb),
    )(x, w)


def _rot_half_cols(w, heads):
    k = w.shape[0]
    w = w.reshape(k, heads, 2, 32)
    return jnp.stack([-w[:, :, 1], w[:, :, 0]], axis=2).reshape(k, heads * 64)


def _prep_w_in(w):
    k = w.shape[0]

    def sl(a, b):
        return w[:, a:b]

    def zc(n):
        return jnp.zeros((k, n), w.dtype)

    q_c, kv_c, k_pe = sl(0, 384), sl(384, 640), sl(640, 704)
    r_q, r_k, r_v, r_g = sl(704, 960), sl(960, 1216), sl(1216, 1728), sl(1728, 2240)
    m_z, m_xbc, m_dt, h_u = sl(2240, 2752), sl(2752, 3776), sl(3776, 3792), sl(3792, 5328)
    cols = [h_u, q_c, zc(128),
            r_q, r_k, _rot_half_cols(r_q, RET_HEADS), _rot_half_cols(r_k, RET_HEADS), r_v, r_g,
            kv_c, k_pe, zc(64), _rot_half_cols(k_pe, 1), zc(64),
            m_z, m_xbc, m_dt, zc(112), zc(PROJ_COLS - COL_DT - LANES)]
    out = jnp.concatenate(cols, axis=1).astype(BF16)
    assert out.shape[1] == PROJ_COLS
    return out


def _prep_w_uq(w):
    k = w.shape[0]
    w = w.reshape(k, MLA_HEADS, MLA_NOPE + MLA_ROPE)
    nope, rope = w[:, :, :MLA_NOPE], w[:, :, MLA_NOPE:]
    z = jnp.zeros((k, MLA_HEADS, 64), w.dtype)
    main = jnp.concatenate([nope, rope, z], axis=2).reshape(k, MLA_HEADS * 256)
    rr = _rot_half_cols(rope.reshape(k, MLA_HEADS * 64), MLA_HEADS).reshape(k, MLA_HEADS, 64)
    rot = jnp.concatenate([rr, z], axis=2).reshape(k, MLA_HEADS * 128)
    return main.astype(BF16), rot.astype(BF16)


def _prep_w_ukv(w):
    k = w.shape[0]
    w = w.reshape(k, MLA_HEADS, MLA_NOPE + MLA_V)
    return jnp.concatenate([w[:, :, :MLA_NOPE].reshape(k, -1), w[:, :, MLA_NOPE:].reshape(k, -1)], axis=1).astype(BF16)


def _rope_tables(s):
    half = 32
    inv_freq = ROPE_BASE ** (-jnp.arange(half, dtype=F32) * 2.0 / 64)
    ang = jnp.arange(s, dtype=F32)[:, None] * inv_freq[None, :]
    cos, sin = jnp.cos(ang), jnp.sin(ang)
    cos64 = jnp.concatenate([cos, cos], axis=1)
    sin64 = jnp.concatenate([sin, sin], axis=1)
    z = jnp.zeros((s, 64), F32)
    return (jnp.concatenate([cos64, z], axis=1), jnp.concatenate([sin64, z], axis=1),
            jnp.tile(cos64, (1, 4)), jnp.tile(sin64, (1, 4)))


def _ret_tables(n):
    heads = jnp.arange(RET_HEADS, dtype=F32)
    lgf = jnp.log1p(-jnp.exp2(-RET_DECAY_EXP_FWD - heads))
    lgb = jnp.log1p(-jnp.exp2(-RET_DECAY_EXP_BWD - heads))
    idx = jnp.arange(n, dtype=F32)

    def lanes(tab, width):
        return jnp.repeat(tab, width, axis=1)

    kdf = lanes(jnp.exp((n - 1.0 - idx)[:, None] * lgf), RET_DK)
    kdb = lanes(jnp.exp(idx[:, None] * lgb), RET_DK)
    qdf = lanes(jnp.exp((idx + 1.0)[:, None] * lgf), RET_DK)
    qdb = lanes(jnp.exp((n - idx)[:, None] * lgb), RET_DK)
    cdf = lanes(jnp.exp(n * lgf)[None, :], RET_DV)
    cdb = lanes(jnp.exp(n * lgb)[None, :], RET_DV)
    diff = idx[:, None] - idx[None, :]
    dec = jnp.where(diff[None] >= 0,
                    jnp.exp(jnp.maximum(diff, 0.0)[None] * lgf[:, None, None]),
                    jnp.exp(jnp.maximum(-diff, 0.0)[None] * lgb[:, None, None]))
    bd = (jnp.arange(RET_HEADS * RET_DK)[:, None] // RET_DK == jnp.arange(RET_HEADS * RET_DV)[None, :] // RET_DV)
    return kdf, kdb, qdf, qdb, cdf, cdb, dec.astype(F32), bd.astype(F32)


def _ssd_tables(n):
    tri = (jnp.arange(n)[:, None] >= jnp.arange(n)[None, :]).astype(F32)
    lane_head = jnp.arange(GROUP_WIDTH) // SSD_HEADDIM
    r = jnp.arange(LANES)
    ef = (r[:, None] == lane_head[None, :]).astype(BF16)
    eb = (r[:, None] == lane_head[None, :] + SSD_HEADS).astype(BF16)
    e = jnp.concatenate([ef, eb], axis=1)
    return tri, jnp.concatenate([e, e], axis=0)


def _dft_tables(l):
    f = jnp.arange(l, dtype=jnp.int32)
    k = (f[:, None] * f[None, :]) % (2 * l)
    ang = k.astype(F32) * (math.pi / l)
    c, s = jnp.cos(ang), jnp.sin(ang)
    sc = jnp.where(f == 0, 1.0, 2.0).astype(F32)[:, None] / (2.0 * l)
    nyq = jnp.where(f % 2 == 0, 1.0, -1.0).astype(F32)[None, :] / (2.0 * l)
    c_sc = jnp.concatenate([c * sc, nyq, jnp.zeros((7, l), F32)], axis=0)
    return c.astype(BF16), s.astype(BF16), c_sc.astype(BF16), (-s * sc).astype(BF16)


def _hyena_positions(l):
    t = jnp.linspace(0.0, 1.0, l, dtype=F32)[:, None]
    bands = (HY_EMB - 1) // 2
    ang = 2.0 * math.pi * jnp.arange(l, dtype=F32)[:, None] / l
    f = jnp.linspace(1e-4, bands - 1, bands, dtype=F32)[None, :]
    z = jnp.concatenate([t, jnp.cos(f * ang), -jnp.sin(f * ang), jnp.zeros((l, LANES - HY_EMB), F32)], axis=-1)
    deltas = jnp.abs(jnp.linspace(HY_MIN_DECAY, HY_MAX_DECAY, HY_WIDTH, dtype=F32))[None, :]
    return z, deltas


def _conv_body(taps, act, x_ref, p_ref, n_ref, w_ref, b_ref, o_ref):
    r = pl.program_id(1)
    tr = x_ref.shape[1]
    half = taps // 2
    prev = jnp.where(r > 0, p_ref[0], 0.0)
    nxt = jnp.where(r < pl.num_programs(1) - 1, n_ref[0], 0.0)
    win = jnp.concatenate([prev, x_ref[0], nxt], axis=0)
    rows = tr + 2 * SUBLANES
    acc = jnp.broadcast_to(b_ref[...], (tr, x_ref.shape[2]))
    for t in range(taps):
        sh = (half - t) % rows
        shifted = win if sh == 0 else pltpu.roll(win, sh, axis=0)
        acc = acc + shifted[SUBLANES:SUBLANES + tr] * w_ref[t:t + 1, :]
    if act:
        acc = _silu(acc)
    o_ref[0] = acc


def _dwconv(proj3, col0, width, w, b, act):
    bsz, s, _ = proj3.shape
    taps = w.shape[0]
    cb = 512
    tr = min(CONV_ROWS, s)
    off = col0 // cb
    nb8 = s // SUBLANES
    t8 = tr // SUBLANES
    return pl.pallas_call(
        functools.partial(_conv_body, taps, act),
        grid=(bsz, s // tr, width // cb),
        in_specs=[pl.BlockSpec((1, tr, cb), lambda i, r, c: (i, r, c + off)),
                  pl.BlockSpec((1, SUBLANES, cb), lambda i, r, c: (i, jnp.maximum(r * t8 - 1, 0), c + off)),
                  pl.BlockSpec((1, SUBLANES, cb), lambda i, r, c: (i, jnp.minimum((r + 1) * t8, nb8 - 1), c + off)),
                  pl.BlockSpec((taps, cb), lambda i, r, c: (0, c)),
                  pl.BlockSpec((1, cb), lambda i, r, c: (0, c))],
        out_specs=pl.BlockSpec((1, tr, cb), lambda i, r, c: (i, r, c)),
        out_shape=jax.ShapeDtypeStruct((bsz, s, width), F32),
        compiler_params=_cparams(("parallel", "parallel", "parallel"), 40),
    )(proj3, proj3, proj3, w, b.reshape(1, width))


def _mla_body(qc_ref, kvpe_ref, cq_ref, sq_ref, ck_ref, sk_ref, qn_ref, wq_ref, wqr_ref, kvn_ref, wkv_ref, on_ref,
              o_ref, k_scr, v_scr):
    @pl.when(pl.program_id(1) == 0)
    def _():
        kvpe = kvpe_ref[0]
        kvn = _rms(kvpe[:, :MLA_KV_LORA], kvn_ref[...]).astype(BF16)
        kpe = (kvpe[:, 256:384] * ck_ref[...] + kvpe[:, 384:512] * sk_ref[...]).astype(BF16)
        for h in range(MLA_HEADS):
            kn = _dot(kvn, wkv_ref[:, h * 128:(h + 1) * 128]).astype(BF16)
            k_scr[h] = jnp.concatenate([kn, kpe], axis=1)
            v_scr[h] = _dot(kvn, wkv_ref[:, 512 + h * 128:512 + (h + 1) * 128]).astype(BF16)

    scale = (MLA_NOPE + MLA_ROPE) ** -0.5
    qn = _rms(qc_ref[0][:, :MLA_Q_LORA], qn_ref[...]).astype(BF16)
    outs = []
    for h in range(MLA_HEADS):
        qm = _dot(qn, wq_ref[:, h * 256:(h + 1) * 256])
        qr = _dot(qn, wqr_ref[:, h * 128:(h + 1) * 128])
        qpe = qm[:, 128:] * cq_ref[...] + qr * sq_ref[...]
        qh = (jnp.concatenate([qm[:, :128], qpe], axis=1) * scale).astype(BF16)
        sc = _dot_nt(qh, k_scr[h])
        p = jnp.exp(sc - jnp.max(sc, axis=-1, keepdims=True))
        den = jnp.sum(p, axis=-1, keepdims=True)
        outs.append(_dot(p.astype(BF16), v_scr[h]) / den)
    o_ref[0] = _rms(jnp.concatenate(outs, axis=1), on_ref[...])


def _mla(proj3, cos_slab, sin_slab, q_norm, w_q, w_qr, kv_norm, w_kv, out_norm):
    bsz, s, _ = proj3.shape
    tq = min(512, s)
    const = lambda i, j: (0, 0)
    return pl.pallas_call(
        _mla_body,
        grid=(bsz, s // tq),
        in_specs=[pl.BlockSpec((1, tq, 512), lambda i, j: (i, j, COL_QC // 512)),
                  pl.BlockSpec((1, s, 512), lambda i, j: (i, 0, COL_KVPE // 512)),
                  pl.BlockSpec((tq, 128), lambda i, j: (j, 0)),
                  pl.BlockSpec((tq, 128), lambda i, j: (j, 0)),
                  pl.BlockSpec((s, 128), const),
                  pl.BlockSpec((s, 128), const),
                  pl.BlockSpec((1, MLA_Q_LORA), const),
                  pl.BlockSpec(w_q.shape, const),
                  pl.BlockSpec(w_qr.shape, const),
                  pl.BlockSpec((1, MLA_KV_LORA), const),
                  pl.BlockSpec(w_kv.shape, const),
                  pl.BlockSpec((1, GROUP_WIDTH), const)],
        out_specs=pl.BlockSpec((1, tq, GROUP_WIDTH), lambda i, j: (i, j, 0)),
        out_shape=jax.ShapeDtypeStruct((bsz, s, GROUP_WIDTH), F32),
        scratch_shapes=[pltpu.VMEM((MLA_HEADS, s, 256), BF16), pltpu.VMEM((MLA_HEADS, s, MLA_V), BF16)],
        compiler_params=_cparams(("parallel", "arbitrary"), 48),
    )(proj3, proj3, cos_slab, sin_slab, cos_slab, sin_slab, q_norm.reshape(1, -1), w_q, w_qr,
      kv_norm.reshape(1, -1), w_kv, out_norm.reshape(1, -1))


def _ret_state_body(nchunks, rk_ref, rkr_ref, rv_ref, cos_ref, sin_ref, kdf_ref, kdb_ref, cdf_ref, cdb_ref, bd_ref,
                    sf_ref, sb_ref, st_scr):
    n = kdf_ref.shape[0]

    def sweep(out_ref, kd_ref, cd_ref, chunk_of):
        st_scr[...] = jnp.zeros_like(st_scr)

        def body(t, carry):
            c = chunk_of(t)
            rows = pl.ds(pl.multiple_of(c * n, n), n)
            k = (rk_ref[0, rows, :] * cos_ref[rows, :] + rkr_ref[0, rows, :] * sin_ref[rows, :]) * (RET_DK ** -0.5)
            out_ref[0, c] = st_scr[...]
            new = _dot_tn((k * kd_ref[...]).astype(BF16), rv_ref[0, rows, :].astype(BF16))
            st_scr[...] = st_scr[...] * cd_ref[...] + new * bd_ref[...]
            return carry

        lax.fori_loop(0, nchunks, body, 0, unroll=2)

    sweep(sf_ref, kdf_ref, cdf_ref, lambda t: t)
    sweep(sb_ref, kdb_ref, cdb_ref, lambda t: nchunks - 1 - t)


def _ret_out_chunk(rq, rk, rqr, rkr, rv, rg, cos, sin, sf, sb, qdf_ref, qdb_ref, dec_ref):
    q = rq * cos + rqr * sin
    kb = ((rk * cos + rkr * sin) * (RET_DK ** -0.5)).astype(BF16)
    vb = rv.astype(BF16)
    cross = (_dot((q * qdf_ref[...]).astype(BF16), sf.astype(BF16))
             + _dot((q * qdb_ref[...]).astype(BF16), sb.astype(BF16)))
    lane_head = lax.broadcasted_iota(jnp.int32, (1, RET_HEADS * RET_DK), 1) // RET_DK
    outs = []
    for h in range(RET_HEADS):
        qh = jnp.where(lane_head == h, q, 0.0).astype(BF16)
        sc = _dot_nt(qh, kb) * dec_ref[h]
        y = _dot(sc.astype(BF16), vb[:, h * RET_DV:(h + 1) * RET_DV]) + cross[:, h * RET_DV:(h + 1) * RET_DV]
        mu = jnp.mean(y, axis=-1, keepdims=True)
        d = y - mu
        var = jnp.mean(d * d, axis=-1, keepdims=True)
        outs.append(d * lax.rsqrt(var + 1e-6))
    return jnp.concatenate(outs, axis=1) * _silu(rg)


def _ret_out_body(rq_ref, rk_ref, rqr_ref, rkr_ref, rv_ref, rg_ref, cos_ref, sin_ref, qdf_ref, qdb_ref, dec_ref,
                  sf_ref, sb_ref, o_ref):
    n = qdf_ref.shape[0]

    def body(k, carry):
        rows = pl.ds(pl.multiple_of(k * n, n), n)
        o_ref[0, rows, :] = _ret_out_chunk(rq_ref[0, rows, :], rk_ref[0, rows, :], rqr_ref[0, rows, :],
                                           rkr_ref[0, rows, :], rv_ref[0, rows, :], rg_ref[0, rows, :],
                                           cos_ref[rows, :], sin_ref[rows, :], sf_ref[0, k], sb_ref[0, k],
                                           qdf_ref, qdb_ref, dec_ref)
        return carry

    lax.fori_loop(0, o_ref.shape[1] // n, body, 0, unroll=2)


def _retention(proj3, cos4, sin4):
    bsz, s, _ = proj3.shape
    n = min(RET_CHUNK, s)
    c = s // n
    kdf, kdb, qdf, qdb, cdf, cdb, dec, bd = _ret_tables(n)
    c256 = COL_RQ // 256
    c512 = COL_RQ // 512
    const1 = lambda i: (0, 0)

    state_shape = jax.ShapeDtypeStruct((bsz, c, RET_HEADS * RET_DK, GROUP_WIDTH), F32)
    state_spec = pl.BlockSpec((1, c, RET_HEADS * RET_DK, GROUP_WIDTH), lambda i: (i, 0, 0, 0))
    sf, sb = pl.pallas_call(
        functools.partial(_ret_state_body, c),
        grid=(bsz,),
        in_specs=[pl.BlockSpec((1, s, 256), lambda i: (i, 0, c256 + 1)),
                  pl.BlockSpec((1, s, 256), lambda i: (i, 0, c256 + 3)),
                  pl.BlockSpec((1, s, 512), lambda i: (i, 0, c512 + 2)),
                  pl.BlockSpec((s, 256), const1), pl.BlockSpec((s, 256), const1),
                  pl.BlockSpec((n, 256), const1), pl.BlockSpec((n, 256), const1),
                  pl.BlockSpec((1, 512), const1), pl.BlockSpec((1, 512), const1),
                  pl.BlockSpec((256, 512), const1)],
        out_specs=[state_spec, state_spec],
        out_shape=[state_shape, state_shape],
        scratch_shapes=[pltpu.VMEM((RET_HEADS * RET_DK, GROUP_WIDTH), F32)],
        compiler_params=_cparams(("parallel",), 48),
    )(proj3, proj3, proj3, cos4, sin4, kdf, kdb, cdf, cdb, bd)

    k = min(RET_OUT_CHUNKS, c)
    const3 = lambda i, j: (0, 0, 0)
    const2 = lambda i, j: (0, 0)
    return pl.pallas_call(
        _ret_out_body,
        grid=(bsz, c // k),
        in_specs=[pl.BlockSpec((1, k * n, 256), lambda i, j: (i, j, c256)),
                  pl.BlockSpec((1, k * n, 256), lambda i, j: (i, j, c256 + 1)),
                  pl.BlockSpec((1, k * n, 256), lambda i, j: (i, j, c256 + 2)),
                  pl.BlockSpec((1, k * n, 256), lambda i, j: (i, j, c256 + 3)),
                  pl.BlockSpec((1, k * n, 512), lambda i, j: (i, j, c512 + 2)),
                  pl.BlockSpec((1, k * n, 512), lambda i, j: (i, j, c512 + 3)),
                  pl.BlockSpec((k * n, 256), lambda i, j: (j, 0)),
                  pl.BlockSpec((k * n, 256), lambda i, j: (j, 0)),
                  pl.BlockSpec((n, 256), const2), pl.BlockSpec((n, 256), const2),
                  pl.BlockSpec((RET_HEADS, n, n), const3),
                  pl.BlockSpec((1, k, 256, 512), lambda i, j: (i, j, 0, 0)),
                  pl.BlockSpec((1, k, 256, 512), lambda i, j: (i, j, 0, 0))],
        out_specs=pl.BlockSpec((1, k * n, GROUP_WIDTH), lambda i, j: (i, j, 0)),
        out_shape=jax.ShapeDtypeStruct((bsz, s, GROUP_WIDTH), F32),
        compiler_params=_cparams(("parallel", "parallel"), 32),
    )(proj3, proj3, proj3, proj3, proj3, proj3, cos4, sin4, qdf, qdb, dec, sf, sb)


def _ssd_decays(dt_rows, dtb_ref, a_ref, tri_ref):
    dt_raw = dt_rows + dtb_ref[...]
    dt = jnp.maximum(dt_raw, 0.0) + jnp.log1p(jnp.exp(-jnp.abs(dt_raw)))
    la = dt * a_ref[...]
    return dt, la, _dot_f32(tri_ref[...], la)


def _expand_heads(v, e):
    hi = v.astype(BF16)
    lo = (v - hi.astype(F32)).astype(BF16)
    return _dot(jnp.concatenate([hi, lo], axis=1), e)


def _ssd_state_body(nchunks, xbc_ref, dt_ref, dtb_ref, a_ref, tri_ref, e_ref, sf_ref, sb_ref, st_scr):
    n = SSD_CHUNK
    w = GROUP_WIDTH
    fwd_lane = lax.broadcasted_iota(jnp.int32, (1, LANES), 1) < SSD_HEADS

    def sweep(out_ref, lo, chunk_of):
        st_scr[...] = jnp.zeros_like(st_scr)

        def body(t, carry):
            c = chunk_of(t)
            rows = pl.ds(pl.multiple_of(c * n, n), n)
            dt, la, cs = _ssd_decays(dt_ref[0, rows, :], dtb_ref, a_ref, tri_ref)
            tot = cs[n - 1:n, :]
            wgt = dt * jnp.exp(jnp.where(fwd_lane, tot - cs, cs - la))
            both = _expand_heads(jnp.concatenate([wgt, jnp.broadcast_to(jnp.exp(tot), (SUBLANES, LANES))], axis=0),
                                 e_ref[:, lo:lo + w])
            out_ref[0, c] = st_scr[...]
            xw = (xbc_ref[0, rows, 0:w] * both[:n]).astype(BF16)
            new = []
            for g in range(SSD_GROUPS):
                bg = xbc_ref[0, rows, w + g * SSD_STATE:w + (g + 1) * SSD_STATE].astype(BF16)
                new.append(_dot_tn(bg, xw[:, g * 256:(g + 1) * 256]))
            st_scr[...] = st_scr[...] * both[n:n + 1] + jnp.concatenate(new, axis=1)
            return carry

        lax.fori_loop(0, nchunks, body, 0, unroll=4)

    sweep(sf_ref, 0, lambda t: t)
    sweep(sb_ref, w, lambda t: nchunks - 1 - t)


def _ssd_out_chunk(xbc, dt_rows, sf, sb, dtb_ref, a_ref, tri_ref, e_ref, dskip_ref):
    n = xbc.shape[0]
    w = GROUP_WIDTH
    dt, la, cs = _ssd_decays(dt_rows, dtb_ref, a_ref, tri_ref)
    ecs = cs - la
    fwd_lane = lax.broadcasted_iota(jnp.int32, (1, LANES), 1) < SSD_HEADS
    carry = jnp.exp(jnp.where(fwd_lane, cs, cs[n - 1:n, :] - ecs))
    both = _expand_heads(jnp.concatenate([dt, carry], axis=0), e_ref[...])
    dt_e = both[:n]
    carry_e = both[n:]
    xs = xbc[:, :w]
    xdt_f = xs * dt_e[:, :w]
    xdt_b = xs * dt_e[:, w:]
    cs_t = cs.T
    ecs_t = ecs.T
    ii = lax.broadcasted_iota(jnp.int32, (n, n), 0)
    jj = lax.broadcasted_iota(jnp.int32, (n, n), 1)
    low = lax.broadcasted_iota(jnp.int32, (1, LANES), 1) < SSD_HEADDIM
    neg = -1e30
    ydiag = []
    yoff = []
    for g in range(SSD_GROUPS):
        bg = xbc[:, w + g * SSD_STATE:w + (g + 1) * SSD_STATE].astype(BF16)
        cg = xbc[:, w + 256 + g * SSD_STATE:w + 256 + (g + 1) * SSD_STATE].astype(BF16)
        cb = _dot_nt(cg, bg)
        for pair in range(2):
            p = 2 * g + pair
            lhs = []
            for h in (2 * p, 2 * p + 1):
                lf = jnp.exp(jnp.where(ii >= jj, cs[:, h:h + 1] - cs_t[h:h + 1, :], neg))
                lhs.append((cb * lf).astype(BF16))
            for h in (2 * p, 2 * p + 1):
                hb = SSD_HEADS + h
                lb = jnp.exp(jnp.where(jj > ii, ecs_t[hb:hb + 1, :] - ecs[:, hb:hb + 1], neg))
                lhs.append((cb * lb).astype(BF16))
            xf = xdt_f[:, p * LANES:(p + 1) * LANES]
            xb = xdt_b[:, p * LANES:(p + 1) * LANES]
            rhs = jnp.concatenate([jnp.where(low, xf, 0.0), jnp.where(low, 0.0, xf),
                                   jnp.where(low, xb, 0.0), jnp.where(low, 0.0, xb)], axis=0).astype(BF16)
            ydiag.append(_dot(jnp.concatenate(lhs, axis=1), rhs))
        yoff.append(_dot(cg, sf[:, g * 256:(g + 1) * 256].astype(BF16)) * carry_e[:, g * 256:(g + 1) * 256]
                    + _dot(cg, sb[:, g * 256:(g + 1) * 256].astype(BF16)) * carry_e[:, w + g * 256:w + (g + 1) * 256])
    return jnp.concatenate(ydiag, axis=1) + jnp.concatenate(yoff, axis=1) + xs * dskip_ref[...]


def _ssd_out_body(xbc_ref, dt_ref, z_ref, dtb_ref, a_ref, tri_ref, e_ref, dskip_ref, nw_ref, sf_ref, sb_ref, o_ref):
    n = SSD_CHUNK

    def body(k, carry):
        rows = pl.ds(pl.multiple_of(k * n, n), n)
        y = _ssd_out_chunk(xbc_ref[0, rows, :], dt_ref[0, rows, :], sf_ref[0, k], sb_ref[0, k], dtb_ref, a_ref, tri_ref,
                           e_ref, dskip_ref)
        o_ref[0, rows, :] = _rms(y * _silu(z_ref[0, rows, :]), nw_ref[...])
        return carry

    lax.fori_loop(0, o_ref.shape[1] // n, body, 0, unroll=2)


def _ssd(proj3, xbc_act, dt_bias, a_log, d_skip, norm_w):
    bsz, s, _ = proj3.shape
    n = SSD_CHUNK
    c = s // n
    tri, e = _ssd_tables(n)
    pad = jnp.zeros((LANES - 2 * SSD_HEADS,), F32)
    dtb = jnp.concatenate([dt_bias.reshape(-1), pad]).reshape(1, LANES)
    a = jnp.concatenate([-jnp.exp(a_log.reshape(-1)), pad]).reshape(1, LANES)
    dskip = jnp.repeat(d_skip, SSD_HEADDIM).reshape(1, GROUP_WIDTH)
    cdt = COL_DT // LANES
    const1 = lambda i: (0, 0)

    state_shape = jax.ShapeDtypeStruct((bsz, c, SSD_STATE, GROUP_WIDTH), F32)
    state_spec = pl.BlockSpec((1, c, SSD_STATE, GROUP_WIDTH), lambda i: (i, 0, 0, 0))
    sf, sb = pl.pallas_call(
        functools.partial(_ssd_state_body, c),
        grid=(bsz,),
        in_specs=[pl.BlockSpec((1, s, SSD_CONV_CH), lambda i: (i, 0, 0)),
                  pl.BlockSpec((1, s, LANES), lambda i: (i, 0, cdt)),
                  pl.BlockSpec((1, LANES), const1), pl.BlockSpec((1, LANES), const1),
                  pl.BlockSpec((n, n), const1), pl.BlockSpec((2 * LANES, 2 * GROUP_WIDTH), const1)],
        out_specs=[state_spec, state_spec],
        out_shape=[state_shape, state_shape],
        scratch_shapes=[pltpu.VMEM((SSD_STATE, GROUP_WIDTH), F32)],
        compiler_params=_cparams(("parallel",), 48),
    )(xbc_act, proj3, dtb, a, tri, e)

    k = min(SSD_OUT_CHUNKS, c)
    const2 = lambda i, j: (0, 0)
    return pl.pallas_call(
        _ssd_out_body,
        grid=(bsz, c // k),
        in_specs=[pl.BlockSpec((1, k * n, SSD_CONV_CH), lambda i, j: (i, j, 0)),
                  pl.BlockSpec((1, k * n, LANES), lambda i, j: (i, j, cdt)),
                  pl.BlockSpec((1, k * n, GROUP_WIDTH), lambda i, j: (i, j, COL_Z // GROUP_WIDTH)),
                  pl.BlockSpec((1, LANES), const2), pl.BlockSpec((1, LANES), const2),
                  pl.BlockSpec((n, n), const2), pl.BlockSpec((2 * LANES, 2 * GROUP_WIDTH), const2),
                  pl.BlockSpec((1, GROUP_WIDTH), const2), pl.BlockSpec((1, GROUP_WIDTH), const2),
                  pl.BlockSpec((1, k, SSD_STATE, GROUP_WIDTH), lambda i, j: (i, j, 0, 0)),
                  pl.BlockSpec((1, k, SSD_STATE, GROUP_WIDTH), lambda i, j: (i, j, 0, 0))],
        out_specs=pl.BlockSpec((1, k * n, GROUP_WIDTH), lambda i, j: (i, j, 0)),
        out_shape=jax.ShapeDtypeStruct((bsz, s, GROUP_WIDTH), F32),
        compiler_params=_cparams(("parallel", "parallel"), 32),
    )(xbc_act, proj3, proj3, dtb, a, tri, e, dskip, norm_w.reshape(1, -1), sf, sb)


def _hy_filter_body(z_ref, w1_ref, b1_ref, w2_ref, b2_ref, w3_ref, fr_ref, dl_ref, hs_ref, hd_ref):
    tl = z_ref.shape[0]
    z = z_ref[...]
    hid = jnp.sin(fr_ref[0:1, :] * (_dot_f32(z, w1_ref[...]) + b1_ref[...]))
    hid = jnp.sin(fr_ref[1:2, :] * (_dot_f32(hid, w2_ref[...]) + b2_ref[...]))
    filt = _dot_f32(hid, w3_ref[...])
    dec = jnp.exp(-z[:, 0:1] * dl_ref[...])
    row = pl.program_id(0) * tl + lax.broadcasted_iota(jnp.int32, (tl, 1), 0)
    for o in range(HY_ORDER):
        base = o * 2 * HY_WIDTH
        hf = filt[:, base:base + HY_WIDTH] * dec
        hb = jnp.where(row == 0, 0.0, filt[:, base + HY_WIDTH:base + 2 * HY_WIDTH] * dec)
        hs_ref[:, o * HY_WIDTH:(o + 1) * HY_WIDTH] = hf + hb
        hd_ref[:, o * HY_WIDTH:(o + 1) * HY_WIDTH] = hf - hb


def _hy_filters(l, w1, b1, w2, b2, w3, freq):
    z, deltas = _hyena_positions(l)
    hid = HY_FILTER_HIDDEN
    w1p = jnp.zeros((LANES, LANES), F32).at[:HY_EMB, :hid].set(w1)
    w2p = jnp.zeros((LANES, LANES), F32).at[:hid, :hid].set(w2)
    w3p = jnp.zeros((LANES, w3.shape[1]), F32).at[:hid].set(w3)
    b1p = jnp.zeros((1, LANES), F32).at[0, :hid].set(b1)
    b2p = jnp.zeros((1, LANES), F32).at[0, :hid].set(b2)
    frp = jnp.zeros((2, LANES), F32).at[:, :hid].set(freq)
    tl = min(256, l)
    ncol = HY_ORDER * HY_WIDTH
    const = lambda i: (0, 0)
    return pl.pallas_call(
        _hy_filter_body,
        grid=(l // tl,),
        in_specs=[pl.BlockSpec((tl, LANES), lambda i: (i, 0)),
                  pl.BlockSpec((LANES, LANES), const), pl.BlockSpec((1, LANES), const),
                  pl.BlockSpec((LANES, LANES), const), pl.BlockSpec((1, LANES), const),
                  pl.BlockSpec((LANES, w3.shape[1]), const), pl.BlockSpec((2, LANES), const),
                  pl.BlockSpec((1, HY_WIDTH), const)],
        out_specs=[pl.BlockSpec((tl, ncol), lambda i: (i, 0)), pl.BlockSpec((tl, ncol), lambda i: (i, 0))],
        out_shape=[jax.ShapeDtypeStruct((l, ncol), F32), jax.ShapeDtypeStruct((l, ncol), F32)],
        compiler_params=_cparams(("parallel",), 32),
    )(z, w1p, b1p, w2p, b2p, w3p, frp, deltas)


def _short_conv_rows(x_ref, w_ref, b_ref, r0, n):
    l = x_ref.shape[1]
    lo = max(r0 - SUBLANES, 0)
    hi = min(r0 + n + SUBLANES, l)
    win = x_ref[0, lo:hi, :]
    pad = jnp.zeros((SUBLANES, win.shape[1]), F32)
    if r0 == 0:
        win = jnp.concatenate([pad, win], axis=0)
    if r0 + n == l:
        win = jnp.concatenate([win, pad], axis=0)
    rows = n + 2 * SUBLANES
    prev = pltpu.roll(win, 1, axis=0)[SUBLANES:SUBLANES + n]
    nxt = pltpu.roll(win, rows - 1, axis=0)[SUBLANES:SUBLANES + n]
    return prev * w_ref[0:1, :] + win[SUBLANES:SUBLANES + n] * w_ref[1:2, :] + nxt * w_ref[2:3, :] + b_ref[...]


def _hy_conv_body(u0_ref, u1_ref, u2_ref, cw0_ref, cw1_ref, cw2_ref, cb0_ref, cb1_ref, cb2_ref, c_ref, s_ref,
                  hr0_ref, hi0_ref, hr1_ref, hi1_ref, b0_ref, b1_ref, o_ref, xb_scr, pre_scr, pim_scr, z_scr):
    l = u0_ref.shape[1]
    tf = min(HY_ROW_TILE, l)
    ntile = l // tf
    sgn = jnp.where(lax.broadcasted_iota(jnp.int32, (l, 1), 0) % 2 == 0, 1.0, -1.0)

    def long_conv(gate_ref, gw_ref, gb_ref, hr_ref, hi_ref, b_ref, out_ref):
        x = z_scr[...]
        xb_scr[...] = x.astype(BF16)
        x_nyq = jnp.sum(x * sgn, axis=0, keepdims=True) * hr_ref[l:l + 1, :]
        for r in range(ntile):
            rows = pl.ds(r * tf, tf)
            xc = _dot(c_ref[rows, :], xb_scr[...])
            xs = _dot(s_ref[rows, :], xb_scr[...])
            hre = hr_ref[rows, :]
            him = hi_ref[rows, :]
            pre_scr[rows, :] = (xc * hre + xs * him).astype(BF16)
            pim_scr[rows, :] = (xc * him - xs * hre).astype(BF16)
        for r in range(ntile):
            rows = pl.ds(r * tf, tf)
            y = _dot(c_ref[rows, :], pre_scr[...]) - _dot(s_ref[rows, :], pim_scr[...])
            y = y + sgn[r * tf:(r + 1) * tf] * x_nyq
            gate = _short_conv_rows(gate_ref, gw_ref, gb_ref, r * tf, tf)
            out_ref[rows, :] = gate * (y + z_scr[rows, :] * b_ref[0])

    for r in range(ntile):
        z_scr[pl.ds(r * tf, tf), :] = _short_conv_rows(u0_ref, cw0_ref, cb0_ref, r * tf, tf)
    long_conv(u1_ref, cw1_ref, cb1_ref, hr0_ref, hi0_ref, b0_ref, z_scr)
    long_conv(u2_ref, cw2_ref, cb2_ref, hr1_ref, hi1_ref, b1_ref, o_ref.at[0])


def _hy_long_conv(proj3, conv_w, conv_b, cmat, smat, hre, him, bias):
    bsz, l, _ = proj3.shape
    tc = 256
    nb = HY_WIDTH // tc
    off = COL_HY // tc
    const = lambda j, i: (0, 0)
    single = pl.Buffered(1)
    bias3 = bias.reshape(HY_ORDER, 1, HY_WIDTH)
    conv_b2 = conv_b.reshape(1, -1)

    def part(k):
        return [pl.BlockSpec((1, l, tc), lambda j, i: (i, 0, off + k * nb + j))]

    def part_w(k):
        return [pl.BlockSpec((HY_SHORT, tc), lambda j, i: (0, k * nb + j))]

    def part_b(k):
        return [pl.BlockSpec((1, tc), lambda j, i: (0, k * nb + j))]

    return pl.pallas_call(
        _hy_conv_body,
        grid=(nb, bsz),
        in_specs=part(0) + part(1) + part(2) + part_w(0) + part_w(1) + part_w(2) + part_b(0) + part_b(1) + part_b(2) + [
            pl.BlockSpec((l, l), const, pipeline_mode=single),
            pl.BlockSpec((l, l), const, pipeline_mode=single),
            pl.BlockSpec((l + SUBLANES, tc), lambda j, i: (0, j), pipeline_mode=single),
            pl.BlockSpec((l, tc), lambda j, i: (0, j), pipeline_mode=single),
            pl.BlockSpec((l + SUBLANES, tc), lambda j, i: (0, nb + j), pipeline_mode=single),
            pl.BlockSpec((l, tc), lambda j, i: (0, nb + j), pipeline_mode=single),
            pl.BlockSpec((1, 1, tc), lambda j, i: (0, 0, j)),
            pl.BlockSpec((1, 1, tc), lambda j, i: (1, 0, j))],
        out_specs=pl.BlockSpec((1, l, tc), lambda j, i: (i, 0, j)),
        out_shape=jax.ShapeDtypeStruct((bsz, l, HY_WIDTH), F32),
        scratch_shapes=[pltpu.VMEM((l, tc), BF16), pltpu.VMEM((l, tc), BF16), pltpu.VMEM((l, tc), BF16),
                        pltpu.VMEM((l, tc), F32)],
        compiler_params=_cparams(("parallel", "parallel"), 56),
    )(proj3, proj3, proj3, conv_w, conv_w, conv_w, conv_b2, conv_b2, conv_b2, cmat, smat, hre, him, hre, him,
      bias3, bias3)


def _hyena(proj3, conv_w, conv_b, w1, b1, w2, b2, w3, freq, bias):
    l = proj3.shape[1]
    hs, hd = _hy_filters(l, w1, b1, w2, b2, w3, freq)
    cmat, smat, c_sc, s_sc_neg = _dft_tables(l)
    tn = 256
    hre = _matmul(c_sc, hs.astype(BF16), l + SUBLANES, tn)
    him = _matmul(s_sc_neg, hd.astype(BF16), l, tn)
    return _hy_long_conv(proj3, conv_w, conv_b, cmat, smat, hre, him, bias)


def _out_proj_body(a_ref, b_ref, c_ref, d_ref, x_ref, w_ref, dn_ref, g_ref, beta_ref, o_ref, *rows_ref):
    gw = GROUP_WIDTH
    h = _dot(a_ref[...].astype(BF16), w_ref[0:gw, :])
    h += _dot(b_ref[...].astype(BF16), w_ref[gw:2 * gw, :])
    h += _dot(c_ref[...].astype(BF16), w_ref[2 * gw:3 * gw, :])
    h += _dot(_rms(d_ref[...], dn_ref[...]).astype(BF16), w_ref[3 * gw:4 * gw, :])
    y = _layer_norm(ALPHA * x_ref[...] + h, g_ref[...], beta_ref[...])
    o_ref[...] = y
    if rows_ref:
        _store_row_tiles(rows_ref[0], y)


def _store_row_tiles(dst_ref, y):
    rows = y.shape[0]
    for c in range(ROW_CHUNKS):
        dst_ref[pl.ds(c, rows, stride=ROW_CHUNKS), :] = y[:, c * LANES:(c + 1) * LANES]


def _load_row_tiles(src_ref, first, rows):
    return jnp.concatenate([src_ref[pl.ds(first * ROW_CHUNKS + c, rows, stride=ROW_CHUNKS), :]
                            for c in range(ROW_CHUNKS)], axis=1)


def _out_proj_ln(a, b, c, d, x, w_out, hy_norm, g, beta, emit_row_tiles):
    m = x.shape[0]
    tm = min(512, m)
    gw = GROUP_WIDTH
    const = lambda i: (0, 0)
    row = lambda i: (i, 0)
    out_specs = [pl.BlockSpec((tm, D_MODEL), row)]
    out_shape = [jax.ShapeDtypeStruct((m, D_MODEL), F32)]
    if emit_row_tiles:
        out_specs.append(pl.BlockSpec((tm * ROW_CHUNKS, LANES), row))
        out_shape.append(jax.ShapeDtypeStruct((m * ROW_CHUNKS, LANES), F32))
    return pl.pallas_call(
        _out_proj_body,
        grid=(m // tm,),
        in_specs=[pl.BlockSpec((tm, gw), row), pl.BlockSpec((tm, gw), row), pl.BlockSpec((tm, gw), row),
                  pl.BlockSpec((tm, gw), row), pl.BlockSpec((tm, D_MODEL), row),
                  pl.BlockSpec((D_MODEL, D_MODEL), const), pl.BlockSpec((1, gw), const),
                  pl.BlockSpec((1, D_MODEL), const), pl.BlockSpec((1, D_MODEL), const)],
        out_specs=out_specs,
        out_shape=out_shape,
        compiler_params=_cparams(("parallel",), 48),
    )(a, b, c, d, x, w_out, hy_norm.reshape(1, -1), g.reshape(1, -1), beta.reshape(1, -1))


def _ffn_body(x_ref, wg_ref, wu_ref, wd_ref, g_ref, beta_ref, o_ref, xb_ref, acc_ref):
    j = pl.program_id(1)

    @pl.when(j == 0)
    def _():
        xb_ref[...] = x_ref[...].astype(BF16)
        acc_ref[...] = jnp.zeros_like(acc_ref)

    xb = xb_ref[...]
    hidden = (_silu(_dot(xb, wg_ref[...])) * _dot(xb, wu_ref[...])).astype(BF16)
    acc_ref[...] += _dot(hidden, wd_ref[...])

    @pl.when(j == pl.num_programs(1) - 1)
    def _():
        o_ref[...] = _layer_norm(ALPHA * x_ref[...] + acc_ref[...], g_ref[...], beta_ref[...])


def _ffn_ln(x, wg, wu, wd, g, beta):
    m = x.shape[0]
    dff = wg.shape[1]
    tm = min(512, m)
    tf = 512
    const = lambda i, j: (0, 0)
    return pl.pallas_call(
        _ffn_body,
        grid=(m // tm, dff // tf),
        in_specs=[pl.BlockSpec((tm, D_MODEL), lambda i, j: (i, 0)),
                  pl.BlockSpec((D_MODEL, tf), lambda i, j: (0, j)),
                  pl.BlockSpec((D_MODEL, tf), lambda i, j: (0, j)),
                  pl.BlockSpec((tf, D_MODEL), lambda i, j: (j, 0)),
                  pl.BlockSpec((1, D_MODEL), const), pl.BlockSpec((1, D_MODEL), const)],
        out_specs=pl.BlockSpec((tm, D_MODEL), lambda i, j: (i, 0)),
        out_shape=jax.ShapeDtypeStruct((m, D_MODEL), F32),
        scratch_shapes=[pltpu.VMEM((tm, D_MODEL), BF16), pltpu.VMEM((tm, D_MODEL), F32)],
        compiler_params=_cparams(("parallel", "arbitrary"), 48),
    )(x, wg, wu, wd, g.reshape(1, -1), beta.reshape(1, -1))


def _router_body(x_ref, w_ref, idx_ref, gate_ref):
    x = x_ref[...]
    lane = lax.broadcasted_iota(jnp.int32, (x.shape[0], LANES), 1)
    neg = -jnp.inf
    logits = jnp.full((x.shape[0], LANES), neg, F32)
    for e in range(N_EXPERTS):
        logits = jnp.where(lane == e, jnp.sum(x * w_ref[e:e + 1, :], axis=-1, keepdims=True), logits)
    m1 = jnp.max(logits, axis=-1, keepdims=True)
    i1 = jnp.min(jnp.where(logits == m1, lane, LANES), axis=-1, keepdims=True)
    rest = jnp.where(lane == i1, neg, logits)
    m2 = jnp.max(rest, axis=-1, keepdims=True)
    i2 = jnp.min(jnp.where(rest == m2, lane, LANES), axis=-1, keepdims=True)
    e2 = jnp.exp(m2 - m1)
    den = 1.0 + e2
    idx_ref[...] = jnp.where(lane == 0, i1, jnp.where(lane == 1, i2, 0))
    gate_ref[...] = jnp.where(lane == 0, 1.0 / den, jnp.where(lane == 1, e2 / den, 0.0))


def _router(x, w_router):
    m = x.shape[0]
    tm = min(512, m)
    return pl.pallas_call(
        _router_body,
        grid=(m // tm,),
        in_specs=[pl.BlockSpec((tm, D_MODEL), lambda i: (i, 0)), pl.BlockSpec((N_EXPERTS, D_MODEL), lambda i: (0, 0))],
        out_specs=[pl.BlockSpec((tm, LANES), lambda i: (i, 0)), pl.BlockSpec((tm, LANES), lambda i: (i, 0))],
        out_shape=[jax.ShapeDtypeStruct((m, LANES), jnp.int32), jax.ShapeDtypeStruct((m, LANES), F32)],
        compiler_params=_cparams(("parallel",), 32),
    )(x, w_router.T)


def _row_copy(src_ref, src_row, buf_ref, buf_row, sem):
    src = src_ref.at[pl.ds(pl.multiple_of(src_row * ROW_CHUNKS, ROW_CHUNKS), ROW_CHUNKS)]
    dst = buf_ref.at[pl.ds(pl.multiple_of(buf_row * ROW_CHUNKS, ROW_CHUNKS), ROW_CHUNKS)]
    return pltpu.make_async_copy(src, dst, sem)


def _start_row_gather(idx_ref, n, src_ref, buf_ref, sem):
    def start(r, carry):
        _row_copy(src_ref, idx_ref[0, 0, r], buf_ref, r, sem).start()
        return carry

    lax.fori_loop(0, n, start, 0, unroll=DMA_LOOP_UNROLL)


def _wait_row_gather(n, src_ref, buf_ref, sem):
    def wait(r, carry):
        _row_copy(src_ref, 0, buf_ref, r, sem).wait()
        return carry

    lax.fori_loop(0, n, wait, 0, unroll=DMA_LOOP_UNROLL)


def _pipelined_row_gather(idx_cur_ref, idx_next_ref, n, src_ref, buf, sems):
    i = pl.program_id(0)
    slot = i % 2

    @pl.when(i == 0)
    def _():
        _start_row_gather(idx_cur_ref, n, src_ref, buf.at[0], sems.at[0])

    @pl.when(i + 1 < pl.num_programs(0))
    def _():
        _start_row_gather(idx_next_ref, n, src_ref, buf.at[1 - slot], sems.at[1 - slot])

    _wait_row_gather(n, src_ref, buf.at[slot], sems.at[slot])
    return slot


def _idx_specs(n, nblk):
    return [pl.BlockSpec((1, 1, n), lambda i: (i, 0, 0), memory_space=pltpu.SMEM),
            pl.BlockSpec((1, 1, n), lambda i: (jnp.minimum(i + 1, nblk - 1), 0, 0), memory_space=pltpu.SMEM)]


def _gather_x_body(idx_cur_ref, idx_next_ref, src_ref, o_ref, buf, sems):
    rows = o_ref.shape[0]
    slot = _pipelined_row_gather(idx_cur_ref, idx_next_ref, rows, src_ref, buf, sems)
    o_ref[...] = _load_row_tiles(buf.at[slot], 0, rows).astype(BF16)


def _gather_x(x_tiles, slot_tok):
    cap = slot_tok.shape[0]
    rows = GATHER_ROWS
    assert cap % rows == 0
    nblk = cap // rows
    idx = slot_tok.reshape(nblk, 1, rows)
    return pl.pallas_call(
        _gather_x_body,
        grid=(nblk,),
        in_specs=_idx_specs(rows, nblk) + [pl.BlockSpec(memory_space=pl.ANY)],
        out_specs=pl.BlockSpec((rows, D_MODEL), lambda i: (i, 0)),
        out_shape=jax.ShapeDtypeStruct((cap, D_MODEL), BF16),
        scratch_shapes=[pltpu.VMEM((2, rows * ROW_CHUNKS, LANES), F32), pltpu.SemaphoreType.DMA((2,))],
        compiler_params=pltpu.CompilerParams(dimension_semantics=("arbitrary",), disable_bounds_checks=True),
    )(idx, idx, x_tiles)


def _expert_body(be_ref, nv_ref, x_ref, wg_ref, wu_ref, wd_ref, o_ref, acc_ref):
    i = pl.program_id(0)
    j = pl.program_id(1)
    half = x_ref.shape[0] // 2

    @pl.when(j == 0)
    def _():
        acc_ref[...] = jnp.zeros_like(acc_ref)

    def swiglu_rows(r0):
        xb = x_ref[r0:r0 + half, :]
        hidden = (_silu(_dot(xb, wg_ref[0])) * _dot(xb, wu_ref[0])).astype(BF16)
        acc_ref[r0:r0 + half, :] += _dot(hidden, wd_ref[0])

    @pl.when(nv_ref[i] > 0)
    def _():
        swiglu_rows(0)

    @pl.when(nv_ref[i] > half)
    def _():
        swiglu_rows(half)

    @pl.when(j == pl.num_programs(1) - 1)
    def _():
        _store_row_tiles(o_ref, acc_ref[...])


def _expert_ffn(x_slots, block_expert, block_valid, wg, wu, wd, tm):
    cap = x_slots.shape[0]
    dff = wg.shape[2]
    tf = 512

    def ff_tile(i, j, nv):
        return jnp.where(nv[i] > 0, j, 0)

    grid_spec = pltpu.PrefetchScalarGridSpec(
        num_scalar_prefetch=2,
        grid=(cap // tm, dff // tf),
        in_specs=[pl.BlockSpec((tm, D_MODEL), lambda i, j, be, nv: (i, 0)),
                  pl.BlockSpec((1, D_MODEL, tf), lambda i, j, be, nv: (be[i], 0, ff_tile(i, j, nv))),
                  pl.BlockSpec((1, D_MODEL, tf), lambda i, j, be, nv: (be[i], 0, ff_tile(i, j, nv))),
                  pl.BlockSpec((1, tf, D_MODEL), lambda i, j, be, nv: (be[i], ff_tile(i, j, nv), 0))],
        out_specs=pl.BlockSpec((tm * ROW_CHUNKS, LANES), lambda i, j, be, nv: (i, 0)),
        scratch_shapes=[pltpu.VMEM((tm, D_MODEL), F32)],
    )
    return pl.pallas_call(
        _expert_body,
        grid_spec=grid_spec,
        out_shape=jax.ShapeDtypeStruct((cap * ROW_CHUNKS, LANES), F32),
        compiler_params=_cparams(("parallel", "arbitrary"), 56),
    )(block_expert, block_valid, x_slots, wg, wu, wd)


def _combine_body(idx_cur_ref, idx_next_ref, x_ref, gt_ref, g_ref, beta_ref, y_ref, o_ref, buf, sems):
    rows = x_ref.shape[0]
    slot = _pipelined_row_gather(idx_cur_ref, idx_next_ref, TOP_K * rows, y_ref, buf, sems)
    gt = gt_ref[...]
    f = (gt[:, 0:1] * _load_row_tiles(buf.at[slot], 0, rows)
         + gt[:, 1:2] * _load_row_tiles(buf.at[slot], rows, rows))
    o_ref[...] = _layer_norm(ALPHA * x_ref[...] + f, g_ref[...], beta_ref[...])


def _combine_ln(x, y_tiles, pos, gates, g, beta):
    m = x.shape[0]
    rows = min(COMBINE_ROWS, m)
    nblk = m // rows
    n = TOP_K * rows
    idx = pos.reshape(nblk, rows, TOP_K).transpose(0, 2, 1).reshape(nblk, 1, n)
    const = lambda i: (0, 0)
    return pl.pallas_call(
        _combine_body,
        grid=(nblk,),
        in_specs=_idx_specs(n, nblk) + [
            pl.BlockSpec((rows, D_MODEL), lambda i: (i, 0)),
            pl.BlockSpec((rows, LANES), lambda i: (i, 0)),
            pl.BlockSpec((1, D_MODEL), const), pl.BlockSpec((1, D_MODEL), const),
            pl.BlockSpec(memory_space=pl.ANY)],
        out_specs=pl.BlockSpec((rows, D_MODEL), lambda i: (i, 0)),
        out_shape=jax.ShapeDtypeStruct((m, D_MODEL), F32),
        scratch_shapes=[pltpu.VMEM((2, n * ROW_CHUNKS, LANES), F32), pltpu.SemaphoreType.DMA((2,))],
        compiler_params=pltpu.CompilerParams(dimension_semantics=("arbitrary",), disable_bounds_checks=True,
                                             vmem_limit_bytes=48 * 1024 * 1024),
    )(idx, idx, x, gates, g.reshape(1, -1), beta.reshape(1, -1), y_tiles)


def _moe_ln(x, x_tiles, w_router, wg, wu, wd, g, beta, tm):
    m = x.shape[0]
    n_asg = m * TOP_K
    idx, gates = _router(x, w_router)
    e_flat = idx[:, :TOP_K].reshape(-1)
    onehot = (e_flat[:, None] == jnp.arange(N_EXPERTS, dtype=jnp.int32)[None, :]).astype(jnp.int32)
    csum = jnp.cumsum(onehot, axis=0)
    counts = csum[-1]
    rank = jnp.sum(csum * onehot, axis=1) - 1
    padded = (counts + tm - 1) // tm * tm
    pend = jnp.cumsum(padded)
    pstart = pend - padded
    dest = (jnp.sum(pstart[None, :] * onehot, axis=1) + rank).astype(jnp.int32)
    cap = n_asg + N_EXPERTS * tm
    tok = jnp.arange(n_asg, dtype=jnp.int32) // TOP_K
    slot_tok = jnp.zeros((cap,), jnp.int32).at[dest].set(tok, unique_indices=True)
    nblk = cap // tm
    block_start = jnp.arange(nblk, dtype=pend.dtype) * tm
    block_expert = jnp.minimum(jnp.searchsorted(pend, block_start, side='right'), N_EXPERTS - 1).astype(jnp.int32)
    block_valid = jnp.clip((pstart + counts)[block_expert] - block_start, 0, tm).astype(jnp.int32)

    x_slots = _gather_x(x_tiles, slot_tok)
    y_tiles = _expert_ffn(x_slots, block_expert, block_valid, wg, wu, wd, tm)
    return _combine_ln(x, y_tiles, dest.reshape(m, TOP_K), gates, g, beta)


def _mixer_ln(x2, bsz, s, p, emit_row_tiles):
    proj = _matmul(x2, p['w_in'], min(1024, x2.shape[0]), PROJ_TILE)
    proj3 = proj.reshape(bsz, s, PROJ_COLS)
    cos_slab, sin_slab, cos4, sin4 = _rope_tables(s)
    out_a = _mla(proj3, cos_slab, sin_slab, p['mla_q_norm'], p['w_q'], p['w_qr'], p['mla_kv_norm'], p['w_kv'],
                 p['mla_out_norm'])
    out_b = _retention(proj3, cos4, sin4)
    xbc_act = _dwconv(proj3, COL_XBC, SSD_CONV_CH, p['ssd_conv_w'], p['ssd_conv_b'], act=True)
    out_c = _ssd(proj3, xbc_act, p['ssd_dt_bias'], p['ssd_a_log'], p['ssd_d'], p['ssd_norm'])
    out_d = _hyena(proj3, p['hy_conv_w'], p['hy_conv_b'], p['hy_w1'], p['hy_b1'], p['hy_w2'], p['hy_b2'], p['hy_w3'],
                   p['hy_freq'], p['hy_bias'])
    m = bsz * s
    gw = GROUP_WIDTH
    return _out_proj_ln(out_a.reshape(m, gw), out_b.reshape(m, gw), out_c.reshape(m, gw), out_d.reshape(m, gw),
                        x2, p['w_out'], p['hy_out_norm'], p['ln1_g'], p['ln1_b'], emit_row_tiles)


_MIXER_KEYS = ('mla_q_norm', 'mla_kv_norm', 'mla_out_norm', 'ssd_conv_w', 'ssd_conv_b', 'ssd_dt_bias', 'ssd_a_log',
               'ssd_d', 'ssd_norm', 'hy_conv_w', 'hy_conv_b', 'hy_w1', 'hy_b1', 'hy_w2', 'hy_b2', 'hy_w3', 'hy_freq',
               'hy_bias', 'hy_out_norm', 'ln1_g', 'ln1_b')

MOE_TM = 1024


def kernel(x, w_in, mla_q_norm, mla_w_uq, mla_kv_norm, mla_w_ukv, mla_out_norm, ssd_conv_w, ssd_conv_b, ssd_dt_bias, ssd_a_log, ssd_d, ssd_norm, hy_conv_w, hy_conv_b, hy_w1, hy_b1, hy_w2, hy_b2, hy_w3, hy_freq, hy_bias, hy_out_norm, w_out, ln1_g, ln1_b, ln2_g, ln2_b, ffn_w_gate, ffn_w_up, ffn_w_down, moe_router, moe_w_gate, moe_w_up, moe_w_down):
    args = dict(locals())
    bsz, s, d = x.shape
    x2 = x.reshape(bsz * s, d)
    for layer in range(DEPTH):
        p = {k: args[k][layer] for k in _MIXER_KEYS}
        p['w_in'] = _prep_w_in(w_in[layer])
        p['w_q'], p['w_qr'] = _prep_w_uq(mla_w_uq[layer])
        p['w_kv'] = _prep_w_ukv(mla_w_ukv[layer])
        p['w_out'] = w_out[layer].astype(BF16)
        j = layer // 2
        if layer % 2 == 0:
            x2, = _mixer_ln(x2, bsz, s, p, False)
            x2 = _ffn_ln(x2, ffn_w_gate[j].astype(BF16), ffn_w_up[j].astype(BF16), ffn_w_down[j].astype(BF16),
                         ln2_g[layer], ln2_b[layer])
        else:
            x2, x_tiles = _mixer_ln(x2, bsz, s, p, True)
            x2 = _moe_ln(x2, x_tiles, moe_router[j], moe_w_gate[j].astype(BF16), moe_w_up[j].astype(BF16),
                         moe_w_down[j].astype(BF16), ln2_g[layer], ln2_b[layer], MOE_TM)
    return x2.reshape(bsz, s, d)
```

```python
import functools
import math

import numpy as np
import jax
import jax.numpy as jnp
from jax import lax
from jax.experimental import pallas as pl
from jax.experimental.pallas import tpu as pltpu

F32 = jnp.float32
BF16 = jnp.bfloat16
HIGHEST = lax.Precision.HIGHEST

D_MODEL = 2048
DEPTH = 2
GROUP_WIDTH = 512
MLA_HEADS = 4
MLA_NOPE = 128
MLA_ROPE = 64
MLA_V = 128
MLA_Q_LORA = 384
MLA_KV_LORA = 256
RET_HEADS = 4
RET_DV = 128
RET_DK = 64
RET_DECAY_EXP_FWD = 5.0
RET_DECAY_EXP_BWD = 5.5
SSD_HEADDIM = 64
SSD_HEADS = 8
SSD_GROUPS = 2
SSD_STATE = 128
SSD_CONV = 5
SSD_CONV_CH = 1024
HY_ORDER = 2
HY_WIDTH = 512
HY_SHORT = 3
HY_EMB = 33
HY_FILTER_HIDDEN = 64
HY_MIN_DECAY = math.log(1e-2) / 1.5
HY_MAX_DECAY = math.log(1e-2) / 0.3
N_EXPERTS = 8
TOP_K = 2
ROPE_BASE = 10000.0
ALPHA = (2 * DEPTH) ** 0.25

V7X_VMEM_BYTES = 64 * 1024 * 1024
LANES = 128
SUBLANES = 8

PROJ_COLS = 6400
COL_HY = 0
COL_QC = 1536
COL_RQ = 2048
COL_KVPE = 4096
COL_Z = 4608
COL_XBC = 5120
COL_DT = 6144
PROJ_TILE = 1280

RET_CHUNK = 256
SSD_CHUNK = 128
HY_ROW_TILE = 512
CONV_ROWS = 1024
RET_OUT_CHUNKS = 4
SSD_OUT_CHUNKS = 8
ROW_CHUNKS = D_MODEL // LANES
GATHER_ROWS = 512
COMBINE_ROWS = 256
DMA_LOOP_UNROLL = 8


def _cparams(semantics, vmem_mb):
    assert vmem_mb * 1024 * 1024 < V7X_VMEM_BYTES
    return pltpu.CompilerParams(dimension_semantics=semantics, vmem_limit_bytes=vmem_mb * 1024 * 1024)


def _sigmoid(x):
    return 1.0 / (1.0 + jnp.exp(-x))


def _silu(x):
    return x * _sigmoid(x)


def _rms(x, w, eps=1e-6):
    return x * lax.rsqrt(jnp.mean(x * x, axis=-1, keepdims=True) + eps) * w


def _layer_norm(y, g, b, eps=1e-5):
    mu = jnp.mean(y, axis=-1, keepdims=True)
    d = y - mu
    var = jnp.mean(d * d, axis=-1, keepdims=True)
    return d * lax.rsqrt(var + eps) * g + b


def _dot(a, b):
    return jnp.dot(a, b, preferred_element_type=F32)


def _dot_nt(a, b):
    return lax.dot_general(a, b, (((1,), (1,)), ((), ())), preferred_element_type=F32)


def _dot_tn(a, b):
    return lax.dot_general(a, b, (((0,), (0,)), ((), ())), preferred_element_type=F32)


def _dot_f32(a, b):
    return jnp.dot(a, b, preferred_element_type=F32, precision=HIGHEST)


def _mm_body(x_ref, w_ref, o_ref, xb_ref):
    @pl.when(pl.program_id(1) == 0)
    def _():
        xb_ref[...] = x_ref[...].astype(BF16)

    o_ref[...] = _dot(xb_ref[...], w_ref[...])


def _matmul(x, w, tm, tn, vmem_mb=48):
    m, k = x.shape
    n = w.shape[1]
    assert m % tm == 0 and n % tn == 0
    return pl.pallas_call(
        _mm_body,
        grid=(m // tm, n // tn),
        in_specs=[pl.BlockSpec((tm, k), lambda i, j: (i, 0)),
                  pl.BlockSpec((k, tn), lambda i, j: (0, j))],
        out_specs=pl.BlockSpec((tm, tn), lambda i, j: (i, j)),
        out_shape=jax.ShapeDtypeStruct((m, n), F32),
        scratch_shapes=[pltpu.VMEM((tm, k), BF16)],
        compiler_params=_cparams(("parallel", "arbitrary"), vmem_mb),
    )(x, w)


def _rot_half_cols(w, heads):
    k = w.shape[0]
    w = w.reshape(k, heads, 2, 32)
    return jnp.stack([-w[:, :, 1], w[:, :, 0]], axis=2).reshape(k, heads * 64)


def _prep_w_in(w):
    k = w.shape[0]

    def sl(a, b):
        return w[:, a:b]

    def zc(n):
        return jnp.zeros((k, n), w.dtype)

    q_c, kv_c, k_pe = sl(0, 384), sl(384, 640), sl(640, 704)
    r_q, r_k, r_v, r_g = sl(704, 960), sl(960, 1216), sl(1216, 1728), sl(1728, 2240)
    m_z, m_xbc, m_dt, h_u = sl(2240, 2752), sl(2752, 3776), sl(3776, 3792), sl(3792, 5328)
    cols = [h_u, q_c, zc(128),
            r_q, r_k, _rot_half_cols(r_q, RET_HEADS), _rot_half_cols(r_k, RET_HEADS), r_v, r_g,
            kv_c, k_pe, zc(64), _rot_half_cols(k_pe, 1), zc(64),
            m_z, m_xbc, m_dt, zc(112), zc(PROJ_COLS - COL_DT - LANES)]
    out = jnp.concatenate(cols, axis=1).astype(BF16)
    assert out.shape[1] == PROJ_COLS
    return out


def _prep_w_uq(w):
    k = w.shape[0]
    w = w.reshape(k, MLA_HEADS, MLA_NOPE + MLA_ROPE)
    nope, rope = w[:, :, :MLA_NOPE], w[:, :, MLA_NOPE:]
    z = jnp.zeros((k, MLA_HEADS, 64), w.dtype)
    main = jnp.concatenate([nope, rope, z], axis=2).reshape(k, MLA_HEADS * 256)
    rr = _rot_half_cols(rope.reshape(k, MLA_HEADS * 64), MLA_HEADS).reshape(k, MLA_HEADS, 64)
    rot = jnp.concatenate([rr, z], axis=2).reshape(k, MLA_HEADS * 128)
    return main.astype(BF16), rot.astype(BF16)


def _prep_w_ukv(w):
    k = w.shape[0]
    w = w.reshape(k, MLA_HEADS, MLA_NOPE + MLA_V)
    return jnp.concatenate([w[:, :, :MLA_NOPE].reshape(k, -1), w[:, :, MLA_NOPE:].reshape(k, -1)], axis=1).astype(BF16)


def _rope_tables(s):
    half = 32
    inv_freq = ROPE_BASE ** (-jnp.arange(half, dtype=F32) * 2.0 / 64)
    ang = jnp.arange(s, dtype=F32)[:, None] * inv_freq[None, :]
    cos, sin = jnp.cos(ang), jnp.sin(ang)
    cos64 = jnp.concatenate([cos, cos], axis=1)
    sin64 = jnp.concatenate([sin, sin], axis=1)
    z = jnp.zeros((s, 64), F32)
    return (jnp.concatenate([cos64, z], axis=1), jnp.concatenate([sin64, z], axis=1),
            jnp.tile(cos64, (1, 4)), jnp.tile(sin64, (1, 4)))


def _ret_tables(n):
    heads = jnp.arange(RET_HEADS, dtype=F32)
    lgf = jnp.log1p(-jnp.exp2(-RET_DECAY_EXP_FWD - heads))
    lgb = jnp.log1p(-jnp.exp2(-RET_DECAY_EXP_BWD - heads))
    idx = jnp.arange(n, dtype=F32)

    def lanes(tab, width):
        return jnp.repeat(tab, width, axis=1)

    kdf = lanes(jnp.exp((n - 1.0 - idx)[:, None] * lgf), RET_DK)
    kdb = lanes(jnp.exp(idx[:, None] * lgb), RET_DK)
    qdf = lanes(jnp.exp((idx + 1.0)[:, None] * lgf), RET_DK)
    qdb = lanes(jnp.exp((n - idx)[:, None] * lgb), RET_DK)
    cdf = lanes(jnp.exp(n * lgf)[None, :], RET_DV)
    cdb = lanes(jnp.exp(n * lgb)[None, :], RET_DV)
    diff = idx[:, None] - idx[None, :]
    dec = jnp.where(diff[None] >= 0,
                    jnp.exp(jnp.maximum(diff, 0.0)[None] * lgf[:, None, None]),
                    jnp.exp(jnp.maximum(-diff, 0.0)[None] * lgb[:, None, None]))
    bd = (jnp.arange(RET_HEADS * RET_DK)[:, None] // RET_DK == jnp.arange(RET_HEADS * RET_DV)[None, :] // RET_DV)
    return kdf, kdb, qdf, qdb, cdf, cdb, dec.astype(F32), bd.astype(F32)


def _ssd_tables(n):
    tri = (jnp.arange(n)[:, None] >= jnp.arange(n)[None, :]).astype(F32)
    lane_head = jnp.arange(GROUP_WIDTH) // SSD_HEADDIM
    r = jnp.arange(LANES)
    ef = (r[:, None] == lane_head[None, :]).astype(BF16)
    eb = (r[:, None] == lane_head[None, :] + SSD_HEADS).astype(BF16)
    e = jnp.concatenate([ef, eb], axis=1)
    return tri, jnp.concatenate([e, e], axis=0)


def _dft_tables(l):
    f = jnp.arange(l, dtype=jnp.int32)
    k = (f[:, None] * f[None, :]) % (2 * l)
    ang = k.astype(F32) * (math.pi / l)
    c, s = jnp.cos(ang), jnp.sin(ang)
    sc = jnp.where(f == 0, 1.0, 2.0).astype(F32)[:, None] / (2.0 * l)
    nyq = jnp.where(f % 2 == 0, 1.0, -1.0).astype(F32)[None, :] / (2.0 * l)
    c_sc = jnp.concatenate([c * sc, nyq, jnp.zeros((7, l), F32)], axis=0)
    return c.astype(BF16), s.astype(BF16), c_sc.astype(BF16), (-s * sc).astype(BF16)


def _hyena_positions(l):
    t = jnp.linspace(0.0, 1.0, l, dtype=F32)[:, None]
    bands = (HY_EMB - 1) // 2
    ang = 2.0 * math.pi * jnp.arange(l, dtype=F32)[:, None] / l
    f = jnp.linspace(1e-4, bands - 1, bands, dtype=F32)[None, :]
    z = jnp.concatenate([t, jnp.cos(f * ang), -jnp.sin(f * ang), jnp.zeros((l, LANES - HY_EMB), F32)], axis=-1)
    deltas = jnp.abs(jnp.linspace(HY_MIN_DECAY, HY_MAX_DECAY, HY_WIDTH, dtype=F32))[None, :]
    return z, deltas


def _conv_body(taps, act, x_ref, p_ref, n_ref, w_ref, b_ref, o_ref):
    r = pl.program_id(1)
    tr = x_ref.shape[1]
    half = taps // 2
    prev = jnp.where(r > 0, p_ref[0], 0.0)
    nxt = jnp.where(r < pl.num_programs(1) - 1, n_ref[0], 0.0)
    win = jnp.concatenate([prev, x_ref[0], nxt], axis=0)
    rows = tr + 2 * SUBLANES
    acc = jnp.broadcast_to(b_ref[...], (tr, x_ref.shape[2]))
    for t in range(taps):
        sh = (half - t) % rows
        shifted = win if sh == 0 else pltpu.roll(win, sh, axis=0)
        acc = acc + shifted[SUBLANES:SUBLANES + tr] * w_ref[t:t + 1, :]
    if act:
        acc = _silu(acc)
    o_ref[0] = acc


def _dwconv(proj3, col0, width, w, b, act):
    bsz, s, _ = proj3.shape
    taps = w.shape[0]
    cb = 512
    tr = min(CONV_ROWS, s)
    off = col0 // cb
    nb8 = s // SUBLANES
    t8 = tr // SUBLANES
    return pl.pallas_call(
        functools.partial(_conv_body, taps, act),
        grid=(bsz, s // tr, width // cb),
        in_specs=[pl.BlockSpec((1, tr, cb), lambda i, r, c: (i, r, c + off)),
                  pl.BlockSpec((1, SUBLANES, cb), lambda i, r, c: (i, jnp.maximum(r * t8 - 1, 0), c + off)),
                  pl.BlockSpec((1, SUBLANES, cb), lambda i, r, c: (i, jnp.minimum((r + 1) * t8, nb8 - 1), c + off)),
                  pl.BlockSpec((taps, cb), lambda i, r, c: (0, c)),
                  pl.BlockSpec((1, cb), lambda i, r, c: (0, c))],
        out_specs=pl.BlockSpec((1, tr, cb), lambda i, r, c: (i, r, c)),
        out_shape=jax.ShapeDtypeStruct((bsz, s, width), F32),
        compiler_params=_cparams(("parallel", "parallel", "parallel"), 40),
    )(proj3, proj3, proj3, w, b.reshape(1, width))


def _mla_body(qc_ref, kvpe_ref, cq_ref, sq_ref, ck_ref, sk_ref, qn_ref, wq_ref, wqr_ref, kvn_ref, wkv_ref, on_ref,
              o_ref, k_scr, v_scr):
    @pl.when(pl.program_id(1) == 0)
    def _():
        kvpe = kvpe_ref[0]
        kvn = _rms(kvpe[:, :MLA_KV_LORA], kvn_ref[...]).astype(BF16)
        kpe = (kvpe[:, 256:384] * ck_ref[...] + kvpe[:, 384:512] * sk_ref[...]).astype(BF16)
        for h in range(MLA_HEADS):
            kn = _dot(kvn, wkv_ref[:, h * 128:(h + 1) * 128]).astype(BF16)
            k_scr[h] = jnp.concatenate([kn, kpe], axis=1)
            v_scr[h] = _dot(kvn, wkv_ref[:, 512 + h * 128:512 + (h + 1) * 128]).astype(BF16)

    scale = (MLA_NOPE + MLA_ROPE) ** -0.5
    qn = _rms(qc_ref[0][:, :MLA_Q_LORA], qn_ref[...]).astype(BF16)
    outs = []
    for h in range(MLA_HEADS):
        qm = _dot(qn, wq_ref[:, h * 256:(h + 1) * 256])
        qr = _dot(qn, wqr_ref[:, h * 128:(h + 1) * 128])
        qpe = qm[:, 128:] * cq_ref[...] + qr * sq_ref[...]
        qh = (jnp.concatenate([qm[:, :128], qpe], axis=1) * scale).astype(BF16)
        sc = _dot_nt(qh, k_scr[h])
        p = jnp.exp(sc - jnp.max(sc, axis=-1, keepdims=True))
        den = jnp.sum(p, axis=-1, keepdims=True)
        outs.append(_dot(p.astype(BF16), v_scr[h]) / den)
    o_ref[0] = _rms(jnp.concatenate(outs, axis=1), on_ref[...])


def _mla(proj3, cos_slab, sin_slab, q_norm, w_q, w_qr, kv_norm, w_kv, out_norm):
    bsz, s, _ = proj3.shape
    tq = min(512, s)
    const = lambda i, j: (0, 0)
    return pl.pallas_call(
        _mla_body,
        grid=(bsz, s // tq),
        in_specs=[pl.BlockSpec((1, tq, 512), lambda i, j: (i, j, COL_QC // 512)),
                  pl.BlockSpec((1, s, 512), lambda i, j: (i, 0, COL_KVPE // 512)),
                  pl.BlockSpec((tq, 128), lambda i, j: (j, 0)),
                  pl.BlockSpec((tq, 128), lambda i, j: (j, 0)),
                  pl.BlockSpec((s, 128), const),
                  pl.BlockSpec((s, 128), const),
                  pl.BlockSpec((1, MLA_Q_LORA), const),
                  pl.BlockSpec(w_q.shape, const),
                  pl.BlockSpec(w_qr.shape, const),
                  pl.BlockSpec((1, MLA_KV_LORA), const),
                  pl.BlockSpec(w_kv.shape, const),
                  pl.BlockSpec((1, GROUP_WIDTH), const)],
        out_specs=pl.BlockSpec((1, tq, GROUP_WIDTH), lambda i, j: (i, j, 0)),
        out_shape=jax.ShapeDtypeStruct((bsz, s, GROUP_WIDTH), F32),
        scratch_shapes=[pltpu.VMEM((MLA_HEADS, s, 256), BF16), pltpu.VMEM((MLA_HEADS, s, MLA_V), BF16)],
        compiler_params=_cparams(("parallel", "arbitrary"), 48),
    )(proj3, proj3, cos_slab, sin_slab, cos_slab, sin_slab, q_norm.reshape(1, -1), w_q, w_qr,
      kv_norm.reshape(1, -1), w_kv, out_norm.reshape(1, -1))


def _ret_state_body(nchunks, rk_ref, rkr_ref, rv_ref, cos_ref, sin_ref, kdf_ref, kdb_ref, cdf_ref, cdb_ref, bd_ref,
                    sf_ref, sb_ref, st_scr):
    n = kdf_ref.shape[0]

    def sweep(out_ref, kd_ref, cd_ref, chunk_of):
        st_scr[...] = jnp.zeros_like(st_scr)

        def body(t, carry):
            c = chunk_of(t)
            rows = pl.ds(pl.multiple_of(c * n, n), n)
            k = (rk_ref[0, rows, :] * cos_ref[rows, :] + rkr_ref[0, rows, :] * sin_ref[rows, :]) * (RET_DK ** -0.5)
            out_ref[0, c] = st_scr[...]
            new = _dot_tn((k * kd_ref[...]).astype(BF16), rv_ref[0, rows, :].astype(BF16))
            st_scr[...] = st_scr[...] * cd_ref[...] + new * bd_ref[...]
            return carry

        lax.fori_loop(0, nchunks, body, 0, unroll=2)

    sweep(sf_ref, kdf_ref, cdf_ref, lambda t: t)
    sweep(sb_ref, kdb_ref, cdb_ref, lambda t: nchunks - 1 - t)


def _ret_out_chunk(rq, rk, rqr, rkr, rv, rg, cos, sin, sf, sb, qdf_ref, qdb_ref, dec_ref):
    q = rq * cos + rqr * sin
    kb = ((rk * cos + rkr * sin) * (RET_DK ** -0.5)).astype(BF16)
    vb = rv.astype(BF16)
    cross = (_dot((q * qdf_ref[...]).astype(BF16), sf.astype(BF16))
             + _dot((q * qdb_ref[...]).astype(BF16), sb.astype(BF16)))
    lane_head = lax.broadcasted_iota(jnp.int32, (1, RET_HEADS * RET_DK), 1) // RET_DK
    outs = []
    for h in range(RET_HEADS):
        qh = jnp.where(lane_head == h, q, 0.0).astype(BF16)
        sc = _dot_nt(qh, kb) * dec_ref[h]
        y = _dot(sc.astype(BF16), vb[:, h * RET_DV:(h + 1) * RET_DV]) + cross[:, h * RET_DV:(h + 1) * RET_DV]
        mu = jnp.mean(y, axis=-1, keepdims=True)
        d = y - mu
        var = jnp.mean(d * d, axis=-1, keepdims=True)
        outs.append(d * lax.rsqrt(var + 1e-6))
    return jnp.concatenate(outs, axis=1) * _silu(rg)


def _ret_out_body(rq_ref, rk_ref, rqr_ref, rkr_ref, rv_ref, rg_ref, cos_ref, sin_ref, qdf_ref, qdb_ref, dec_ref,
                  sf_ref, sb_ref, o_ref):
    n = qdf_ref.shape[0]

    def body(k, carry):
        rows = pl.ds(pl.multiple_of(k * n, n), n)
        o_ref[0, rows, :] = _ret_out_chunk(rq_ref[0, rows, :], rk_ref[0, rows, :], rqr_ref[0, rows, :],
                                           rkr_ref[0, rows, :], rv_ref[0, rows, :], rg_ref[0, rows, :],
                                           cos_ref[rows, :], sin_ref[rows, :], sf_ref[0, k], sb_ref[0, k],
                                           qdf_ref, qdb_ref, dec_ref)
        return carry

    lax.fori_loop(0, o_ref.shape[1] // n, body, 0, unroll=2)


def _retention(proj3, cos4, sin4):
    bsz, s, _ = proj3.shape
    n = min(RET_CHUNK, s)
    c = s // n
    kdf, kdb, qdf, qdb, cdf, cdb, dec, bd = _ret_tables(n)
    c256 = COL_RQ // 256
    c512 = COL_RQ // 512
    const1 = lambda i: (0, 0)

    state_shape = jax.ShapeDtypeStruct((bsz, c, RET_HEADS * RET_DK, GROUP_WIDTH), F32)
    state_spec = pl.BlockSpec((1, c, RET_HEADS * RET_DK, GROUP_WIDTH), lambda i: (i, 0, 0, 0))
    sf, sb = pl.pallas_call(
        functools.partial(_ret_state_body, c),
        grid=(bsz,),
        in_specs=[pl.BlockSpec((1, s, 256), lambda i: (i, 0, c256 + 1)),
                  pl.BlockSpec((1, s, 256), lambda i: (i, 0, c256 + 3)),
                  pl.BlockSpec((1, s, 512), lambda i: (i, 0, c512 + 2)),
                  pl.BlockSpec((s, 256), const1), pl.BlockSpec((s, 256), const1),
                  pl.BlockSpec((n, 256), const1), pl.BlockSpec((n, 256), const1),
                  pl.BlockSpec((1, 512), const1), pl.BlockSpec((1, 512), const1),
                  pl.BlockSpec((256, 512), const1)],
        out_specs=[state_spec, state_spec],
        out_shape=[state_shape, state_shape],
        scratch_shapes=[pltpu.VMEM((RET_HEADS * RET_DK, GROUP_WIDTH), F32)],
        compiler_params=_cparams(("parallel",), 48),
    )(proj3, proj3, proj3, cos4, sin4, kdf, kdb, cdf, cdb, bd)

    k = min(RET_OUT_CHUNKS, c)
    const3 = lambda i, j: (0, 0, 0)
    const2 = lambda i, j: (0, 0)
    return pl.pallas_call(
        _ret_out_body,
        grid=(bsz, c // k),
        in_specs=[pl.BlockSpec((1, k * n, 256), lambda i, j: (i, j, c256)),
                  pl.BlockSpec((1, k * n, 256), lambda i, j: (i, j, c256 + 1)),
                  pl.BlockSpec((1, k * n, 256), lambda i, j: (i, j, c256 + 2)),
                  pl.BlockSpec((1, k * n, 256), lambda i, j: (i, j, c256 + 3)),
                  pl.BlockSpec((1, k * n, 512), lambda i, j: (i, j, c512 + 2)),
                  pl.BlockSpec((1, k * n, 512), lambda i, j: (i, j, c512 + 3)),
                  pl.BlockSpec((k * n, 256), lambda i, j: (j, 0)),
                  pl.BlockSpec((k * n, 256), lambda i, j: (j, 0)),
                  pl.BlockSpec((n, 256), const2), pl.BlockSpec((n, 256), const2),
                  pl.BlockSpec((RET_HEADS, n, n), const3),
                  pl.BlockSpec((1, k, 256, 512), lambda i, j: (i, j, 0, 0)),
                  pl.BlockSpec((1, k, 256, 512), lambda i, j: (i, j, 0, 0))],
        out_specs=pl.BlockSpec((1, k * n, GROUP_WIDTH), lambda i, j: (i, j, 0)),
        out_shape=jax.ShapeDtypeStruct((bsz, s, GROUP_WIDTH), F32),
        compiler_params=_cparams(("parallel", "parallel"), 32),
    )(proj3, proj3, proj3, proj3, proj3, proj3, cos4, sin4, qdf, qdb, dec, sf, sb)


def _ssd_decays(dt_rows, dtb_ref, a_ref, tri_ref):
    dt_raw = dt_rows + dtb_ref[...]
    dt = jnp.maximum(dt_raw, 0.0) + jnp.log1p(jnp.exp(-jnp.abs(dt_raw)))
    la = dt * a_ref[...]
    return dt, la, _dot_f32(tri_ref[...], la)


def _expand_heads(v, e):
    hi = v.astype(BF16)
    lo = (v - hi.astype(F32)).astype(BF16)
    return _dot(jnp.concatenate([hi, lo], axis=1), e)


def _ssd_state_body(nchunks, xbc_ref, dt_ref, dtb_ref, a_ref, tri_ref, e_ref, sf_ref, sb_ref, st_scr):
    n = SSD_CHUNK
    w = GROUP_WIDTH
    fwd_lane = lax.broadcasted_iota(jnp.int32, (1, LANES), 1) < SSD_HEADS

    def sweep(out_ref, lo, chunk_of):
        st_scr[...] = jnp.zeros_like(st_scr)

        def body(t, carry):
            c = chunk_of(t)
            rows = pl.ds(pl.multiple_of(c * n, n), n)
            dt, la, cs = _ssd_decays(dt_ref[0, rows, :], dtb_ref, a_ref, tri_ref)
            tot = cs[n - 1:n, :]
            wgt = dt * jnp.exp(jnp.where(fwd_lane, tot - cs, cs - la))
            both = _expand_heads(jnp.concatenate([wgt, jnp.broadcast_to(jnp.exp(tot), (SUBLANES, LANES))], axis=0),
                                 e_ref[:, lo:lo + w])
            out_ref[0, c] = st_scr[...]
            xw = (xbc_ref[0, rows, 0:w] * both[:n]).astype(BF16)
            new = []
            for g in range(SSD_GROUPS):
                bg = xbc_ref[0, rows, w + g * SSD_STATE:w + (g + 1) * SSD_STATE].astype(BF16)
                new.append(_dot_tn(bg, xw[:, g * 256:(g + 1) * 256]))
            st_scr[...] = st_scr[...] * both[n:n + 1] + jnp.concatenate(new, axis=1)
            return carry

        lax.fori_loop(0, nchunks, body, 0, unroll=4)

    sweep(sf_ref, 0, lambda t: t)
    sweep(sb_ref, w, lambda t: nchunks - 1 - t)


def _ssd_out_chunk(xbc, dt_rows, sf, sb, dtb_ref, a_ref, tri_ref, e_ref, dskip_ref):
    n = xbc.shape[0]
    w = GROUP_WIDTH
    dt, la, cs = _ssd_decays(dt_rows, dtb_ref, a_ref, tri_ref)
    ecs = cs - la
    fwd_lane = lax.broadcasted_iota(jnp.int32, (1, LANES), 1) < SSD_HEADS
    carry = jnp.exp(jnp.where(fwd_lane, cs, cs[n - 1:n, :] - ecs))
    both = _expand_heads(jnp.concatenate([dt, carry], axis=0), e_ref[...])
    dt_e = both[:n]
    carry_e = both[n:]
    xs = xbc[:, :w]
    xdt_f = xs * dt_e[:, :w]
    xdt_b = xs * dt_e[:, w:]
    cs_t = cs.T
    ecs_t = ecs.T
    ii = lax.broadcasted_iota(jnp.int32, (n, n), 0)
    jj = lax.broadcasted_iota(jnp.int32, (n, n), 1)
    low = lax.broadcasted_iota(jnp.int32, (1, LANES), 1) < SSD_HEADDIM
    neg = -1e30
    ydiag = []
    yoff = []
    for g in range(SSD_GROUPS):
        bg = xbc[:, w + g * SSD_STATE:w + (g + 1) * SSD_STATE].astype(BF16)
        cg = xbc[:, w + 256 + g * SSD_STATE:w + 256 + (g + 1) * SSD_STATE].astype(BF16)
        cb = _dot_nt(cg, bg)
        for pair in range(2):
            p = 2 * g + pair
            lhs = []
            for h in (2 * p, 2 * p + 1):
                lf = jnp.exp(jnp.where(ii >= jj, cs[:, h:h + 1] - cs_t[h:h + 1, :], neg))
                lhs.append((cb * lf).astype(BF16))
            for h in (2 * p, 2 * p + 1):
                hb = SSD_HEADS + h
                lb = jnp.exp(jnp.where(jj > ii, ecs_t[hb:hb + 1, :] - ecs[:, hb:hb + 1], neg))
                lhs.append((cb * lb).astype(BF16))
            xf = xdt_f[:, p * LANES:(p + 1) * LANES]
            xb = xdt_b[:, p * LANES:(p + 1) * LANES]
            rhs = jnp.concatenate([jnp.where(low, xf, 0.0), jnp.where(low, 0.0, xf),
                                   jnp.where(low, xb, 0.0), jnp.where(low, 0.0, xb)], axis=0).astype(BF16)
            ydiag.append(_dot(jnp.concatenate(lhs, axis=1), rhs))
        yoff.append(_dot(cg, sf[:, g * 256:(g + 1) * 256].astype(BF16)) * carry_e[:, g * 256:(g + 1) * 256]
                    + _dot(cg, sb[:, g * 256:(g + 1) * 256].astype(BF16)) * carry_e[:, w + g * 256:w + (g + 1) * 256])
    return jnp.concatenate(ydiag, axis=1) + jnp.concatenate(yoff, axis=1) + xs * dskip_ref[...]


def _ssd_out_body(xbc_ref, dt_ref, z_ref, dtb_ref, a_ref, tri_ref, e_ref, dskip_ref, nw_ref, sf_ref, sb_ref, o_ref):
    n = SSD_CHUNK

    def body(k, carry):
        rows = pl.ds(pl.multiple_of(k * n, n), n)
        y = _ssd_out_chunk(xbc_ref[0, rows, :], dt_ref[0, rows, :], sf_ref[0, k], sb_ref[0, k], dtb_ref, a_ref, tri_ref,
                           e_ref, dskip_ref)
        o_ref[0, rows, :] = _rms(y * _silu(z_ref[0, rows, :]), nw_ref[...])
        return carry

    lax.fori_loop(0, o_ref.shape[1] // n, body, 0, unroll=2)


def _ssd(proj3, xbc_act, dt_bias, a_log, d_skip, norm_w):
    bsz, s, _ = proj3.shape
    n = SSD_CHUNK
    c = s // n
    tri, e = _ssd_tables(n)
    pad = jnp.zeros((LANES - 2 * SSD_HEADS,), F32)
    dtb = jnp.concatenate([dt_bias.reshape(-1), pad]).reshape(1, LANES)
    a = jnp.concatenate([-jnp.exp(a_log.reshape(-1)), pad]).reshape(1, LANES)
    dskip = jnp.repeat(d_skip, SSD_HEADDIM).reshape(1, GROUP_WIDTH)
    cdt = COL_DT // LANES
    const1 = lambda i: (0, 0)

    state_shape = jax.ShapeDtypeStruct((bsz, c, SSD_STATE, GROUP_WIDTH), F32)
    state_spec = pl.BlockSpec((1, c, SSD_STATE, GROUP_WIDTH), lambda i: (i, 0, 0, 0))
    sf, sb = pl.pallas_call(
        functools.partial(_ssd_state_body, c),
        grid=(bsz,),
        in_specs=[pl.BlockSpec((1, s, SSD_CONV_CH), lambda i: (i, 0, 0)),
                  pl.BlockSpec((1, s, LANES), lambda i: (i, 0, cdt)),
                  pl.BlockSpec((1, LANES), const1), pl.BlockSpec((1, LANES), const1),
                  pl.BlockSpec((n, n), const1), pl.BlockSpec((2 * LANES, 2 * GROUP_WIDTH), const1)],
        out_specs=[state_spec, state_spec],
        out_shape=[state_shape, state_shape],
        scratch_shapes=[pltpu.VMEM((SSD_STATE, GROUP_WIDTH), F32)],
        compiler_params=_cparams(("parallel",), 48),
    )(xbc_act, proj3, dtb, a, tri, e)

    k = min(SSD_OUT_CHUNKS, c)
    const2 = lambda i, j: (0, 0)
    return pl.pallas_call(
        _ssd_out_body,
        grid=(bsz, c // k),
        in_specs=[pl.BlockSpec((1, k * n, SSD_CONV_CH), lambda i, j: (i, j, 0)),
                  pl.BlockSpec((1, k * n, LANES), lambda i, j: (i, j, cdt)),
                  pl.BlockSpec((1, k * n, GROUP_WIDTH), lambda i, j: (i, j, COL_Z // GROUP_WIDTH)),
                  pl.BlockSpec((1, LANES), const2), pl.BlockSpec((1, LANES), const2),
                  pl.BlockSpec((n, n), const2), pl.BlockSpec((2 * LANES, 2 * GROUP_WIDTH), const2),
                  pl.BlockSpec((1, GROUP_WIDTH), const2), pl.BlockSpec((1, GROUP_WIDTH), const2),
                  pl.BlockSpec((1, k, SSD_STATE, GROUP_WIDTH), lambda i, j: (i, j, 0, 0)),
                  pl.BlockSpec((1, k, SSD_STATE, GROUP_WIDTH), lambda i, j: (i, j, 0, 0))],
        out_specs=pl.BlockSpec((1, k * n, GROUP_WIDTH), lambda i, j: (i, j, 0)),
        out_shape=jax.ShapeDtypeStruct((bsz, s, GROUP_WIDTH), F32),
        compiler_params=_cparams(("parallel", "parallel"), 32),
    )(xbc_act, proj3, proj3, dtb, a, tri, e, dskip, norm_w.reshape(1, -1), sf, sb)


def _hy_filter_body(z_ref, w1_ref, b1_ref, w2_ref, b2_ref, w3_ref, fr_ref, dl_ref, hs_ref, hd_ref):
    tl = z_ref.shape[0]
    z = z_ref[...]
    hid = jnp.sin(fr_ref[0:1, :] * (_dot_f32(z, w1_ref[...]) + b1_ref[...]))
    hid = jnp.sin(fr_ref[1:2, :] * (_dot_f32(hid, w2_ref[...]) + b2_ref[...]))
    filt = _dot_f32(hid, w3_ref[...])
    dec = jnp.exp(-z[:, 0:1] * dl_ref[...])
    row = pl.program_id(0) * tl + lax.broadcasted_iota(jnp.int32, (tl, 1), 0)
    for o in range(HY_ORDER):
        base = o * 2 * HY_WIDTH
        hf = filt[:, base:base + HY_WIDTH] * dec
        hb = jnp.where(row == 0, 0.0, filt[:, base + HY_WIDTH:base + 2 * HY_WIDTH] * dec)
        hs_ref[:, o * HY_WIDTH:(o + 1) * HY_WIDTH] = hf + hb
        hd_ref[:, o * HY_WIDTH:(o + 1) * HY_WIDTH] = hf - hb


def _hy_filters(l, w1, b1, w2, b2, w3, freq):
    z, deltas = _hyena_positions(l)
    hid = HY_FILTER_HIDDEN
    w1p = jnp.zeros((LANES, LANES), F32).at[:HY_EMB, :hid].set(w1)
    w2p = jnp.zeros((LANES, LANES), F32).at[:hid, :hid].set(w2)
    w3p = jnp.zeros((LANES, w3.shape[1]), F32).at[:hid].set(w3)
    b1p = jnp.zeros((1, LANES), F32).at[0, :hid].set(b1)
    b2p = jnp.zeros((1, LANES), F32).at[0, :hid].set(b2)
    frp = jnp.zeros((2, LANES), F32).at[:, :hid].set(freq)
    tl = min(256, l)
    ncol = HY_ORDER * HY_WIDTH
    const = lambda i: (0, 0)
    return pl.pallas_call(
        _hy_filter_body,
        grid=(l // tl,),
        in_specs=[pl.BlockSpec((tl, LANES), lambda i: (i, 0)),
                  pl.BlockSpec((LANES, LANES), const), pl.BlockSpec((1, LANES), const),
                  pl.BlockSpec((LANES, LANES), const), pl.BlockSpec((1, LANES), const),
                  pl.BlockSpec((LANES, w3.shape[1]), const), pl.BlockSpec((2, LANES), const),
                  pl.BlockSpec((1, HY_WIDTH), const)],
        out_specs=[pl.BlockSpec((tl, ncol), lambda i: (i, 0)), pl.BlockSpec((tl, ncol), lambda i: (i, 0))],
        out_shape=[jax.ShapeDtypeStruct((l, ncol), F32), jax.ShapeDtypeStruct((l, ncol), F32)],
        compiler_params=_cparams(("parallel",), 32),
    )(z, w1p, b1p, w2p, b2p, w3p, frp, deltas)


def _short_conv_rows(x_ref, w_ref, b_ref, r0, n):
    l = x_ref.shape[1]
    lo = max(r0 - SUBLANES, 0)
    hi = min(r0 + n + SUBLANES, l)
    win = x_ref[0, lo:hi, :]
    pad = jnp.zeros((SUBLANES, win.shape[1]), F32)
    if r0 == 0:
        win = jnp.concatenate([pad, win], axis=0)
    if r0 + n == l:
        win = jnp.concatenate([win, pad], axis=0)
    rows = n + 2 * SUBLANES
    prev = pltpu.roll(win, 1, axis=0)[SUBLANES:SUBLANES + n]
    nxt = pltpu.roll(win, rows - 1, axis=0)[SUBLANES:SUBLANES + n]
    return prev * w_ref[0:1, :] + win[SUBLANES:SUBLANES + n] * w_ref[1:2, :] + nxt * w_ref[2:3, :] + b_ref[...]


def _hy_conv_body(u0_ref, u1_ref, u2_ref, cw0_ref, cw1_ref, cw2_ref, cb0_ref, cb1_ref, cb2_ref, c_ref, s_ref,
                  hr0_ref, hi0_ref, hr1_ref, hi1_ref, b0_ref, b1_ref, o_ref, xb_scr, pre_scr, pim_scr, z_scr):
    l = u0_ref.shape[1]
    tf = min(HY_ROW_TILE, l)
    ntile = l // tf
    sgn = jnp.where(lax.broadcasted_iota(jnp.int32, (l, 1), 0) % 2 == 0, 1.0, -1.0)

    def long_conv(gate_ref, gw_ref, gb_ref, hr_ref, hi_ref, b_ref, out_ref):
        x = z_scr[...]
        xb_scr[...] = x.astype(BF16)
        x_nyq = jnp.sum(x * sgn, axis=0, keepdims=True) * hr_ref[l:l + 1, :]
        for r in range(ntile):
            rows = pl.ds(r * tf, tf)
            xc = _dot(c_ref[rows, :], xb_scr[...])
            xs = _dot(s_ref[rows, :], xb_scr[...])
            hre = hr_ref[rows, :]
            him = hi_ref[rows, :]
            pre_scr[rows, :] = (xc * hre + xs * him).astype(BF16)
            pim_scr[rows, :] = (xc * him - xs * hre).astype(BF16)
        for r in range(ntile):
            rows = pl.ds(r * tf, tf)
            y = _dot(c_ref[rows, :], pre_scr[...]) - _dot(s_ref[rows, :], pim_scr[...])
            y = y + sgn[r * tf:(r + 1) * tf] * x_nyq
            gate = _short_conv_rows(gate_ref, gw_ref, gb_ref, r * tf, tf)
            out_ref[rows, :] = gate * (y + z_scr[rows, :] * b_ref[0])

    for r in range(ntile):
        z_scr[pl.ds(r * tf, tf), :] = _short_conv_rows(u0_ref, cw0_ref, cb0_ref, r * tf, tf)
    long_conv(u1_ref, cw1_ref, cb1_ref, hr0_ref, hi0_ref, b0_ref, z_scr)
    long_conv(u2_ref, cw2_ref, cb2_ref, hr1_ref, hi1_ref, b1_ref, o_ref.at[0])


def _hy_long_conv(proj3, conv_w, conv_b, cmat, smat, hre, him, bias):
    bsz, l, _ = proj3.shape
    tc = 256
    nb = HY_WIDTH // tc
    off = COL_HY // tc
    const = lambda j, i: (0, 0)
    single = pl.Buffered(1)
    bias3 = bias.reshape(HY_ORDER, 1, HY_WIDTH)
    conv_b2 = conv_b.reshape(1, -1)

    def part(k):
        return [pl.BlockSpec((1, l, tc), lambda j, i: (i, 0, off + k * nb + j))]

    def part_w(k):
        return [pl.BlockSpec((HY_SHORT, tc), lambda j, i: (0, k * nb + j))]

    def part_b(k):
        return [pl.BlockSpec((1, tc), lambda j, i: (0, k * nb + j))]

    return pl.pallas_call(
        _hy_conv_body,
        grid=(nb, bsz),
        in_specs=part(0) + part(1) + part(2) + part_w(0) + part_w(1) + part_w(2) + part_b(0) + part_b(1) + part_b(2) + [
            pl.BlockSpec((l, l), const, pipeline_mode=single),
            pl.BlockSpec((l, l), const, pipeline_mode=single),
            pl.BlockSpec((l + SUBLANES, tc), lambda j, i: (0, j), pipeline_mode=single),
            pl.BlockSpec((l, tc), lambda j, i: (0, j), pipeline_mode=single),
            pl.BlockSpec((l + SUBLANES, tc), lambda j, i: (0, nb + j), pipeline_mode=single),
            pl.BlockSpec((l, tc), lambda j, i: (0, nb + j), pipeline_mode=single),
            pl.BlockSpec((1, 1, tc), lambda j, i: (0, 0, j)),
            pl.BlockSpec((1, 1, tc), lambda j, i: (1, 0, j))],
        out_specs=pl.BlockSpec((1, l, tc), lambda j, i: (i, 0, j)),
        out_shape=jax.ShapeDtypeStruct((bsz, l, HY_WIDTH), F32),
        scratch_shapes=[pltpu.VMEM((l, tc), BF16), pltpu.VMEM((l, tc), BF16), pltpu.VMEM((l, tc), BF16),
                        pltpu.VMEM((l, tc), F32)],
        compiler_params=_cparams(("parallel", "parallel"), 56),
    )(proj3, proj3, proj3, conv_w, conv_w, conv_w, conv_b2, conv_b2, conv_b2, cmat, smat, hre, him, hre, him,
      bias3, bias3)


def _hyena(proj3, conv_w, conv_b, w1, b1, w2, b2, w3, freq, bias):
    l = proj3.shape[1]
    hs, hd = _hy_filters(l, w1, b1, w2, b2, w3, freq)
    cmat, smat, c_sc, s_sc_neg = _dft_tables(l)
    tn = 256
    hre = _matmul(c_sc, hs.astype(BF16), l + SUBLANES, tn)
    him = _matmul(s_sc_neg, hd.astype(BF16), l, tn)
    return _hy_long_conv(proj3, conv_w, conv_b, cmat, smat, hre, him, bias)


def _out_proj_body(a_ref, b_ref, c_ref, d_ref, x_ref, w_ref, dn_ref, g_ref, beta_ref, o_ref, *rows_ref):
    gw = GROUP_WIDTH
    h = _dot(a_ref[...].astype(BF16), w_ref[0:gw, :])
    h += _dot(b_ref[...].astype(BF16), w_ref[gw:2 * gw, :])
    h += _dot(c_ref[...].astype(BF16), w_ref[2 * gw:3 * gw, :])
    h += _dot(_rms(d_ref[...], dn_ref[...]).astype(BF16), w_ref[3 * gw:4 * gw, :])
    y = _layer_norm(ALPHA * x_ref[...] + h, g_ref[...], beta_ref[...])
    o_ref[...] = y
    if rows_ref:
        _store_row_tiles(rows_ref[0], y)


def _store_row_tiles(dst_ref, y):
    rows = y.shape[0]
    for c in range(ROW_CHUNKS):
        dst_ref[pl.ds(c, rows, stride=ROW_CHUNKS), :] = y[:, c * LANES:(c + 1) * LANES]


def _load_row_tiles(src_ref, first, rows):
    return jnp.concatenate([src_ref[pl.ds(first * ROW_CHUNKS + c, rows, stride=ROW_CHUNKS), :]
                            for c in range(ROW_CHUNKS)], axis=1)


def _out_proj_ln(a, b, c, d, x, w_out, hy_norm, g, beta, emit_row_tiles):
    m = x.shape[0]
    tm = min(512, m)
    gw = GROUP_WIDTH
    const = lambda i: (0, 0)
    row = lambda i: (i, 0)
    out_specs = [pl.BlockSpec((tm, D_MODEL), row)]
    out_shape = [jax.ShapeDtypeStruct((m, D_MODEL), F32)]
    if emit_row_tiles:
        out_specs.append(pl.BlockSpec((tm * ROW_CHUNKS, LANES), row))
        out_shape.append(jax.ShapeDtypeStruct((m * ROW_CHUNKS, LANES), F32))
    return pl.pallas_call(
        _out_proj_body,
        grid=(m // tm,),
        in_specs=[pl.BlockSpec((tm, gw), row), pl.BlockSpec((tm, gw), row), pl.BlockSpec((tm, gw), row),
                  pl.BlockSpec((tm, gw), row), pl.BlockSpec((tm, D_MODEL), row),
                  pl.BlockSpec((D_MODEL, D_MODEL), const), pl.BlockSpec((1, gw), const),
                  pl.BlockSpec((1, D_MODEL), const), pl.BlockSpec((1, D_MODEL), const)],
        out_specs=out_specs,
        out_shape=out_shape,
        compiler_params=_cparams(("parallel",), 48),
    )(a, b, c, d, x, w_out, hy_norm.reshape(1, -1), g.reshape(1, -1), beta.reshape(1, -1))


def _ffn_body(x_ref, wg_ref, wu_ref, wd_ref, g_ref, beta_ref, o_ref, xb_ref, acc_ref):
    j = pl.program_id(1)

    @pl.when(j == 0)
    def _():
        xb_ref[...] = x_ref[...].astype(BF16)
        acc_ref[...] = jnp.zeros_like(acc_ref)

    xb = xb_ref[...]
    hidden = (_silu(_dot(xb, wg_ref[...])) * _dot(xb, wu_ref[...])).astype(BF16)
    acc_ref[...] += _dot(hidden, wd_ref[...])

    @pl.when(j == pl.num_programs(1) - 1)
    def _():
        o_ref[...] = _layer_norm(ALPHA * x_ref[...] + acc_ref[...], g_ref[...], beta_ref[...])


def _ffn_ln(x, wg, wu, wd, g, beta):
    m = x.shape[0]
    dff = wg.shape[1]
    tm = min(512, m)
    tf = 512
    const = lambda i, j: (0, 0)
    return pl.pallas_call(
        _ffn_body,
        grid=(m // tm, dff // tf),
        in_specs=[pl.BlockSpec((tm, D_MODEL), lambda i, j: (i, 0)),
                  pl.BlockSpec((D_MODEL, tf), lambda i, j: (0, j)),
                  pl.BlockSpec((D_MODEL, tf), lambda i, j: (0, j)),
                  pl.BlockSpec((tf, D_MODEL), lambda i, j: (j, 0)),
                  pl.BlockSpec((1, D_MODEL), const), pl.BlockSpec((1, D_MODEL), const)],
        out_specs=pl.BlockSpec((tm, D_MODEL), lambda i, j: (i, 0)),
        out_shape=jax.ShapeDtypeStruct((m, D_MODEL), F32),
        scratch_shapes=[pltpu.VMEM((tm, D_MODEL), BF16), pltpu.VMEM((tm, D_MODEL), F32)],
        compiler_params=_cparams(("parallel", "arbitrary"), 48),
    )(x, wg, wu, wd, g.reshape(1, -1), beta.reshape(1, -1))


def _router_body(x_ref, w_ref, idx_ref, gate_ref):
    x = x_ref[...]
    lane = lax.broadcasted_iota(jnp.int32, (x.shape[0], LANES), 1)
    neg = -jnp.inf
    logits = jnp.full((x.shape[0], LANES), neg, F32)
    for e in range(N_EXPERTS):
        logits = jnp.where(lane == e, jnp.sum(x * w_ref[e:e + 1, :], axis=-1, keepdims=True), logits)
    m1 = jnp.max(logits, axis=-1, keepdims=True)
    i1 = jnp.min(jnp.where(logits == m1, lane, LANES), axis=-1, keepdims=True)
    rest = jnp.where(lane == i1, neg, logits)
    m2 = jnp.max(rest, axis=-1, keepdims=True)
    i2 = jnp.min(jnp.where(rest == m2, lane, LANES), axis=-1, keepdims=True)
    e2 = jnp.exp(m2 - m1)
    den = 1.0 + e2
    idx_ref[...] = jnp.where(lane == 0, i1, jnp.where(lane == 1, i2, 0))
    gate_ref[...] = jnp.where(lane == 0, 1.0 / den, jnp.where(lane == 1, e2 / den, 0.0))


def _router(x, w_router):
    m = x.shape[0]
    tm = min(512, m)
    return pl.pallas_call(
        _router_body,
        grid=(m // tm,),
        in_specs=[pl.BlockSpec((tm, D_MODEL), lambda i: (i, 0)), pl.BlockSpec((N_EXPERTS, D_MODEL), lambda i: (0, 0))],
        out_specs=[pl.BlockSpec((tm, LANES), lambda i: (i, 0)), pl.BlockSpec((tm, LANES), lambda i: (i, 0))],
        out_shape=[jax.ShapeDtypeStruct((m, LANES), jnp.int32), jax.ShapeDtypeStruct((m, LANES), F32)],
        compiler_params=_cparams(("parallel",), 32),
    )(x, w_router.T)


def _row_copy(src_ref, src_row, buf_ref, buf_row, sem):
    src = src_ref.at[pl.ds(pl.multiple_of(src_row * ROW_CHUNKS, ROW_CHUNKS), ROW_CHUNKS)]
    dst = buf_ref.at[pl.ds(pl.multiple_of(buf_row * ROW_CHUNKS, ROW_CHUNKS), ROW_CHUNKS)]
    return pltpu.make_async_copy(src, dst, sem)


def _start_row_gather(idx_ref, n, src_ref, buf_ref, sem):
    def start(h, carry):
        for p in range(2):
            r = 2 * h + p
            _row_copy(src_ref, idx_ref[0, 0, r], buf_ref, r, sem).start(priority=p)
        return carry

    lax.fori_loop(0, n // 2, start, 0, unroll=DMA_LOOP_UNROLL // 2)


def _wait_row_gather(n, src_ref, buf_ref, sem):
    def wait(r, carry):
        _row_copy(src_ref, 0, buf_ref, r, sem).wait()
        return carry

    lax.fori_loop(0, n, wait, 0, unroll=DMA_LOOP_UNROLL)


def _pipelined_row_gather(idx_cur_ref, idx_next_ref, n, src_ref, buf, sems):
    i = pl.program_id(0)
    slot = i % 2

    @pl.when(i == 0)
    def _():
        _start_row_gather(idx_cur_ref, n, src_ref, buf.at[0], sems.at[0])

    @pl.when(i + 1 < pl.num_programs(0))
    def _():
        _start_row_gather(idx_next_ref, n, src_ref, buf.at[1 - slot], sems.at[1 - slot])

    _wait_row_gather(n, src_ref, buf.at[slot], sems.at[slot])
    return slot


def _idx_specs(n, nblk):
    return [pl.BlockSpec((1, 1, n), lambda i: (i, 0, 0), memory_space=pltpu.SMEM),
            pl.BlockSpec((1, 1, n), lambda i: (jnp.minimum(i + 1, nblk - 1), 0, 0), memory_space=pltpu.SMEM)]


def _gather_x_body(idx_cur_ref, idx_next_ref, src_ref, o_ref, buf, sems):
    rows = o_ref.shape[0]
    slot = _pipelined_row_gather(idx_cur_ref, idx_next_ref, rows, src_ref, buf, sems)
    o_ref[...] = _load_row_tiles(buf.at[slot], 0, rows).astype(BF16)


def _gather_x(x_tiles, slot_tok):
    cap = slot_tok.shape[0]
    rows = GATHER_ROWS
    assert cap % rows == 0
    nblk = cap // rows
    idx = slot_tok.reshape(nblk, 1, rows)
    return pl.pallas_call(
        _gather_x_body,
        grid=(nblk,),
        in_specs=_idx_specs(rows, nblk) + [pl.BlockSpec(memory_space=pl.ANY)],
        out_specs=pl.BlockSpec((rows, D_MODEL), lambda i: (i, 0)),
        out_shape=jax.ShapeDtypeStruct((cap, D_MODEL), BF16),
        scratch_shapes=[pltpu.VMEM((2, rows * ROW_CHUNKS, LANES), F32), pltpu.SemaphoreType.DMA((2,))],
        compiler_params=pltpu.CompilerParams(dimension_semantics=("arbitrary",), disable_bounds_checks=True),
    )(idx, idx, x_tiles)


def _expert_body(be_ref, nv_ref, x_ref, wg_ref, wu_ref, wd_ref, o_ref, acc_ref):
    i = pl.program_id(0)
    j = pl.program_id(1)
    half = x_ref.shape[0] // 2

    @pl.when(j == 0)
    def _():
        acc_ref[...] = jnp.zeros_like(acc_ref)

    def swiglu_rows(r0):
        xb = x_ref[r0:r0 + half, :]
        hidden = (_silu(_dot(xb, wg_ref[0])) * _dot(xb, wu_ref[0])).astype(BF16)
        acc_ref[r0:r0 + half, :] += _dot(hidden, wd_ref[0])

    @pl.when(nv_ref[i] > 0)
    def _():
        swiglu_rows(0)

    @pl.when(nv_ref[i] > half)
    def _():
        swiglu_rows(half)

    @pl.when(j == pl.num_programs(1) - 1)
    def _():
        _store_row_tiles(o_ref, acc_ref[...])


def _expert_ffn(x_slots, block_expert, block_valid, wg, wu, wd, tm):
    cap = x_slots.shape[0]
    dff = wg.shape[2]
    tf = 512

    def ff_tile(i, j, nv):
        return jnp.where(nv[i] > 0, j, 0)

    grid_spec = pltpu.PrefetchScalarGridSpec(
        num_scalar_prefetch=2,
        grid=(cap // tm, dff // tf),
        in_specs=[pl.BlockSpec((tm, D_MODEL), lambda i, j, be, nv: (i, 0)),
                  pl.BlockSpec((1, D_MODEL, tf), lambda i, j, be, nv: (be[i], 0, ff_tile(i, j, nv))),
                  pl.BlockSpec((1, D_MODEL, tf), lambda i, j, be, nv: (be[i], 0, ff_tile(i, j, nv))),
                  pl.BlockSpec((1, tf, D_MODEL), lambda i, j, be, nv: (be[i], ff_tile(i, j, nv), 0))],
        out_specs=pl.BlockSpec((tm * ROW_CHUNKS, LANES), lambda i, j, be, nv: (i, 0)),
        scratch_shapes=[pltpu.VMEM((tm, D_MODEL), F32)],
    )
    return pl.pallas_call(
        _expert_body,
        grid_spec=grid_spec,
        out_shape=jax.ShapeDtypeStruct((cap * ROW_CHUNKS, LANES), F32),
        compiler_params=_cparams(("parallel", "arbitrary"), 56),
    )(block_expert, block_valid, x_slots, wg, wu, wd)


def _combine_body(idx_cur_ref, idx_next_ref, x_ref, gt_ref, g_ref, beta_ref, y_ref, o_ref, buf, sems):
    rows = x_ref.shape[0]
    slot = _pipelined_row_gather(idx_cur_ref, idx_next_ref, TOP_K * rows, y_ref, buf, sems)
    gt = gt_ref[...]
    f = (gt[:, 0:1] * _load_row_tiles(buf.at[slot], 0, rows)
         + gt[:, 1:2] * _load_row_tiles(buf.at[slot], rows, rows))
    o_ref[...] = _layer_norm(ALPHA * x_ref[...] + f, g_ref[...], beta_ref[...])


def _combine_ln(x, y_tiles, pos, gates, g, beta):
    m = x.shape[0]
    rows = min(COMBINE_ROWS, m)
    nblk = m // rows
    n = TOP_K * rows
    idx = pos.reshape(nblk, rows, TOP_K).transpose(0, 2, 1).reshape(nblk, 1, n)
    const = lambda i: (0, 0)
    return pl.pallas_call(
        _combine_body,
        grid=(nblk,),
        in_specs=_idx_specs(n, nblk) + [
            pl.BlockSpec((rows, D_MODEL), lambda i: (i, 0)),
            pl.BlockSpec((rows, LANES), lambda i: (i, 0)),
            pl.BlockSpec((1, D_MODEL), const), pl.BlockSpec((1, D_MODEL), const),
            pl.BlockSpec(memory_space=pl.ANY)],
        out_specs=pl.BlockSpec((rows, D_MODEL), lambda i: (i, 0)),
        out_shape=jax.ShapeDtypeStruct((m, D_MODEL), F32),
        scratch_shapes=[pltpu.VMEM((2, n * ROW_CHUNKS, LANES), F32), pltpu.SemaphoreType.DMA((2,))],
        compiler_params=pltpu.CompilerParams(dimension_semantics=("arbitrary",), disable_bounds_checks=True,
                                             vmem_limit_bytes=48 * 1024 * 1024),
    )(idx, idx, x, gates, g.reshape(1, -1), beta.reshape(1, -1), y_tiles)


def _moe_ln(x, x_tiles, w_router, wg, wu, wd, g, beta, tm):
    m = x.shape[0]
    n_asg = m * TOP_K
    idx, gates = _router(x, w_router)
    e_flat = idx[:, :TOP_K].reshape(-1)
    onehot = (e_flat[:, None] == jnp.arange(N_EXPERTS, dtype=jnp.int32)[None, :]).astype(jnp.int32)
    csum = jnp.cumsum(onehot, axis=0)
    counts = csum[-1]
    rank = jnp.sum(csum * onehot, axis=1) - 1
    padded = (counts + tm - 1) // tm * tm
    pend = jnp.cumsum(padded)
    pstart = pend - padded
    dest = (jnp.sum(pstart[None, :] * onehot, axis=1) + rank).astype(jnp.int32)
    cap = n_asg + N_EXPERTS * tm
    tok = jnp.arange(n_asg, dtype=jnp.int32) // TOP_K
    slot_tok = jnp.zeros((cap,), jnp.int32).at[dest].set(tok, unique_indices=True)
    nblk = cap // tm
    block_start = jnp.arange(nblk, dtype=pend.dtype) * tm
    block_expert = jnp.minimum(jnp.searchsorted(pend, block_start, side='right'), N_EXPERTS - 1).astype(jnp.int32)
    block_valid = jnp.clip((pstart + counts)[block_expert] - block_start, 0, tm).astype(jnp.int32)

    x_slots = _gather_x(x_tiles, slot_tok)
    y_tiles = _expert_ffn(x_slots, block_expert, block_valid, wg, wu, wd, tm)
    return _combine_ln(x, y_tiles, dest.reshape(m, TOP_K), gates, g, beta)


def _mixer_ln(x2, bsz, s, p, emit_row_tiles):
    proj = _matmul(x2, p['w_in'], min(1024, x2.shape[0]), PROJ_TILE)
    proj3 = proj.reshape(bsz, s, PROJ_COLS)
    cos_slab, sin_slab, cos4, sin4 = _rope_tables(s)
    out_a = _mla(proj3, cos_slab, sin_slab, p['mla_q_norm'], p['w_q'], p['w_qr'], p['mla_kv_norm'], p['w_kv'],
                 p['mla_out_norm'])
    out_b = _retention(proj3, cos4, sin4)
    xbc_act = _dwconv(proj3, COL_XBC, SSD_CONV_CH, p['ssd_conv_w'], p['ssd_conv_b'], act=True)
    out_c = _ssd(proj3, xbc_act, p['ssd_dt_bias'], p['ssd_a_log'], p['ssd_d'], p['ssd_norm'])
    out_d = _hyena(proj3, p['hy_conv_w'], p['hy_conv_b'], p['hy_w1'], p['hy_b1'], p['hy_w2'], p['hy_b2'], p['hy_w3'],
                   p['hy_freq'], p['hy_bias'])
    m = bsz * s
    gw = GROUP_WIDTH
    return _out_proj_ln(out_a.reshape(m, gw), out_b.reshape(m, gw), out_c.reshape(m, gw), out_d.reshape(m, gw),
                        x2, p['w_out'], p['hy_out_norm'], p['ln1_g'], p['ln1_b'], emit_row_tiles)


_MIXER_KEYS = ('mla_q_norm', 'mla_kv_norm', 'mla_out_norm', 'ssd_conv_w', 'ssd_conv_b', 'ssd_dt_bias', 'ssd_a_log',
               'ssd_d', 'ssd_norm', 'hy_conv_w', 'hy_conv_b', 'hy_w1', 'hy_b1', 'hy_w2', 'hy_b2', 'hy_w3', 'hy_freq',
               'hy_bias', 'hy_out_norm', 'ln1_g', 'ln1_b')

MOE_TM = 1024


def kernel(x, w_in, mla_q_norm, mla_w_uq, mla_kv_norm, mla_w_ukv, mla_out_norm, ssd_conv_w, ssd_conv_b, ssd_dt_bias, ssd_a_log, ssd_d, ssd_norm, hy_conv_w, hy_conv_b, hy_w1, hy_b1, hy_w2, hy_b2, hy_w3, hy_freq, hy_bias, hy_out_norm, w_out, ln1_g, ln1_b, ln2_g, ln2_b, ffn_w_gate, ffn_w_up, ffn_w_down, moe_router, moe_w_gate, moe_w_up, moe_w_down):
    args = dict(locals())
    bsz, s, d = x.shape
    x2 = x.reshape(bsz * s, d)
    for layer in range(DEPTH):
        p = {k: args[k][layer] for k in _MIXER_KEYS}
        p['w_in'] = _prep_w_in(w_in[layer])
        p['w_q'], p['w_qr'] = _prep_w_uq(mla_w_uq[layer])
        p['w_kv'] = _prep_w_ukv(mla_w_ukv[layer])
        p['w_out'] = w_out[layer].astype(BF16)
        j = layer // 2
        if layer % 2 == 0:
            x2, = _mixer_ln(x2, bsz, s, p, False)
            x2 = _ffn_ln(x2, ffn_w_gate[j].astype(BF16), ffn_w_up[j].astype(BF16), ffn_w_down[j].astype(BF16),
                         ln2_g[layer], ln2_b[layer])
        else:
            x2, x_tiles = _mixer_ln(x2, bsz, s, p, True)
            x2 = _moe_ln(x2, x_tiles, moe_router[j], moe_w_gate[j].astype(BF16), moe_w_up[j].astype(BF16),
                         moe_w_down[j].astype(BF16), ln2_g[layer], ln2_b[layer], MOE_TM)
    return x2.reshape(bsz, s, d)
```
